```python
import math
import jax, jax.numpy as jnp
from jax import lax
import numpy as np

D_MODEL = 1024
BATCH = 4
SEQ = 8192
DEPTH = 2

GRID_W = 64
CTX_LEN = 256
HEAD_DIM = 64
N_HEADS = D_MODEL // HEAD_DIM
HEADS_A = N_HEADS // 2
HEADS_B = N_HEADS - HEADS_A
GQA_GROUP = 4
KV_A = HEADS_A // GQA_GROUP
KV_B = HEADS_B // GQA_GROUP
WINDOW = 128
Q_BLOCK = 128
HEADS_C = D_MODEL // (2 * HEAD_DIM)
FFN_HIDDEN = ((8 * D_MODEL // 3 + 255) // 256) * 256
ROPE_THETA = 10000.0
EPS = 1e-6
NEG_INF = -1e30
N_EVEN = (DEPTH + 1) // 2
N_ODD = DEPTH // 2
EVEN_IN = (HEADS_A + 2 * KV_A + HEADS_B + 2 * KV_B) * HEAD_DIM
EVEN_OUT = (HEADS_A + HEADS_B) * HEAD_DIM
ODD_IN = 3 * 2 * HEADS_C * HEAD_DIM
ODD_OUT = HEADS_C * 2 * HEAD_DIM

kernel_name = "hybrid_dit_window_axial_diff_prefix"


def _rms(x, w):
    xf = x.astype(jnp.float32)
    y = xf * lax.rsqrt(jnp.mean(xf * xf, axis=-1, keepdims=True) + EPS)
    return (y * w.astype(jnp.float32)).astype(x.dtype)


def _axial_tables(rows):
    row = jnp.repeat(jnp.arange(rows, dtype=jnp.float32), GRID_W)
    col = jnp.tile(jnp.arange(GRID_W, dtype=jnp.float32), rows)
    n_freq = HEAD_DIM // 4
    inv = ROPE_THETA ** (-jnp.arange(n_freq, dtype=jnp.float32) / n_freq)
    ang = jnp.concatenate([row[:, None] * inv, col[:, None] * inv], axis=-1)
    return jnp.cos(ang), jnp.sin(ang)


def _rope(x, cos, sin):
    n = x.shape[1]
    bshape = (1, n) + (1,) * (x.ndim - 3) + (HEAD_DIM // 2,)
    c = cos.reshape(bshape).astype(x.dtype)
    s = sin.reshape(bshape).astype(x.dtype)
    xp = x.reshape(*x.shape[:-1], HEAD_DIM // 2, 2)
    x0, x1 = xp[..., 0], xp[..., 1]
    return jnp.stack([x0 * c - x1 * s, x0 * s + x1 * c], axis=-1).reshape(x.shape)


def _softmax(s, sink):
    if sink is None:
        return jax.nn.softmax(s, axis=-1)
    m = jnp.maximum(jnp.max(s, axis=-1, keepdims=True), sink)
    p = jnp.exp(s - m)
    return p / (jnp.sum(p, axis=-1, keepdims=True) + jnp.exp(sink - m))


def _to_blocks(q):
    b, n = q.shape[:2]
    return jnp.moveaxis(q.reshape(b, n // Q_BLOCK, Q_BLOCK, *q.shape[2:]), 1, 0)


def _from_blocks(o):
    o = jnp.moveaxis(o, 0, 1)
    return o.reshape(o.shape[0], o.shape[1] * o.shape[2], -1)


def _window_attend(q, k_lat, v_lat, k_ctx, v_ctx, sink):
    n = q.shape[1]
    n_ctx = k_ctx.shape[1]
    span = Q_BLOCK + 2 * WINDOW
    pad = ((0, 0), (WINDOW, WINDOW), (0, 0), (0, 0))
    k_pad = jnp.pad(k_lat, pad)
    v_pad = jnp.pad(v_lat, pad)
    scale = HEAD_DIM ** -0.5
    sink_b = sink.astype(jnp.float32)[None, :, :, None, None]

    def one_block(args):
        qb, i = args
        start = i * Q_BLOCK
        kw = lax.dynamic_slice_in_dim(k_pad, start, span, axis=1)
        vw = lax.dynamic_slice_in_dim(v_pad, start, span, axis=1)
        qpos = start + jnp.arange(Q_BLOCK)
        kpos = start - WINDOW + jnp.arange(span)
        ok = (jnp.abs(qpos[:, None] - kpos[None, :]) <= WINDOW) & (kpos[None, :] >= 0) & (kpos[None, :] < n)
        s_c = jnp.einsum('bqhgd,bkhd->bhgqk', qb, k_ctx).astype(jnp.float32) * scale
        s_w = jnp.einsum('bqhgd,bkhd->bhgqk', qb, kw).astype(jnp.float32) * scale
        s_w = jnp.where(ok, s_w, NEG_INF)
        p = _softmax(jnp.concatenate([s_c, s_w], axis=-1), sink_b).astype(v_lat.dtype)
        return (jnp.einsum('bhgqk,bkhd->bqhgd', p[..., :n_ctx], v_ctx)
                + jnp.einsum('bhgqk,bkhd->bqhgd', p[..., n_ctx:], vw))

    out = lax.map(one_block, (_to_blocks(q), jnp.arange(n // Q_BLOCK)))
    return _from_blocks(out)


def _dense_attend(q, k, v, sink):
    scale = HEAD_DIM ** -0.5
    sink_b = None if sink is None else sink.astype(jnp.float32)[None, :, :, None, None]

    def one_block(qb):
        s = jnp.einsum('bqhgd,bkhd->bhgqk', qb, k).astype(jnp.float32) * scale
        p = _softmax(s, sink_b).astype(v.dtype)
        return jnp.einsum('bhgqk,bkhd->bqhgd', p, v)

    return _from_blocks(lax.map(one_block, _to_blocks(q)))


def _diff_attend(q, k, v, lam):
    scale = HEAD_DIM ** -0.5

    def one_block(qb):
        s = jnp.einsum('bqhad,bkhad->bhaqk', qb, k).astype(jnp.float32) * scale
        p = jax.nn.softmax(s, axis=-1)
        a = (p[:, :, 0] - lam * p[:, :, 1]).astype(v.dtype)
        return jnp.einsum('bhqk,bkhe->bqhe', a, v)

    out = lax.map(one_block, _to_blocks(q))
    out = jnp.moveaxis(out, 0, 1)
    return out.reshape(out.shape[0], out.shape[1] * out.shape[2], *out.shape[3:])


def _even_mixer(hx, hc, w_in, w_out, qn_a, kn_a, qn_b, kn_b, sink_a, cos, sin, with_ctx):
    d = HEAD_DIM
    cuts = [int(v) for v in np.cumsum([HEADS_A * d, KV_A * d, KV_A * d, HEADS_B * d, KV_B * d])]

    def heads(p, use_rope):
        b, n = p.shape[:2]
        qa, ka, va, qb, kb, vb = jnp.split(p, cuts, axis=-1)
        qa = _rms(qa.reshape(b, n, HEADS_A, d), qn_a)
        ka = _rms(ka.reshape(b, n, KV_A, d), kn_a)
        qb = _rms(qb.reshape(b, n, HEADS_B, d), qn_b)
        kb = _rms(kb.reshape(b, n, KV_B, d), kn_b)
        if use_rope:
            qa, ka, qb, kb = _rope(qa, cos, sin), _rope(ka, cos, sin), _rope(qb, cos, sin), _rope(kb, cos, sin)
        return (qa.reshape(b, n, KV_A, GQA_GROUP, d), ka, va.reshape(b, n, KV_A, d),
                qb.reshape(b, n, KV_B, GQA_GROUP, d), kb, vb.reshape(b, n, KV_B, d))

    qa_x, ka_x, va_x, qb_x, kb_x, vb_x = heads(hx @ w_in, True)
    qa_c, ka_c, va_c, qb_c, kb_c, vb_c = heads(hc @ w_in, False)
    sink = sink_a.reshape(KV_A, GQA_GROUP)
    oa_x = _window_attend(qa_x, ka_x, va_x, ka_c, va_c, sink)
    ob_x = _dense_attend(qb_x, jnp.concatenate([kb_c, kb_x], axis=1),
                         jnp.concatenate([vb_c, vb_x], axis=1), None)
    out_x = jnp.concatenate([oa_x, ob_x], axis=-1) @ w_out
    out_c = None
    if with_ctx:
        oa_c = _dense_attend(qa_c, ka_c, va_c, sink)
        ob_c = _dense_attend(qb_c, kb_c, vb_c, None)
        out_c = jnp.concatenate([oa_c, ob_c], axis=-1) @ w_out
    return out_x, out_c


def _odd_mixer(hx, hc, w_in, w_out, qn, kn, lq1, lk1, lq2, lk2, subln_w, lam_init, cos, sin, with_ctx):
    d = HEAD_DIM
    qs = 2 * HEADS_C * d

    def heads(p, use_rope):
        b, n = p.shape[:2]
        q, k, v = jnp.split(p, [qs, 2 * qs], axis=-1)
        q = _rms(q.reshape(b, n, HEADS_C, 2, d), qn)
        k = _rms(k.reshape(b, n, HEADS_C, 2, d), kn)
        if use_rope:
            q, k = _rope(q, cos, sin), _rope(k, cos, sin)
        return q, k, v.reshape(b, n, HEADS_C, 2 * d)

    q_x, k_x, v_x = heads(hx @ w_in, True)
    q_c, k_c, v_c = heads(hc @ w_in, False)
    f32 = jnp.float32
    lam = (jnp.exp(jnp.sum(lq1.astype(f32) * lk1.astype(f32)))
           - jnp.exp(jnp.sum(lq2.astype(f32) * lk2.astype(f32))) + lam_init)

    def finish(o):
        o = _rms(o, subln_w) * (1.0 - lam_init)
        return o.reshape(o.shape[0], o.shape[1], -1) @ w_out

    o_x = _diff_attend(q_x, jnp.concatenate([k_c, k_x], axis=1), jnp.concatenate([v_c, v_x], axis=1), lam)
    out_x = finish(o_x)
    out_c = None
    if with_ctx:
        out_c = finish(_diff_attend(q_c, k_c, v_c, lam))
    return out_x, out_c


def _swiglu(h, w_in, w_out):
    gate, up = jnp.split(h @ w_in, 2, axis=-1)
    return (jax.nn.silu(gate) * up) @ w_out


def setup_inputs(seed: int = 0) -> dict:
    key = jax.random.key(seed)
    ks = jax.random.split(key, 26)
    f32 = jnp.float32

    def nrm(k, shape, s):
        return jax.random.normal(k, shape, f32) * s

    def gain(k, shape):
        return 1.0 + 0.02 * jax.random.normal(k, shape, f32)

    D, F, d = D_MODEL, FFN_HIDDEN, HEAD_DIM
    return {
        "x": nrm(ks[0], (BATCH, SEQ, D), 1.0),
        "c": nrm(ks[1], (BATCH, D), 1.0),
        "ctx": nrm(ks[2], (BATCH, CTX_LEN, D), 1.0),
        "c_ctx": nrm(ks[3], (D,), 1.0),
        "mod_w": nrm(ks[4], (DEPTH, D, 6 * D), 0.5 * D ** -0.5),
        "mod_b": nrm(ks[5], (DEPTH, 6 * D), 0.02),
        "norm_mix_w": gain(ks[6], (DEPTH, D)),
        "norm_ffn_w": gain(ks[7], (DEPTH, D)),
        "ev_w_in": nrm(ks[8], (N_EVEN, D, EVEN_IN), D ** -0.5),
        "ev_w_out": nrm(ks[9], (N_EVEN, EVEN_OUT, D), EVEN_OUT ** -0.5),
        "ev_qn_a": gain(ks[10], (N_EVEN, d)),
        "ev_kn_a": gain(ks[11], (N_EVEN, d)),
        "ev_qn_b": gain(ks[12], (N_EVEN, d)),
        "ev_kn_b": gain(ks[13], (N_EVEN, d)),
        "ev_sink_a": nrm(ks[14], (N_EVEN, HEADS_A), 0.5),
        "od_w_in": nrm(ks[15], (N_ODD, D, ODD_IN), D ** -0.5),
        "od_w_out": nrm(ks[16], (N_ODD, ODD_OUT, D), ODD_OUT ** -0.5),
        "od_qn": gain(ks[17], (N_ODD, d)),
        "od_kn": gain(ks[18], (N_ODD, d)),
        "od_lq1": nrm(ks[19], (N_ODD, d), 0.1),
        "od_lk1": nrm(ks[20], (N_ODD, d), 0.1),
        "od_lq2": nrm(ks[21], (N_ODD, d), 0.1),
        "od_lk2": nrm(ks[22], (N_ODD, d), 0.1),
        "od_subln": gain(ks[23], (N_ODD, 2 * d)),
        "ffn_w_in": nrm(ks[24], (DEPTH, D, 2 * F), D ** -0.5),
        "ffn_w_out": nrm(ks[25], (DEPTH, F, D), F ** -0.5),
    }


def reference(x, c, ctx, c_ctx, mod_w, mod_b, norm_mix_w, norm_ffn_w,
              ev_w_in, ev_w_out, ev_qn_a, ev_kn_a, ev_qn_b, ev_kn_b, ev_sink_a,
              od_w_in, od_w_out, od_qn, od_kn, od_lq1, od_lk1, od_lq2, od_lk2, od_subln,
              ffn_w_in, ffn_w_out):
    n_lat = x.shape[1]
    ROWS = n_lat // GRID_W
    cos, sin = _axial_tables(ROWS)
    s_c = jax.nn.silu(c)
    s_cc = jax.nn.silu(c_ctx)[None, :]
    for l in range(DEPTH):
        last = l == DEPTH - 1
        i = l // 2
        mx = (s_c @ mod_w[l] + mod_b[l])[:, None, :]
        mc = (s_cc @ mod_w[l] + mod_b[l])[:, None, :]
        shx1, scx1, gx1, shx2, scx2, gx2 = jnp.split(mx, 6, axis=-1)
        shc1, scc1, gc1, shc2, scc2, gc2 = jnp.split(mc, 6, axis=-1)
        hx = _rms(x, norm_mix_w[l]) * (1.0 + scx1) + shx1
        hc = _rms(ctx, norm_mix_w[l]) * (1.0 + scc1) + shc1
        if l % 2 == 0:
            ax, ac = _even_mixer(hx, hc, ev_w_in[i], ev_w_out[i], ev_qn_a[i], ev_kn_a[i],
                                 ev_qn_b[i], ev_kn_b[i], ev_sink_a[i], cos, sin, not last)
        else:
            lam_init = 0.8 - 0.6 * math.exp(-0.3 * l)
            ax, ac = _odd_mixer(hx, hc, od_w_in[i], od_w_out[i], od_qn[i], od_kn[i],
                                od_lq1[i], od_lk1[i], od_lq2[i], od_lk2[i], od_subln[i],
                                lam_init, cos, sin, not last)
        x = x + gx1 * ax
        x = x + gx2 * _swiglu(_rms(x, norm_ffn_w[l]) * (1.0 + scx2) + shx2, ffn_w_in[l], ffn_w_out[l])
        if not last:
            ctx = ctx + gc1 * ac
            ctx = ctx + gc2 * _swiglu(_rms(ctx, norm_ffn_w[l]) * (1.0 + scc2) + shc2, ffn_w_in[l], ffn_w_out[l])
    return x
```

```python
import functools
import math

import jax
import jax.numpy as jnp
from jax import lax
from jax.experimental import pallas as pl
from jax.experimental.pallas import tpu as pltpu

D_MODEL = 1024
HEAD_DIM = 64
CTX_LEN = 256
GRID_W = 64
WINDOW = 128
FFN_HIDDEN = 2816
DEPTH = 2
ROPE_THETA = 10000.0
EPS = 1e-6
NEG_INF = -1e30
LOG2E = 1.4426950408889634
Q_SCALE = HEAD_DIM ** -0.5 * LOG2E
MOD_ROWS = 8
ROW_TILE = 256
VMEM_LIMIT = 56 * 1024 * 1024

F32 = jnp.float32
BF16 = jnp.bfloat16


def _nt_dot(a, b):
    return lax.dot_general(a, b, (((1,), (1,)), ((), ())), preferred_element_type=F32)


def _dot(a, b):
    return jnp.dot(a, b, preferred_element_type=F32)


def _params(sem):
    return pltpu.CompilerParams(dimension_semantics=sem, vmem_limit_bytes=VMEM_LIMIT)


def _mod_kernel(cc_ref, w_ref, b_ref, o_ref):
    a = cc_ref[...]
    a = a / (1.0 + jnp.exp(-a))
    o_ref[0] = _dot(a.astype(BF16), w_ref[0].astype(BF16)) + b_ref[0]


def _modulation(cc, mod_w, mod_b):
    depth, d, n = mod_w.shape
    tn = 1536
    return pl.pallas_call(
        _mod_kernel,
        grid=(depth, n // tn),
        in_specs=[
            pl.BlockSpec((MOD_ROWS, d), lambda l, j: (0, 0)),
            pl.BlockSpec((1, d, tn), lambda l, j: (l, 0, j)),
            pl.BlockSpec((1, 1, tn), lambda l, j: (l, 0, j)),
        ],
        out_specs=pl.BlockSpec((1, MOD_ROWS, tn), lambda l, j: (l, 0, j)),
        out_shape=jax.ShapeDtypeStruct((depth, MOD_ROWS, n), F32),
        compiler_params=_params(("arbitrary", "arbitrary")),
        name="modulation",
    )(cc, mod_w, mod_b.reshape(depth, 1, n))


def _modulated_norm(x, nw, shift, scale):
    ms = jnp.mean(x * x, axis=-1, keepdims=True)
    return (x * lax.rsqrt(ms + EPS) * nw) * (1.0 + scale) + shift


def _inproj_kernel(x_ref, mod_ref, nw_ref, w_ref, g_ref, cos_ref, sin_ref, bd_ref,
                   q_ref, k_ref, vt_ref, *, n_q, n_k, n_v, d_v):
    tm = x_ref.shape[1]
    h = _modulated_norm(x_ref[0], nw_ref[...], mod_ref[0, 0, 0:1, :], mod_ref[0, 0, 1:2, :])
    y = _dot(h.astype(BF16), w_ref[...])
    cos = cos_ref[...]
    sin = sin_ref[...]
    lane = lax.broadcasted_iota(jnp.int32, (tm, 128), 1)
    first_half = (lane % HEAD_DIM) < (HEAD_DIM // 2)
    n_norm = n_q + n_k * HEAD_DIM
    for c in range(n_norm // 256):
        yc = y[:, 256 * c:256 * (c + 1)]
        ss = _dot((yc * yc).astype(BF16), bd_ref[...])
        z = yc * lax.rsqrt(ss * (1.0 / HEAD_DIM) + EPS) * g_ref[:, 256 * c:256 * (c + 1)]
        for half in range(2):
            zc = z[:, 128 * half:128 * (half + 1)]
            partner = jnp.where(first_half, pltpu.roll(zc, 96, 1), pltpu.roll(zc, 32, 1))
            o = zc * cos + partner * sin
            col = 256 * c + 128 * half
            if col < n_q:
                q_ref[0, :, col:col + 128] = (o * Q_SCALE).astype(BF16)
            else:
                kh = (col - n_q) // HEAD_DIM
                k_ref[0, kh] = o[:, :HEAD_DIM].astype(BF16)
                k_ref[0, kh + 1] = o[:, HEAD_DIM:].astype(BF16)
    vt = y[:, n_norm:].T
    vt_ref[0] = vt.reshape(n_v, d_v, tm).astype(BF16)


def _inproj(xs, mod_tab, nw, w, gains, cos_t, sin_t, bd, *, n_k, n_v, d_v):
    b, tb, d = xs.shape
    n_q = D_MODEL
    n_in = w.shape[1]
    tm = ROW_TILE
    kern = functools.partial(_inproj_kernel, n_q=n_q, n_k=n_k, n_v=n_v, d_v=d_v)
    return pl.pallas_call(
        kern,
        grid=(b, tb // tm),
        in_specs=[
            pl.BlockSpec((1, tm, d), lambda bi, i: (bi, i, 0)),
            pl.BlockSpec((1, 1, MOD_ROWS, d), lambda bi, i: (bi, jnp.minimum(i, 1), 0, 0)),
            pl.BlockSpec((1, d), lambda bi, i: (0, 0)),
            pl.BlockSpec((d, n_in), lambda bi, i: (0, 0)),
            pl.BlockSpec((1, gains.shape[1]), lambda bi, i: (0, 0)),
            pl.BlockSpec((tm, 128), lambda bi, i: (i, 0)),
            pl.BlockSpec((tm, 128), lambda bi, i: (i, 0)),
            pl.BlockSpec((256, 256), lambda bi, i: (0, 0)),
        ],
        out_specs=[
            pl.BlockSpec((1, tm, n_q), lambda bi, i: (bi, i, 0)),
            pl.BlockSpec((1, n_k, tm, HEAD_DIM), lambda bi, i: (bi, 0, i, 0)),
            pl.BlockSpec((1, n_v, d_v, tm), lambda bi, i: (bi, 0, 0, i)),
        ],
        out_shape=[
            jax.ShapeDtypeStruct((b, tb, n_q), BF16),
            jax.ShapeDtypeStruct((b, n_k, tb, HEAD_DIM), BF16),
            jax.ShapeDtypeStruct((b, n_v, d_v, tb), BF16),
        ],
        compiler_params=_params(("arbitrary", "arbitrary")),
        name="inproj",
    )(xs, mod_tab, nw, w, gains, cos_t, sin_t, bd)


def _stack_heads(q, n):
    return jnp.concatenate([q[:, HEAD_DIM * g:HEAD_DIM * (g + 1)] for g in range(n)], axis=0)


def _unstack_heads(ot, n, tq):
    return jnp.concatenate([ot[:, tq * g:tq * (g + 1)].T for g in range(n)], axis=1)


def _window_kernel(q_ref, k_ref, vt_ref, sink_ref, o_ref):
    tq = q_ref.shape[1]
    tb = k_ref.shape[2]
    span = tq + 2 * WINDOW
    t = pl.program_id(2)
    q4 = _stack_heads(q_ref[0], 4)
    ws = pl.multiple_of(jnp.clip(tq * t - WINDOW, 0, tb - span), 128)
    s_c = _nt_dot(k_ref[0, 0, 0:CTX_LEN, :], q4)
    s_w = _nt_dot(k_ref[0, 0, pl.ds(ws, span), :], q4)
    kpos = ws - CTX_LEN + lax.broadcasted_iota(jnp.int32, s_w.shape, 0)
    qpos = tq * t - CTX_LEN + lax.broadcasted_iota(jnp.int32, s_w.shape, 1) % tq
    ok = (jnp.abs(qpos - kpos) <= WINDOW) & (kpos >= 0) & (qpos >= 0)
    s_w = jnp.where(ok, s_w, NEG_INF)
    sink = sink_ref[0]
    m = jnp.maximum(jnp.maximum(jnp.max(s_c, axis=0, keepdims=True),
                                jnp.max(s_w, axis=0, keepdims=True)), sink)
    p_c = jnp.exp2(s_c - m)
    p_w = jnp.exp2(s_w - m)
    l = (jnp.sum(p_c, axis=0, keepdims=True) + jnp.sum(p_w, axis=0, keepdims=True)
         + jnp.exp2(sink - m))
    acc = (_dot(vt_ref[0, 0, :, 0:CTX_LEN], p_c.astype(BF16))
           + _dot(vt_ref[0, 0, :, pl.ds(ws, span)], p_w.astype(BF16)))
    o_ref[0] = _unstack_heads(acc / l, 4, tq).astype(BF16)


def _window_attention(q, k, vt, sink_rows, *, tq):
    b, tb, _ = q.shape
    return pl.pallas_call(
        _window_kernel,
        grid=(b, 2, tb // tq),
        in_specs=[
            pl.BlockSpec((1, tq, 256), lambda bi, h, i: (bi, i, h)),
            pl.BlockSpec((1, 1, tb, HEAD_DIM), lambda bi, h, i: (bi, h, 0, 0)),
            pl.BlockSpec((1, 1, HEAD_DIM, tb), lambda bi, h, i: (bi, h, 0, 0)),
            pl.BlockSpec((1, 1, 4 * tq), lambda bi, h, i: (h, 0, 0)),
        ],
        out_specs=pl.BlockSpec((1, tq, 256), lambda bi, h, i: (bi, i, h)),
        out_shape=jax.ShapeDtypeStruct((b, tb, 512), BF16),
        compiler_params=_params(("arbitrary", "arbitrary", "arbitrary")),
        name="window_attention",
    )(q, k, vt, sink_rows)


def _online_softmax_pv(scores, vt_slice, n_ctx_keys, n_chunks, tk):
    s = scores(0, n_ctx_keys)
    m = jnp.max(s, axis=0, keepdims=True)
    p = jnp.exp2(s - m)
    l = jnp.sum(p, axis=0, keepdims=True)
    acc = _dot(vt_slice(0, n_ctx_keys), p.astype(BF16))

    def body(j, carry):
        m, l, acc = carry
        lo = pl.multiple_of(n_ctx_keys + j * tk, 128)
        s = scores(lo, tk)
        m_new = jnp.maximum(m, jnp.max(s, axis=0, keepdims=True))
        alpha = jnp.exp2(m - m_new)
        p = jnp.exp2(s - m_new)
        l = alpha * l + jnp.sum(p, axis=0, keepdims=True)
        acc = alpha * acc + _dot(vt_slice(lo, tk), p.astype(BF16))
        return m_new, l, acc

    m, l, acc = lax.fori_loop(0, n_chunks, body, (m, l, acc))
    return acc / l


def _dense_kernel(q_ref, k_ref, vt_ref, o_ref, *, tk):
    tq = q_ref.shape[1]
    tb = k_ref.shape[2]
    q4 = _stack_heads(q_ref[0], 4)

    def scores(lo, size):
        return _nt_dot(k_ref[0, 0, pl.ds(lo, size), :], q4)

    def vt_slice(lo, size):
        return vt_ref[0, 0, :, pl.ds(lo, size)]

    is_latent = pl.program_id(2) * tq >= CTX_LEN
    n_chunks = jnp.where(is_latent, (tb - CTX_LEN) // tk, 0)
    ot = _online_softmax_pv(scores, vt_slice, CTX_LEN, n_chunks, tk)
    o_ref[0] = _unstack_heads(ot, 4, tq).astype(BF16)


def _dense_attention(q, k, vt, *, tq, tk):
    b, tb, _ = q.shape
    return pl.pallas_call(
        functools.partial(_dense_kernel, tk=tk),
        grid=(b, 2, tb // tq),
        in_specs=[
            pl.BlockSpec((1, tq, 256), lambda bi, h, i: (bi, i, 2 + h)),
            pl.BlockSpec((1, 1, tb, HEAD_DIM), lambda bi, h, i: (bi, 2 + h, 0, 0)),
            pl.BlockSpec((1, 1, HEAD_DIM, tb), lambda bi, h, i: (bi, 2 + h, 0, 0)),
        ],
        out_specs=pl.BlockSpec((1, tq, 256), lambda bi, h, i: (bi, i, h)),
        out_shape=jax.ShapeDtypeStruct((b, tb, 512), BF16),
        compiler_params=_params(("arbitrary", "arbitrary", "arbitrary")),
        name="dense_attention",
    )(q, k, vt)


def _diff_kernel(q_ref, k_ref, vt_ref, lam_ref, subw_ref, o_ref, *, tk, lam_init):
    tq = q_ref.shape[1]
    tb = k_ref.shape[2]
    q = q_ref[0]
    q1 = q[:, :HEAD_DIM]
    q2 = q[:, HEAD_DIM:]

    def scores(lo, size):
        return jnp.concatenate([_nt_dot(k_ref[0, 0, pl.ds(lo, size), :], q1),
                                _nt_dot(k_ref[0, 1, pl.ds(lo, size), :], q2)], axis=1)

    def vt_slice(lo, size):
        return vt_ref[0, 0, :, pl.ds(lo, size)]

    ot = _online_softmax_pv(scores, vt_slice, CTX_LEN, (tb - CTX_LEN) // tk, tk)
    lam = (jnp.exp(jnp.sum(lam_ref[0:1, :] * lam_ref[1:2, :], axis=-1, keepdims=True))
           - jnp.exp(jnp.sum(lam_ref[2:3, :] * lam_ref[3:4, :], axis=-1, keepdims=True)) + lam_init)
    o = ot[:, :tq] - lam * ot[:, tq:]
    ms = jnp.mean(o * o, axis=0, keepdims=True)
    o = (o * lax.rsqrt(ms + EPS) * subw_ref[...]) * (1.0 - lam_init)
    o_ref[0] = o.T.astype(BF16)


def _diff_attention(q, k, vt, lam_vecs, subw, *, tq, tk, lam_init):
    b, tb, _ = q.shape
    n_lat = tb - CTX_LEN
    n_heads = vt.shape[1]
    d_v = vt.shape[2]
    q_off = CTX_LEN // tq
    return pl.pallas_call(
        functools.partial(_diff_kernel, tk=tk, lam_init=lam_init),
        grid=(b, n_heads, n_lat // tq),
        in_specs=[
            pl.BlockSpec((1, tq, 2 * HEAD_DIM), lambda bi, h, i: (bi, i + q_off, h)),
            pl.BlockSpec((1, 2, tb, HEAD_DIM), lambda bi, h, i: (bi, h, 0, 0)),
            pl.BlockSpec((1, 1, d_v, tb), lambda bi, h, i: (bi, h, 0, 0)),
            pl.BlockSpec((4, HEAD_DIM), lambda bi, h, i: (0, 0)),
            pl.BlockSpec((d_v, 1), lambda bi, h, i: (0, 0)),
        ],
        out_specs=pl.BlockSpec((1, tq, d_v), lambda bi, h, i: (bi, i, h)),
        out_shape=jax.ShapeDtypeStruct((b, n_lat, n_heads * d_v), BF16),
        compiler_params=_params(("arbitrary", "arbitrary", "arbitrary")),
        name="diff_attention",
    )(q, k, vt, lam_vecs, subw)


def _post_kernel(*refs, n_attn):
    x_ref, mod_ref = refs[0], refs[1]
    o_refs = refs[2:2 + n_attn]
    wo_ref, nw_ref, wi_ref, wf_ref, out_ref = refs[2 + n_attn:]
    mod = mod_ref[0, 0]
    kw = wo_ref.shape[0] // n_attn
    a = _dot(o_refs[0][0], wo_ref[0:kw, :])
    for j in range(1, n_attn):
        a = a + _dot(o_refs[j][0], wo_ref[kw * j:kw * (j + 1), :])
    x1 = x_ref[0] + mod[2:3, :] * a
    h = _modulated_norm(x1, nw_ref[...], mod[3:4, :], mod[4:5, :])
    u = _dot(h.astype(BF16), wi_ref[...])
    f = wf_ref.shape[0]
    gate = u[:, :f]
    act = (gate / (1.0 + jnp.exp(-gate))) * u[:, f:]
    y = _dot(act.astype(BF16), wf_ref[...])
    out_ref[0] = x1 + mod[5:6, :] * y


def _post(xs, mod_tab, attn_outs, wo, nw, wi, wf, *, latent_only):
    b, tb, d = xs.shape
    tm = ROW_TILE
    off = CTX_LEN // tm if latent_only else 0
    n_rows = tb - CTX_LEN if latent_only else tb
    n_attn = len(attn_outs)

    def const(shape):
        return pl.BlockSpec(shape, lambda bi, i: (0,) * len(shape))

    in_specs = [
        pl.BlockSpec((1, tm, d), lambda bi, i: (bi, i + off, 0)),
        pl.BlockSpec((1, 1, MOD_ROWS, d), lambda bi, i: (bi, jnp.minimum(i + off, 1), 0, 0)),
    ]
    for o in attn_outs:
        in_specs.append(pl.BlockSpec((1, tm, o.shape[2]), lambda bi, i: (bi, i, 0)))
    in_specs += [const(wo.shape), const(nw.shape), const(wi.shape), const(wf.shape)]
    return pl.pallas_call(
        functools.partial(_post_kernel, n_attn=n_attn),
        grid=(b, n_rows // tm),
        in_specs=in_specs,
        out_specs=pl.BlockSpec((1, tm, d), lambda bi, i: (bi, i, 0)),
        out_shape=jax.ShapeDtypeStruct((b, n_rows, d), F32),
        compiler_params=_params(("arbitrary", "arbitrary")),
        name="post",
    )(xs, mod_tab, *attn_outs, wo, nw, wi, wf)


def _deinterleave_perm(n_heads):
    one = jnp.concatenate([jnp.arange(0, HEAD_DIM, 2), jnp.arange(1, HEAD_DIM, 2)])
    return (jnp.arange(n_heads)[:, None] * HEAD_DIM + one[None, :]).reshape(-1)


def _rope_tables(n_lat):
    rows = n_lat // GRID_W
    row = jnp.repeat(jnp.arange(rows, dtype=F32), GRID_W)
    col = jnp.tile(jnp.arange(GRID_W, dtype=F32), rows)
    n_freq = HEAD_DIM // 4
    inv = ROPE_THETA ** (-jnp.arange(n_freq, dtype=F32) / n_freq)
    ang = jnp.concatenate([row[:, None] * inv, col[:, None] * inv], axis=-1)
    cos, sin = jnp.cos(ang), jnp.sin(ang)
    cos = jnp.concatenate([jnp.ones((CTX_LEN, HEAD_DIM // 2), F32), cos], axis=0)
    sin = jnp.concatenate([jnp.zeros((CTX_LEN, HEAD_DIM // 2), F32), sin], axis=0)
    cos_t = jnp.tile(jnp.concatenate([cos, cos], axis=-1), (1, 2))
    sin_t = jnp.tile(jnp.concatenate([-sin, sin], axis=-1), (1, 2))
    return cos_t, sin_t


def _gain_row(parts):
    one = _deinterleave_perm(1)
    return jnp.concatenate([jnp.tile(g[one], n) for g, n in parts])[None, :].astype(F32)


def kernel(x, c, ctx, c_ctx, mod_w, mod_b, norm_mix_w, norm_ffn_w, ev_w_in, ev_w_out, ev_qn_a, ev_kn_a,
           ev_qn_b, ev_kn_b, ev_sink_a, od_w_in, od_w_out, od_qn, od_kn, od_lq1, od_lk1, od_lq2, od_lk2,
           od_subln, ffn_w_in, ffn_w_out):
    b, n_lat, d = x.shape
    assert d == D_MODEL and ctx.shape[1] == CTX_LEN and b < MOD_ROWS and n_lat % 512 == 0
    hd = HEAD_DIM

    xs = jnp.concatenate([ctx, x], axis=1)

    cc = jnp.zeros((MOD_ROWS, d), F32).at[:b].set(c).at[b].set(c_ctx)
    mod = _modulation(cc, mod_w, mod_b).reshape(DEPTH, MOD_ROWS, 6, d)
    mod_lat = mod[:, :b]
    mod_ctx = jnp.broadcast_to(mod[:, b:b + 1], mod_lat.shape)
    mod_tab = jnp.stack([mod_ctx, mod_lat], axis=2)
    mod_tab = jnp.pad(mod_tab, ((0, 0), (0, 0), (0, 0), (0, MOD_ROWS - 6), (0, 0)))

    cos_t, sin_t = _rope_tables(n_lat)
    bd = jnp.kron(jnp.eye(256 // hd, dtype=F32), jnp.ones((hd, hd), F32)).astype(BF16)

    w = ev_w_in[0]
    qa, ka, va, qb, kb, vb = jnp.split(w, [512, 640, 768, 1280, 1408], axis=1)
    p8, p2 = _deinterleave_perm(8), _deinterleave_perm(2)
    w0 = jnp.concatenate([qa[:, p8], qb[:, p8], ka[:, p2], kb[:, p2], va, vb], axis=1).astype(BF16)
    g0 = _gain_row([(ev_qn_a[0], 8), (ev_qn_b[0], 8), (ev_kn_a[0], 2), (ev_kn_b[0], 2)])
    q0, k0, vt0 = _inproj(xs, mod_tab[0], norm_mix_w[0][None, :], w0, g0, cos_t, sin_t, bd,
                          n_k=4, n_v=4, d_v=hd)
    tq_a = 128
    sink_rows = jnp.repeat(ev_sink_a[0].astype(F32) * LOG2E, tq_a).reshape(2, 1, 4 * tq_a)
    o_a = _window_attention(q0, k0, vt0, sink_rows, tq=tq_a)
    o_b = _dense_attention(q0, k0, vt0, tq=128, tk=512)
    xs = _post(xs, mod_tab[0], [o_a, o_b], ev_w_out[0].astype(BF16), norm_ffn_w[0][None, :],
               ffn_w_in[0].astype(BF16), ffn_w_out[0].astype(BF16), latent_only=False)

    w = od_w_in[0]
    p16 = _deinterleave_perm(16)
    w1 = jnp.concatenate([w[:, :1024][:, p16], w[:, 1024:2048][:, p16], w[:, 2048:]], axis=1).astype(BF16)
    g1 = _gain_row([(od_qn[0], 16), (od_kn[0], 16)])
    q1, k1, vt1 = _inproj(xs, mod_tab[1], norm_mix_w[1][None, :], w1, g1, cos_t, sin_t, bd,
                          n_k=16, n_v=8, d_v=2 * hd)
    lam_init = 0.8 - 0.6 * math.exp(-0.3 * 1)
    lam_vecs = jnp.stack([od_lq1[0], od_lk1[0], od_lq2[0], od_lk2[0]]).astype(F32)
    o_c = _diff_attention(q1, k1, vt1, lam_vecs, od_subln[0].astype(F32)[:, None],
                          tq=256, tk=512, lam_init=lam_init)
    return _post(xs, mod_tab[1], [o_c], od_w_out[0].astype(BF16), norm_ffn_w[1][None, :],
                 ffn_w_in[1].astype(BF16), ffn_w_out[1].astype(BF16), latent_only=True)
```

```python
import functools
import math

import jax
import jax.numpy as jnp
from jax import lax
from jax.experimental import pallas as pl
from jax.experimental.pallas import tpu as pltpu

D_MODEL = 1024
HEAD_DIM = 64
CTX_LEN = 256
GRID_W = 64
WINDOW = 128
FFN_HIDDEN = 2816
DEPTH = 2
ROPE_THETA = 10000.0
EPS = 1e-6
NEG_INF = -1e30
LOG2E = 1.4426950408889634
Q_SCALE = HEAD_DIM ** -0.5 * LOG2E
MOD_ROWS = 8
ROW_TILE = 256
FLASH_BUFFERS = 4
SOFTMAX_ROWS = 64
SUM_ROWS = 16
VMEM_LIMIT = 56 * 1024 * 1024

F32 = jnp.float32
BF16 = jnp.bfloat16


def _nt_dot(a, b):
    return lax.dot_general(a, b, (((1,), (1,)), ((), ())), preferred_element_type=F32)


def _dot(a, b):
    return jnp.dot(a, b, preferred_element_type=F32)


def _params(sem):
    return pltpu.CompilerParams(dimension_semantics=sem, vmem_limit_bytes=VMEM_LIMIT)


def _mod_kernel(cc_ref, w_ref, b_ref, o_ref):
    a = cc_ref[...]
    a = a / (1.0 + jnp.exp(-a))
    o_ref[0] = _dot(a.astype(BF16), w_ref[0].astype(BF16)) + b_ref[0]


def _modulation(cc, mod_w, mod_b):
    depth, d, n = mod_w.shape
    tn = 1536
    return pl.pallas_call(
        _mod_kernel,
        grid=(depth, n // tn),
        in_specs=[
            pl.BlockSpec((MOD_ROWS, d), lambda l, j: (0, 0)),
            pl.BlockSpec((1, d, tn), lambda l, j: (l, 0, j)),
            pl.BlockSpec((1, 1, tn), lambda l, j: (l, 0, j)),
        ],
        out_specs=pl.BlockSpec((1, MOD_ROWS, tn), lambda l, j: (l, 0, j)),
        out_shape=jax.ShapeDtypeStruct((depth, MOD_ROWS, n), F32),
        compiler_params=_params(("arbitrary", "arbitrary")),
        name="modulation",
    )(cc, mod_w, mod_b.reshape(depth, 1, n))


def _modulated_norm(x, nw, shift, scale):
    ms = jnp.mean(x * x, axis=-1, keepdims=True)
    return (x * lax.rsqrt(ms + EPS) * nw) * (1.0 + scale) + shift


def _inproj_kernel(x_ref, mod_ref, nw_ref, w_ref, g_ref, cos_ref, sin_ref, bd_ref,
                   q_ref, k_ref, vt_ref, *, n_q, n_k, n_v, d_v):
    tm = x_ref.shape[1]
    h = _modulated_norm(x_ref[0], nw_ref[...], mod_ref[0, 0, 0:1, :], mod_ref[0, 0, 1:2, :])
    y = _dot(h.astype(BF16), w_ref[...])
    cos = cos_ref[...]
    sin = sin_ref[...]
    lane = lax.broadcasted_iota(jnp.int32, (tm, 128), 1)
    first_half = (lane % HEAD_DIM) < (HEAD_DIM // 2)
    n_norm = n_q + n_k * HEAD_DIM
    for c in range(n_norm // 256):
        yc = y[:, 256 * c:256 * (c + 1)]
        ss = _dot((yc * yc).astype(BF16), bd_ref[...])
        z = yc * lax.rsqrt(ss * (1.0 / HEAD_DIM) + EPS) * g_ref[:, 256 * c:256 * (c + 1)]
        for half in range(2):
            zc = z[:, 128 * half:128 * (half + 1)]
            partner = jnp.where(first_half, pltpu.roll(zc, 96, 1), pltpu.roll(zc, 32, 1))
            o = zc * cos + partner * sin
            col = 256 * c + 128 * half
            if col < n_q:
                q_ref[0, :, col:col + 128] = (o * Q_SCALE).astype(BF16)
            else:
                kh = (col - n_q) // HEAD_DIM
                k_ref[0, kh] = o[:, :HEAD_DIM].astype(BF16)
                k_ref[0, kh + 1] = o[:, HEAD_DIM:].astype(BF16)
    vt = y[:, n_norm:].T.astype(BF16)
    ones_rows = (lax.broadcasted_iota(jnp.int32, (SUM_ROWS, tm), 0) == 0).astype(BF16)
    for hv in range(n_v):
        vt_ref[0, hv, 0:d_v, :] = vt[d_v * hv:d_v * (hv + 1), :]
        vt_ref[0, hv, d_v:d_v + SUM_ROWS, :] = ones_rows


def _inproj(xs, mod_tab, nw, w, gains, cos_t, sin_t, bd, *, n_k, n_v, d_v):
    b, tb, d = xs.shape
    n_q = D_MODEL
    n_in = w.shape[1]
    tm = ROW_TILE
    kern = functools.partial(_inproj_kernel, n_q=n_q, n_k=n_k, n_v=n_v, d_v=d_v)
    return pl.pallas_call(
        kern,
        grid=(b, tb // tm),
        in_specs=[
            pl.BlockSpec((1, tm, d), lambda bi, i: (bi, i, 0)),
            pl.BlockSpec((1, 1, MOD_ROWS, d), lambda bi, i: (bi, jnp.minimum(i, 1), 0, 0)),
            pl.BlockSpec((1, d), lambda bi, i: (0, 0)),
            pl.BlockSpec((d, n_in), lambda bi, i: (0, 0)),
            pl.BlockSpec((1, gains.shape[1]), lambda bi, i: (0, 0)),
            pl.BlockSpec((tm, 128), lambda bi, i: (i, 0)),
            pl.BlockSpec((tm, 128), lambda bi, i: (i, 0)),
            pl.BlockSpec((256, 256), lambda bi, i: (0, 0)),
        ],
        out_specs=[
            pl.BlockSpec((1, tm, n_q), lambda bi, i: (bi, i, 0)),
            pl.BlockSpec((1, n_k, tm, HEAD_DIM), lambda bi, i: (bi, 0, i, 0)),
            pl.BlockSpec((1, n_v, d_v + SUM_ROWS, tm), lambda bi, i: (bi, 0, 0, i)),
        ],
        out_shape=[
            jax.ShapeDtypeStruct((b, tb, n_q), BF16),
            jax.ShapeDtypeStruct((b, n_k, tb, HEAD_DIM), BF16),
            jax.ShapeDtypeStruct((b, n_v, d_v + SUM_ROWS, tb), BF16),
        ],
        compiler_params=_params(("arbitrary", "arbitrary")),
        name="inproj",
    )(xs, mod_tab, nw, w, gains, cos_t, sin_t, bd)


def _stack_heads(q, n):
    return jnp.concatenate([q[:, HEAD_DIM * g:HEAD_DIM * (g + 1)] for g in range(n)], axis=0)


def _unstack_heads(ot, n, tq):
    return jnp.concatenate([ot[:, tq * g:tq * (g + 1)].T for g in range(n)], axis=1)


def _window_kernel(q_ref, k_ref, vt_ref, sink_ref, o_ref):
    tq = q_ref.shape[1]
    tb = k_ref.shape[2]
    span = tq + 2 * WINDOW
    t = pl.program_id(2)
    q4 = _stack_heads(q_ref[0], 4)
    ws = pl.multiple_of(jnp.clip(tq * t - WINDOW, 0, tb - span), 128)
    s_c = _nt_dot(k_ref[0, 0, 0:CTX_LEN, :], q4)
    s_w = _nt_dot(k_ref[0, 0, pl.ds(ws, span), :], q4)
    kpos = ws - CTX_LEN + lax.broadcasted_iota(jnp.int32, s_w.shape, 0)
    qpos = tq * t - CTX_LEN + lax.broadcasted_iota(jnp.int32, s_w.shape, 1) % tq
    ok = (jnp.abs(qpos - kpos) <= WINDOW) & (kpos >= 0) & (qpos >= 0)
    s_w = jnp.where(ok, s_w, NEG_INF)
    sink = sink_ref[0]
    m = jnp.maximum(jnp.maximum(jnp.max(s_c, axis=0, keepdims=True),
                                jnp.max(s_w, axis=0, keepdims=True)), sink)
    p_c = jnp.exp2(s_c - m)
    p_w = jnp.exp2(s_w - m)
    acc = (_dot(vt_ref[0, 0, :, 0:CTX_LEN], p_c.astype(BF16))
           + _dot(vt_ref[0, 0, :, pl.ds(ws, span)], p_w.astype(BF16)))
    l = acc[HEAD_DIM:HEAD_DIM + 1, :] + jnp.exp2(sink - m)
    o_ref[0] = _unstack_heads(acc[:HEAD_DIM, :] / l, 4, tq).astype(BF16)


def _window_attention(q, k, vt, sink_rows, *, tq):
    b, tb, _ = q.shape
    return pl.pallas_call(
        _window_kernel,
        grid=(b, 2, tb // tq),
        in_specs=[
            pl.BlockSpec((1, tq, 256), lambda bi, h, i: (bi, i, h)),
            pl.BlockSpec((1, 1, tb, HEAD_DIM), lambda bi, h, i: (bi, h, 0, 0)),
            pl.BlockSpec((1, 1, HEAD_DIM + SUM_ROWS, tb), lambda bi, h, i: (bi, h, 0, 0)),
            pl.BlockSpec((1, 1, 4 * tq), lambda bi, h, i: (h, 0, 0)),
        ],
        out_specs=pl.BlockSpec((1, tq, 256), lambda bi, h, i: (bi, i, h)),
        out_shape=jax.ShapeDtypeStruct((b, tb, 512), BF16),
        compiler_params=_params(("arbitrary", "arbitrary", "arbitrary")),
        name="window_attention",
    )(q, k, vt, sink_rows)


def _softmax_step(s_ref, p_ref, m, chunk_max):
    m_new = jnp.maximum(m, chunk_max)
    for r in range(0, s_ref.shape[0], SOFTMAX_ROWS):
        p_ref[r:r + SOFTMAX_ROWS, :] = jnp.exp2(s_ref[r:r + SOFTMAX_ROWS, :] - m_new).astype(BF16)
    return m_new, jnp.exp2(m - m_new)


def _flash(scores, vt_slice, scratch, n_chunks, tk, d_v, finish, latent_pred=None):
    s_refs, p_refs = scratch[:FLASH_BUFFERS], scratch[FLASH_BUFFERS:]
    s = scores(0, CTX_LEN)
    m0 = jnp.max(s, axis=0, keepdims=True)
    acc0 = _dot(vt_slice(0, CTX_LEN), jnp.exp2(s - m0).astype(BF16))

    def normalised(acc):
        return acc[:d_v, :] / acc[d_v:d_v + 1, :]

    def write_scores(c):
        s = scores(CTX_LEN + c * tk, tk)
        s_refs[c % FLASH_BUFFERS][...] = s
        return jnp.max(s, axis=0, keepdims=True)

    def latent():
        m, acc = m0, acc0
        cmax = {c: write_scores(c) for c in range(min(2, n_chunks))}
        for c in range(n_chunks):
            if c + 2 < n_chunks:
                cmax[c + 2] = write_scores(c + 2)
            s_ref, p_ref = s_refs[c % FLASH_BUFFERS], p_refs[c % FLASH_BUFFERS]
            m, alpha = _softmax_step(s_ref, p_ref, m, cmax.pop(c))
            acc = alpha * acc + _dot(vt_slice(CTX_LEN + c * tk, tk), p_ref[...])
        finish(normalised(acc))

    if latent_pred is None:
        latent()
    else:
        pl.when(latent_pred)(latent)
        pl.when(jnp.logical_not(latent_pred))(lambda: finish(normalised(acc0)))


def _flash_scratch(tk, nq):
    return [pltpu.VMEM((tk, nq), F32)] * FLASH_BUFFERS + [pltpu.VMEM((tk, nq), BF16)] * FLASH_BUFFERS


def _dense_kernel(q_ref, k_ref, vt_ref, o_ref, *scratch, tk):
    tq = q_ref.shape[1]
    tb = k_ref.shape[2]
    q4 = _stack_heads(q_ref[0], 4)

    def scores(lo, size):
        return _nt_dot(k_ref[0, 0, lo:lo + size, :], q4)

    def vt_slice(lo, size):
        return vt_ref[0, 0, :, lo:lo + size]

    def finish(ot):
        o_ref[0] = _unstack_heads(ot, 4, tq).astype(BF16)

    _flash(scores, vt_slice, scratch, (tb - CTX_LEN) // tk, tk, HEAD_DIM, finish,
           latent_pred=pl.program_id(2) * tq >= CTX_LEN)


def _dense_attention(q, k, vt, *, tq, tk):
    b, tb, _ = q.shape
    assert (tb - CTX_LEN) % tk == 0
    return pl.pallas_call(
        functools.partial(_dense_kernel, tk=tk),
        grid=(b, 2, tb // tq),
        in_specs=[
            pl.BlockSpec((1, tq, 256), lambda bi, h, i: (bi, i, 2 + h)),
            pl.BlockSpec((1, 1, tb, HEAD_DIM), lambda bi, h, i: (bi, 2 + h, 0, 0)),
            pl.BlockSpec((1, 1, HEAD_DIM + SUM_ROWS, tb), lambda bi, h, i: (bi, 2 + h, 0, 0)),
        ],
        out_specs=pl.BlockSpec((1, tq, 256), lambda bi, h, i: (bi, i, h)),
        out_shape=jax.ShapeDtypeStruct((b, tb, 512), BF16),
        scratch_shapes=_flash_scratch(tk, 4 * tq),
        compiler_params=_params(("arbitrary", "arbitrary", "arbitrary")),
        name="dense_attention",
    )(q, k, vt)


def _diff_kernel(q_ref, k_ref, vt_ref, lam_ref, subw_ref, o_ref, *scratch, tk, lam_init):
    tq = q_ref.shape[1]
    tb = k_ref.shape[2]
    q = q_ref[0]
    q1 = q[:, :HEAD_DIM]
    q2 = q[:, HEAD_DIM:]

    def scores(lo, size):
        return jnp.concatenate([_nt_dot(k_ref[0, 0, lo:lo + size, :], q1),
                                _nt_dot(k_ref[0, 1, lo:lo + size, :], q2)], axis=1)

    def vt_slice(lo, size):
        return vt_ref[0, 0, :, lo:lo + size]

    def finish(ot):
        lam = (jnp.exp(jnp.sum(lam_ref[0:1, :] * lam_ref[1:2, :], axis=-1, keepdims=True))
               - jnp.exp(jnp.sum(lam_ref[2:3, :] * lam_ref[3:4, :], axis=-1, keepdims=True)) + lam_init)
        o = ot[:, :tq] - lam * ot[:, tq:]
        ms = jnp.mean(o * o, axis=0, keepdims=True)
        o = (o * lax.rsqrt(ms + EPS) * subw_ref[...]) * (1.0 - lam_init)
        o_ref[0] = o.T.astype(BF16)

    _flash(scores, vt_slice, scratch, (tb - CTX_LEN) // tk, tk, 2 * HEAD_DIM, finish)


def _diff_attention(q, k, vt, lam_vecs, subw, *, tq, tk, lam_init):
    b, tb, _ = q.shape
    n_lat = tb - CTX_LEN
    n_heads = vt.shape[1]
    d_v = vt.shape[2] - SUM_ROWS
    q_off = CTX_LEN // tq
    return pl.pallas_call(
        functools.partial(_diff_kernel, tk=tk, lam_init=lam_init),
        grid=(b, n_heads, n_lat // tq),
        in_specs=[
            pl.BlockSpec((1, tq, 2 * HEAD_DIM), lambda bi, h, i: (bi, i + q_off, h)),
            pl.BlockSpec((1, 2, tb, HEAD_DIM), lambda bi, h, i: (bi, h, 0, 0)),
            pl.BlockSpec((1, 1, d_v + SUM_ROWS, tb), lambda bi, h, i: (bi, h, 0, 0)),
            pl.BlockSpec((4, HEAD_DIM), lambda bi, h, i: (0, 0)),
            pl.BlockSpec((d_v, 1), lambda bi, h, i: (0, 0)),
        ],
        out_specs=pl.BlockSpec((1, tq, d_v), lambda bi, h, i: (bi, i, h)),
        out_shape=jax.ShapeDtypeStruct((b, n_lat, n_heads * d_v), BF16),
        scratch_shapes=_flash_scratch(tk, 2 * tq),
        compiler_params=_params(("arbitrary", "arbitrary", "arbitrary")),
        name="diff_attention",
    )(q, k, vt, lam_vecs, subw)


def _post_kernel(*refs, n_attn):
    x_ref, mod_ref = refs[0], refs[1]
    o_refs = refs[2:2 + n_attn]
    wo_ref, nw_ref, wi_ref, wf_ref, out_ref = refs[2 + n_attn:]
    mod = mod_ref[0, 0]
    kw = wo_ref.shape[0] // n_attn
    a = _dot(o_refs[0][0], wo_ref[0:kw, :])
    for j in range(1, n_attn):
        a = a + _dot(o_refs[j][0], wo_ref[kw * j:kw * (j + 1), :])
    x1 = x_ref[0] + mod[2:3, :] * a
    h = _modulated_norm(x1, nw_ref[...], mod[3:4, :], mod[4:5, :])
    u = _dot(h.astype(BF16), wi_ref[...])
    f = wf_ref.shape[0]
    gate = u[:, :f]
    act = (gate / (1.0 + jnp.exp(-gate))) * u[:, f:]
    y = _dot(act.astype(BF16), wf_ref[...])
    out_ref[0] = x1 + mod[5:6, :] * y


def _post(xs, mod_tab, attn_outs, wo, nw, wi, wf, *, latent_only):
    b, tb, d = xs.shape
    tm = ROW_TILE
    off = CTX_LEN // tm if latent_only else 0
    n_rows = tb - CTX_LEN if latent_only else tb
    n_attn = len(attn_outs)

    def const(shape):
        return pl.BlockSpec(shape, lambda bi, i: (0,) * len(shape))

    in_specs = [
        pl.BlockSpec((1, tm, d), lambda bi, i: (bi, i + off, 0)),
        pl.BlockSpec((1, 1, MOD_ROWS, d), lambda bi, i: (bi, jnp.minimum(i + off, 1), 0, 0)),
    ]
    for o in attn_outs:
        in_specs.append(pl.BlockSpec((1, tm, o.shape[2]), lambda bi, i: (bi, i, 0)))
    in_specs += [const(wo.shape), const(nw.shape), const(wi.shape), const(wf.shape)]
    return pl.pallas_call(
        functools.partial(_post_kernel, n_attn=n_attn),
        grid=(b, n_rows // tm),
        in_specs=in_specs,
        out_specs=pl.BlockSpec((1, tm, d), lambda bi, i: (bi, i, 0)),
        out_shape=jax.ShapeDtypeStruct((b, n_rows, d), F32),
        compiler_params=_params(("arbitrary", "arbitrary")),
        name="post",
    )(xs, mod_tab, *attn_outs, wo, nw, wi, wf)


def _deinterleave_perm(n_heads):
    one = jnp.concatenate([jnp.arange(0, HEAD_DIM, 2), jnp.arange(1, HEAD_DIM, 2)])
    return (jnp.arange(n_heads)[:, None] * HEAD_DIM + one[None, :]).reshape(-1)


def _rope_tables(n_lat):
    rows = n_lat // GRID_W
    row = jnp.repeat(jnp.arange(rows, dtype=F32), GRID_W)
    col = jnp.tile(jnp.arange(GRID_W, dtype=F32), rows)
    n_freq = HEAD_DIM // 4
    inv = ROPE_THETA ** (-jnp.arange(n_freq, dtype=F32) / n_freq)
    ang = jnp.concatenate([row[:, None] * inv, col[:, None] * inv], axis=-1)
    cos, sin = jnp.cos(ang), jnp.sin(ang)
    cos = jnp.concatenate([jnp.ones((CTX_LEN, HEAD_DIM // 2), F32), cos], axis=0)
    sin = jnp.concatenate([jnp.zeros((CTX_LEN, HEAD_DIM // 2), F32), sin], axis=0)
    cos_t = jnp.tile(jnp.concatenate([cos, cos], axis=-1), (1, 2))
    sin_t = jnp.tile(jnp.concatenate([-sin, sin], axis=-1), (1, 2))
    return cos_t, sin_t


def _gain_row(parts):
    one = _deinterleave_perm(1)
    return jnp.concatenate([jnp.tile(g[one], n) for g, n in parts])[None, :].astype(F32)


def kernel(x, c, ctx, c_ctx, mod_w, mod_b, norm_mix_w, norm_ffn_w, ev_w_in, ev_w_out, ev_qn_a, ev_kn_a,
           ev_qn_b, ev_kn_b, ev_sink_a, od_w_in, od_w_out, od_qn, od_kn, od_lq1, od_lk1, od_lq2, od_lk2,
           od_subln, ffn_w_in, ffn_w_out):
    b, n_lat, d = x.shape
    assert d == D_MODEL and ctx.shape[1] == CTX_LEN and b < MOD_ROWS and n_lat % 512 == 0
    hd = HEAD_DIM

    xs = jnp.concatenate([ctx, x], axis=1)

    cc = jnp.zeros((MOD_ROWS, d), F32).at[:b].set(c).at[b].set(c_ctx)
    mod = _modulation(cc, mod_w, mod_b).reshape(DEPTH, MOD_ROWS, 6, d)
    mod_lat = mod[:, :b]
    mod_ctx = jnp.broadcast_to(mod[:, b:b + 1], mod_lat.shape)
    mod_tab = jnp.stack([mod_ctx, mod_lat], axis=2)
    mod_tab = jnp.pad(mod_tab, ((0, 0), (0, 0), (0, 0), (0, MOD_ROWS - 6), (0, 0)))

    cos_t, sin_t = _rope_tables(n_lat)
    bd = jnp.kron(jnp.eye(256 // hd, dtype=F32), jnp.ones((hd, hd), F32)).astype(BF16)

    w = ev_w_in[0]
    qa, ka, va, qb, kb, vb = jnp.split(w, [512, 640, 768, 1280, 1408], axis=1)
    p8, p2 = _deinterleave_perm(8), _deinterleave_perm(2)
    w0 = jnp.concatenate([qa[:, p8], qb[:, p8], ka[:, p2], kb[:, p2], va, vb], axis=1).astype(BF16)
    g0 = _gain_row([(ev_qn_a[0], 8), (ev_qn_b[0], 8), (ev_kn_a[0], 2), (ev_kn_b[0], 2)])
    q0, k0, vt0 = _inproj(xs, mod_tab[0], norm_mix_w[0][None, :], w0, g0, cos_t, sin_t, bd,
                          n_k=4, n_v=4, d_v=hd)
    tq_a = 128
    sink_rows = jnp.repeat(ev_sink_a[0].astype(F32) * LOG2E, tq_a).reshape(2, 1, 4 * tq_a)
    o_a = _window_attention(q0, k0, vt0, sink_rows, tq=tq_a)
    o_b = _dense_attention(q0, k0, vt0, tq=128, tk=512)
    xs = _post(xs, mod_tab[0], [o_a, o_b], ev_w_out[0].astype(BF16), norm_ffn_w[0][None, :],
               ffn_w_in[0].astype(BF16), ffn_w_out[0].astype(BF16), latent_only=False)

    w = od_w_in[0]
    p16 = _deinterleave_perm(16)
    w1 = jnp.concatenate([w[:, :1024][:, p16], w[:, 1024:2048][:, p16], w[:, 2048:]], axis=1).astype(BF16)
    g1 = _gain_row([(od_qn[0], 16), (od_kn[0], 16)])
    q1, k1, vt1 = _inproj(xs, mod_tab[1], norm_mix_w[1][None, :], w1, g1, cos_t, sin_t, bd,
                          n_k=16, n_v=8, d_v=2 * hd)
    lam_init = 0.8 - 0.6 * math.exp(-0.3 * 1)
    lam_vecs = jnp.stack([od_lq1[0], od_lk1[0], od_lq2[0], od_lk2[0]]).astype(F32)
    o_c = _diff_attention(q1, k1, vt1, lam_vecs, od_subln[0].astype(F32)[:, None],
                          tq=256, tk=512, lam_init=lam_init)
    return _post(xs, mod_tab[1], [o_c], od_w_out[0].astype(BF16), norm_ffn_w[1][None, :],
                 ffn_w_in[1].astype(BF16), ffn_w_out[1].astype(BF16), latent_only=True)
```

```python
import functools
import math

import jax
import jax.numpy as jnp
from jax import lax
from jax.experimental import pallas as pl
from jax.experimental.pallas import tpu as pltpu

D_MODEL = 1024
HEAD_DIM = 64
CTX_LEN = 256
GRID_W = 64
WINDOW = 128
FFN_HIDDEN = 2816
DEPTH = 2
ROPE_THETA = 10000.0
EPS = 1e-6
NEG_INF = -1e30
LOG2E = 1.4426950408889634
Q_SCALE = HEAD_DIM ** -0.5 * LOG2E
MOD_ROWS = 8
ROW_TILE = 256
FLASH_BUFFERS = 4
SOFTMAX_ROWS = 64
SUM_ROWS = 16
VMEM_LIMIT = 56 * 1024 * 1024

F32 = jnp.float32
BF16 = jnp.bfloat16


def _nt_dot(a, b):
    return lax.dot_general(a, b, (((1,), (1,)), ((), ())), preferred_element_type=F32)


def _dot(a, b):
    return jnp.dot(a, b, preferred_element_type=F32)


def _params(sem):
    return pltpu.CompilerParams(dimension_semantics=sem, vmem_limit_bytes=VMEM_LIMIT)


def _mod_kernel(cc_ref, w_ref, b_ref, o_ref):
    a = cc_ref[...]
    a = a / (1.0 + jnp.exp(-a))
    o_ref[0] = _dot(a.astype(BF16), w_ref[0].astype(BF16)) + b_ref[0]


def _modulation(cc, mod_w, mod_b):
    depth, d, n = mod_w.shape
    tn = 1536
    return pl.pallas_call(
        _mod_kernel,
        grid=(depth, n // tn),
        in_specs=[
            pl.BlockSpec((MOD_ROWS, d), lambda l, j: (0, 0)),
            pl.BlockSpec((1, d, tn), lambda l, j: (l, 0, j)),
            pl.BlockSpec((1, 1, tn), lambda l, j: (l, 0, j)),
        ],
        out_specs=pl.BlockSpec((1, MOD_ROWS, tn), lambda l, j: (l, 0, j)),
        out_shape=jax.ShapeDtypeStruct((depth, MOD_ROWS, n), F32),
        compiler_params=_params(("arbitrary", "arbitrary")),
        name="modulation",
    )(cc, mod_w, mod_b.reshape(depth, 1, n))


def _modulated_norm(x, nw, shift, scale):
    ms = jnp.mean(x * x, axis=-1, keepdims=True)
    return (x * lax.rsqrt(ms + EPS) * nw) * (1.0 + scale) + shift


def _inproj_kernel(x_ref, mod_ref, nw_ref, w_ref, g_ref, cos_ref, sin_ref, bd_ref,
                   q_ref, k_ref, vt_ref, *, n_q, n_k, n_v, d_v):
    tm = x_ref.shape[1]
    h = _modulated_norm(x_ref[0], nw_ref[...], mod_ref[0, 0, 0:1, :], mod_ref[0, 0, 1:2, :])
    y = _dot(h.astype(BF16), w_ref[...])
    cos = cos_ref[...]
    sin = sin_ref[...]
    lane = lax.broadcasted_iota(jnp.int32, (tm, 128), 1)
    first_half = (lane % HEAD_DIM) < (HEAD_DIM // 2)
    n_norm = n_q + n_k * HEAD_DIM
    for c in range(n_norm // 256):
        yc = y[:, 256 * c:256 * (c + 1)]
        ss = _dot((yc * yc).astype(BF16), bd_ref[...])
        z = yc * lax.rsqrt(ss * (1.0 / HEAD_DIM) + EPS) * g_ref[:, 256 * c:256 * (c + 1)]
        for half in range(2):
            zc = z[:, 128 * half:128 * (half + 1)]
            partner = jnp.where(first_half, pltpu.roll(zc, 96, 1), pltpu.roll(zc, 32, 1))
            o = zc * cos + partner * sin
            col = 256 * c + 128 * half
            if col < n_q:
                q_ref[0, :, col:col + 128] = (o * Q_SCALE).astype(BF16)
            else:
                kh = (col - n_q) // HEAD_DIM
                k_ref[0, kh] = o[:, :HEAD_DIM].astype(BF16)
                k_ref[0, kh + 1] = o[:, HEAD_DIM:].astype(BF16)
    vt = y[:, n_norm:].T.astype(BF16)
    ones_rows = (lax.broadcasted_iota(jnp.int32, (SUM_ROWS, tm), 0) == 0).astype(BF16)
    for hv in range(n_v):
        vt_ref[0, hv, 0:d_v, :] = vt[d_v * hv:d_v * (hv + 1), :]
        vt_ref[0, hv, d_v:d_v + SUM_ROWS, :] = ones_rows


def _inproj(xs, mod_tab, nw, w, gains, cos_t, sin_t, bd, *, n_k, n_v, d_v):
    b, tb, d = xs.shape
    n_q = D_MODEL
    n_in = w.shape[1]
    tm = ROW_TILE
    kern = functools.partial(_inproj_kernel, n_q=n_q, n_k=n_k, n_v=n_v, d_v=d_v)
    return pl.pallas_call(
        kern,
        grid=(b, tb // tm),
        in_specs=[
            pl.BlockSpec((1, tm, d), lambda bi, i: (bi, i, 0)),
            pl.BlockSpec((1, 1, MOD_ROWS, d), lambda bi, i: (bi, jnp.minimum(i, 1), 0, 0)),
            pl.BlockSpec((1, d), lambda bi, i: (0, 0)),
            pl.BlockSpec((d, n_in), lambda bi, i: (0, 0)),
            pl.BlockSpec((1, gains.shape[1]), lambda bi, i: (0, 0)),
            pl.BlockSpec((tm, 128), lambda bi, i: (i, 0)),
            pl.BlockSpec((tm, 128), lambda bi, i: (i, 0)),
            pl.BlockSpec((256, 256), lambda bi, i: (0, 0)),
        ],
        out_specs=[
            pl.BlockSpec((1, tm, n_q), lambda bi, i: (bi, i, 0)),
            pl.BlockSpec((1, n_k, tm, HEAD_DIM), lambda bi, i: (bi, 0, i, 0)),
            pl.BlockSpec((1, n_v, d_v + SUM_ROWS, tm), lambda bi, i: (bi, 0, 0, i)),
        ],
        out_shape=[
            jax.ShapeDtypeStruct((b, tb, n_q), BF16),
            jax.ShapeDtypeStruct((b, n_k, tb, HEAD_DIM), BF16),
            jax.ShapeDtypeStruct((b, n_v, d_v + SUM_ROWS, tb), BF16),
        ],
        compiler_params=_params(("arbitrary", "arbitrary")),
        name="inproj",
    )(xs, mod_tab, nw, w, gains, cos_t, sin_t, bd)


def _stack_heads(q, n):
    return jnp.concatenate([q[:, HEAD_DIM * g:HEAD_DIM * (g + 1)] for g in range(n)], axis=0)


def _unstack_heads(ot, n, tq):
    return jnp.concatenate([ot[:, tq * g:tq * (g + 1)].T for g in range(n)], axis=1)


def _window_kernel(q_ref, k_ref, vt_ref, sink_ref, o_ref):
    tq = q_ref.shape[1]
    tb = k_ref.shape[2]
    span = tq + 2 * WINDOW
    t = pl.program_id(2)
    q4 = _stack_heads(q_ref[0], 4)
    ws = pl.multiple_of(jnp.clip(tq * t - WINDOW, 0, tb - span), 128)
    s_c = _nt_dot(k_ref[0, 0, 0:CTX_LEN, :], q4)
    s_w = _nt_dot(k_ref[0, 0, pl.ds(ws, span), :], q4)
    kpos = ws - CTX_LEN + lax.broadcasted_iota(jnp.int32, s_w.shape, 0)
    qpos = tq * t - CTX_LEN + lax.broadcasted_iota(jnp.int32, s_w.shape, 1) % tq
    ok = (jnp.abs(qpos - kpos) <= WINDOW) & (kpos >= 0) & (qpos >= 0)
    s_w = jnp.where(ok, s_w, NEG_INF)
    sink = sink_ref[0]
    m = jnp.maximum(jnp.maximum(jnp.max(s_c, axis=0, keepdims=True),
                                jnp.max(s_w, axis=0, keepdims=True)), sink)
    p_c = jnp.exp2(s_c - m)
    p_w = jnp.exp2(s_w - m)
    acc = (_dot(vt_ref[0, 0, :, 0:CTX_LEN], p_c.astype(BF16))
           + _dot(vt_ref[0, 0, :, pl.ds(ws, span)], p_w.astype(BF16)))
    l = acc[HEAD_DIM:HEAD_DIM + 1, :] + jnp.exp2(sink - m)
    o_ref[0] = _unstack_heads(acc[:HEAD_DIM, :] / l, 4, tq).astype(BF16)


def _window_attention(q, k, vt, sink_rows, *, tq):
    b, tb, _ = q.shape
    return pl.pallas_call(
        _window_kernel,
        grid=(b, 2, tb // tq),
        in_specs=[
            pl.BlockSpec((1, tq, 256), lambda bi, h, i: (bi, i, h)),
            pl.BlockSpec((1, 1, tb, HEAD_DIM), lambda bi, h, i: (bi, h, 0, 0)),
            pl.BlockSpec((1, 1, HEAD_DIM + SUM_ROWS, tb), lambda bi, h, i: (bi, h, 0, 0)),
            pl.BlockSpec((1, 1, 4 * tq), lambda bi, h, i: (h, 0, 0)),
        ],
        out_specs=pl.BlockSpec((1, tq, 256), lambda bi, h, i: (bi, i, h)),
        out_shape=jax.ShapeDtypeStruct((b, tb, 512), BF16),
        compiler_params=_params(("arbitrary", "arbitrary", "arbitrary")),
        name="window_attention",
    )(q, k, vt, sink_rows)


def _softmax_step(s_ref, p_ref, m, chunk_max):
    m_new = jnp.maximum(m, chunk_max)
    for r in range(0, s_ref.shape[0], SOFTMAX_ROWS):
        p_ref[r:r + SOFTMAX_ROWS, :] = jnp.exp2(s_ref[r:r + SOFTMAX_ROWS, :] - m_new).astype(BF16)
    return m_new, jnp.exp2(m - m_new)


def _flash(scores, vt_slice, scratch, n_chunks, tk, d_v, finish, latent_pred=None):
    n_streams = len(scores)
    nbuf = n_streams * FLASH_BUFFERS
    s_refs, p_refs = scratch[:nbuf], scratch[nbuf:]
    m0, acc0 = [], []
    for score in scores:
        s = score(0, CTX_LEN)
        m0.append(jnp.max(s, axis=0, keepdims=True))
        acc0.append(_dot(vt_slice(0, CTX_LEN), jnp.exp2(s - m0[-1]).astype(BF16)))

    def normalised(accs):
        return [acc[:d_v, :] / acc[d_v:d_v + 1, :] for acc in accs]

    def buf(j, c):
        return j * FLASH_BUFFERS + c % FLASH_BUFFERS

    def write_scores(j, c):
        s = scores[j](CTX_LEN + c * tk, tk)
        s_refs[buf(j, c)][...] = s
        return jnp.max(s, axis=0, keepdims=True)

    def latent():
        m, acc = list(m0), list(acc0)
        cmax = {(j, c): write_scores(j, c) for c in range(min(2, n_chunks)) for j in range(n_streams)}
        for c in range(n_chunks):
            for j in range(n_streams):
                if c + 2 < n_chunks:
                    cmax[j, c + 2] = write_scores(j, c + 2)
                s_ref, p_ref = s_refs[buf(j, c)], p_refs[buf(j, c)]
                m[j], alpha = _softmax_step(s_ref, p_ref, m[j], cmax.pop((j, c)))
                acc[j] = alpha * acc[j] + _dot(vt_slice(CTX_LEN + c * tk, tk), p_ref[...])
        finish(normalised(acc))

    if latent_pred is None:
        latent()
    else:
        pl.when(latent_pred)(latent)
        pl.when(jnp.logical_not(latent_pred))(lambda: finish(normalised(acc0)))


def _flash_scratch(tk, nq, n_streams):
    n = n_streams * FLASH_BUFFERS
    return [pltpu.VMEM((tk, nq), F32)] * n + [pltpu.VMEM((tk, nq), BF16)] * n


def _dense_kernel(q_ref, k_ref, vt_ref, o_ref, *scratch, tk):
    tq = q_ref.shape[1]
    tb = k_ref.shape[2]
    q4 = _stack_heads(q_ref[0], 4)
    q_halves = [q4[:2 * tq, :], q4[2 * tq:, :]]
    scores = [lambda lo, size, qh=qh: _nt_dot(k_ref[0, 0, lo:lo + size, :], qh) for qh in q_halves]

    def vt_slice(lo, size):
        return vt_ref[0, 0, :, lo:lo + size]

    def finish(ots):
        o_ref[0] = _unstack_heads(jnp.concatenate(ots, axis=1), 4, tq).astype(BF16)

    _flash(scores, vt_slice, scratch, (tb - CTX_LEN) // tk, tk, HEAD_DIM, finish,
           latent_pred=pl.program_id(2) * tq >= CTX_LEN)


def _dense_attention(q, k, vt, *, tq, tk):
    b, tb, _ = q.shape
    assert (tb - CTX_LEN) % tk == 0
    return pl.pallas_call(
        functools.partial(_dense_kernel, tk=tk),
        grid=(b, 2, tb // tq),
        in_specs=[
            pl.BlockSpec((1, tq, 256), lambda bi, h, i: (bi, i, 2 + h)),
            pl.BlockSpec((1, 1, tb, HEAD_DIM), lambda bi, h, i: (bi, 2 + h, 0, 0)),
            pl.BlockSpec((1, 1, HEAD_DIM + SUM_ROWS, tb), lambda bi, h, i: (bi, 2 + h, 0, 0)),
        ],
        out_specs=pl.BlockSpec((1, tq, 256), lambda bi, h, i: (bi, i, h)),
        out_shape=jax.ShapeDtypeStruct((b, tb, 512), BF16),
        scratch_shapes=_flash_scratch(tk, 2 * tq, 2),
        compiler_params=_params(("arbitrary", "arbitrary", "arbitrary")),
        name="dense_attention",
    )(q, k, vt)


def _diff_kernel(q_ref, k_ref, vt_ref, lam_ref, subw_ref, o_ref, *scratch, tk, lam_init):
    tq = q_ref.shape[1]
    tb = k_ref.shape[2]
    q = q_ref[0]
    q1 = q[:, :HEAD_DIM]
    q2 = q[:, HEAD_DIM:]

    scores = [lambda lo, size: _nt_dot(k_ref[0, 0, lo:lo + size, :], q1),
              lambda lo, size: _nt_dot(k_ref[0, 1, lo:lo + size, :], q2)]

    def vt_slice(lo, size):
        return vt_ref[0, 0, :, lo:lo + size]

    def finish(ots):
        lam = (jnp.exp(jnp.sum(lam_ref[0:1, :] * lam_ref[1:2, :], axis=-1, keepdims=True))
               - jnp.exp(jnp.sum(lam_ref[2:3, :] * lam_ref[3:4, :], axis=-1, keepdims=True)) + lam_init)
        o = ots[0] - lam * ots[1]
        ms = jnp.mean(o * o, axis=0, keepdims=True)
        o = (o * lax.rsqrt(ms + EPS) * subw_ref[...]) * (1.0 - lam_init)
        o_ref[0] = o.T.astype(BF16)

    _flash(scores, vt_slice, scratch, (tb - CTX_LEN) // tk, tk, 2 * HEAD_DIM, finish)


def _diff_attention(q, k, vt, lam_vecs, subw, *, tq, tk, lam_init):
    b, tb, _ = q.shape
    n_lat = tb - CTX_LEN
    n_heads = vt.shape[1]
    d_v = vt.shape[2] - SUM_ROWS
    q_off = CTX_LEN // tq
    return pl.pallas_call(
        functools.partial(_diff_kernel, tk=tk, lam_init=lam_init),
        grid=(b, n_heads, n_lat // tq),
        in_specs=[
            pl.BlockSpec((1, tq, 2 * HEAD_DIM), lambda bi, h, i: (bi, i + q_off, h)),
            pl.BlockSpec((1, 2, tb, HEAD_DIM), lambda bi, h, i: (bi, h, 0, 0)),
            pl.BlockSpec((1, 1, d_v + SUM_ROWS, tb), lambda bi, h, i: (bi, h, 0, 0)),
            pl.BlockSpec((4, HEAD_DIM), lambda bi, h, i: (0, 0)),
            pl.BlockSpec((d_v, 1), lambda bi, h, i: (0, 0)),
        ],
        out_specs=pl.BlockSpec((1, tq, d_v), lambda bi, h, i: (bi, i, h)),
        out_shape=jax.ShapeDtypeStruct((b, n_lat, n_heads * d_v), BF16),
        scratch_shapes=_flash_scratch(tk, tq, 2),
        compiler_params=_params(("arbitrary", "arbitrary", "arbitrary")),
        name="diff_attention",
    )(q, k, vt, lam_vecs, subw)


def _post_kernel(*refs, n_attn):
    x_ref, mod_ref = refs[0], refs[1]
    o_refs = refs[2:2 + n_attn]
    wo_ref, nw_ref, wi_ref, wf_ref, out_ref = refs[2 + n_attn:]
    mod = mod_ref[0, 0]
    kw = wo_ref.shape[0] // n_attn
    a = _dot(o_refs[0][0], wo_ref[0:kw, :])
    for j in range(1, n_attn):
        a = a + _dot(o_refs[j][0], wo_ref[kw * j:kw * (j + 1), :])
    x1 = x_ref[0] + mod[2:3, :] * a
    h = _modulated_norm(x1, nw_ref[...], mod[3:4, :], mod[4:5, :])
    u = _dot(h.astype(BF16), wi_ref[...])
    f = wf_ref.shape[0]
    gate = u[:, :f]
    act = (gate / (1.0 + jnp.exp(-gate))) * u[:, f:]
    y = _dot(act.astype(BF16), wf_ref[...])
    out_ref[0] = x1 + mod[5:6, :] * y


def _post(xs, mod_tab, attn_outs, wo, nw, wi, wf, *, latent_only):
    b, tb, d = xs.shape
    tm = ROW_TILE
    off = CTX_LEN // tm if latent_only else 0
    n_rows = tb - CTX_LEN if latent_only else tb
    n_attn = len(attn_outs)

    def const(shape):
        return pl.BlockSpec(shape, lambda bi, i: (0,) * len(shape))

    in_specs = [
        pl.BlockSpec((1, tm, d), lambda bi, i: (bi, i + off, 0)),
        pl.BlockSpec((1, 1, MOD_ROWS, d), lambda bi, i: (bi, jnp.minimum(i + off, 1), 0, 0)),
    ]
    for o in attn_outs:
        in_specs.append(pl.BlockSpec((1, tm, o.shape[2]), lambda bi, i: (bi, i, 0)))
    in_specs += [const(wo.shape), const(nw.shape), const(wi.shape), const(wf.shape)]
    return pl.pallas_call(
        functools.partial(_post_kernel, n_attn=n_attn),
        grid=(b, n_rows // tm),
        in_specs=in_specs,
        out_specs=pl.BlockSpec((1, tm, d), lambda bi, i: (bi, i, 0)),
        out_shape=jax.ShapeDtypeStruct((b, n_rows, d), F32),
        compiler_params=_params(("arbitrary", "arbitrary")),
        name="post",
    )(xs, mod_tab, *attn_outs, wo, nw, wi, wf)


def _deinterleave_perm(n_heads):
    one = jnp.concatenate([jnp.arange(0, HEAD_DIM, 2), jnp.arange(1, HEAD_DIM, 2)])
    return (jnp.arange(n_heads)[:, None] * HEAD_DIM + one[None, :]).reshape(-1)


def _rope_tables(n_lat):
    rows = n_lat // GRID_W
    row = jnp.repeat(jnp.arange(rows, dtype=F32), GRID_W)
    col = jnp.tile(jnp.arange(GRID_W, dtype=F32), rows)
    n_freq = HEAD_DIM // 4
    inv = ROPE_THETA ** (-jnp.arange(n_freq, dtype=F32) / n_freq)
    ang = jnp.concatenate([row[:, None] * inv, col[:, None] * inv], axis=-1)
    cos, sin = jnp.cos(ang), jnp.sin(ang)
    cos = jnp.concatenate([jnp.ones((CTX_LEN, HEAD_DIM // 2), F32), cos], axis=0)
    sin = jnp.concatenate([jnp.zeros((CTX_LEN, HEAD_DIM // 2), F32), sin], axis=0)
    cos_t = jnp.tile(jnp.concatenate([cos, cos], axis=-1), (1, 2))
    sin_t = jnp.tile(jnp.concatenate([-sin, sin], axis=-1), (1, 2))
    return cos_t, sin_t


def _gain_row(parts):
    one = _deinterleave_perm(1)
    return jnp.concatenate([jnp.tile(g[one], n) for g, n in parts])[None, :].astype(F32)


def kernel(x, c, ctx, c_ctx, mod_w, mod_b, norm_mix_w, norm_ffn_w, ev_w_in, ev_w_out, ev_qn_a, ev_kn_a,
           ev_qn_b, ev_kn_b, ev_sink_a, od_w_in, od_w_out, od_qn, od_kn, od_lq1, od_lk1, od_lq2, od_lk2,
           od_subln, ffn_w_in, ffn_w_out):
    b, n_lat, d = x.shape
    assert d == D_MODEL and ctx.shape[1] == CTX_LEN and b < MOD_ROWS and n_lat % 512 == 0
    hd = HEAD_DIM

    xs = jnp.concatenate([ctx, x], axis=1)

    cc = jnp.zeros((MOD_ROWS, d), F32).at[:b].set(c).at[b].set(c_ctx)
    mod = _modulation(cc, mod_w, mod_b).reshape(DEPTH, MOD_ROWS, 6, d)
    mod_lat = mod[:, :b]
    mod_ctx = jnp.broadcast_to(mod[:, b:b + 1], mod_lat.shape)
    mod_tab = jnp.stack([mod_ctx, mod_lat], axis=2)
    mod_tab = jnp.pad(mod_tab, ((0, 0), (0, 0), (0, 0), (0, MOD_ROWS - 6), (0, 0)))

    cos_t, sin_t = _rope_tables(n_lat)
    bd = jnp.kron(jnp.eye(256 // hd, dtype=F32), jnp.ones((hd, hd), F32)).astype(BF16)

    w = ev_w_in[0]
    qa, ka, va, qb, kb, vb = jnp.split(w, [512, 640, 768, 1280, 1408], axis=1)
    p8, p2 = _deinterleave_perm(8), _deinterleave_perm(2)
    w0 = jnp.concatenate([qa[:, p8], qb[:, p8], ka[:, p2], kb[:, p2], va, vb], axis=1).astype(BF16)
    g0 = _gain_row([(ev_qn_a[0], 8), (ev_qn_b[0], 8), (ev_kn_a[0], 2), (ev_kn_b[0], 2)])
    q0, k0, vt0 = _inproj(xs, mod_tab[0], norm_mix_w[0][None, :], w0, g0, cos_t, sin_t, bd,
                          n_k=4, n_v=4, d_v=hd)
    tq_a = 128
    sink_rows = jnp.repeat(ev_sink_a[0].astype(F32) * LOG2E, tq_a).reshape(2, 1, 4 * tq_a)
    o_a = _window_attention(q0, k0, vt0, sink_rows, tq=tq_a)
    o_b = _dense_attention(q0, k0, vt0, tq=128, tk=512)
    xs = _post(xs, mod_tab[0], [o_a, o_b], ev_w_out[0].astype(BF16), norm_ffn_w[0][None, :],
               ffn_w_in[0].astype(BF16), ffn_w_out[0].astype(BF16), latent_only=False)

    w = od_w_in[0]
    p16 = _deinterleave_perm(16)
    w1 = jnp.concatenate([w[:, :1024][:, p16], w[:, 1024:2048][:, p16], w[:, 2048:]], axis=1).astype(BF16)
    g1 = _gain_row([(od_qn[0], 16), (od_kn[0], 16)])
    q1, k1, vt1 = _inproj(xs, mod_tab[1], norm_mix_w[1][None, :], w1, g1, cos_t, sin_t, bd,
                          n_k=16, n_v=8, d_v=2 * hd)
    lam_init = 0.8 - 0.6 * math.exp(-0.3 * 1)
    lam_vecs = jnp.stack([od_lq1[0], od_lk1[0], od_lq2[0], od_lk2[0]]).astype(F32)
    o_c = _diff_attention(q1, k1, vt1, lam_vecs, od_subln[0].astype(F32)[:, None],
                          tq=256, tk=512, lam_init=lam_init)
    return _post(xs, mod_tab[1], [o_c], od_w_out[0].astype(BF16), norm_ffn_w[1][None, :],
                 ffn_w_in[1].astype(BF16), ffn_w_out[1].astype(BF16), latent_only=True)
```

```python
import functools
import math

import jax
import jax.numpy as jnp
from jax import lax
from jax.experimental import pallas as pl
from jax.experimental.pallas import tpu as pltpu

D_MODEL = 1024
HEAD_DIM = 64
CTX_LEN = 256
GRID_W = 64
WINDOW = 128
FFN_HIDDEN = 2816
DEPTH = 2
ROPE_THETA = 10000.0
EPS = 1e-6
NEG_INF = -1e30
LOG2E = 1.4426950408889634
Q_SCALE = HEAD_DIM ** -0.5 * LOG2E
MOD_ROWS = 8
ROW_TILE = 256
FLASH_BUFFERS = 4
SOFTMAX_ROWS = 64
SUM_ROWS = 16
VMEM_LIMIT = 56 * 1024 * 1024

F32 = jnp.float32
BF16 = jnp.bfloat16


def _nt_dot(a, b):
    return lax.dot_general(a, b, (((1,), (1,)), ((), ())), preferred_element_type=F32)


def _dot(a, b):
    return jnp.dot(a, b, preferred_element_type=F32)


def _params(sem):
    return pltpu.CompilerParams(dimension_semantics=sem, vmem_limit_bytes=VMEM_LIMIT)


def _mod_kernel(cc_ref, w_ref, b_ref, o_ref):
    a = cc_ref[...]
    a = a / (1.0 + jnp.exp(-a))
    o_ref[0] = _dot(a.astype(BF16), w_ref[0].astype(BF16)) + b_ref[0]


def _modulation(cc, mod_w, mod_b):
    depth, d, n = mod_w.shape
    tn = 1536
    return pl.pallas_call(
        _mod_kernel,
        grid=(depth, n // tn),
        in_specs=[
            pl.BlockSpec((MOD_ROWS, d), lambda l, j: (0, 0)),
            pl.BlockSpec((1, d, tn), lambda l, j: (l, 0, j)),
            pl.BlockSpec((1, 1, tn), lambda l, j: (l, 0, j)),
        ],
        out_specs=pl.BlockSpec((1, MOD_ROWS, tn), lambda l, j: (l, 0, j)),
        out_shape=jax.ShapeDtypeStruct((depth, MOD_ROWS, n), F32),
        compiler_params=_params(("arbitrary", "arbitrary")),
        name="modulation",
    )(cc, mod_w, mod_b.reshape(depth, 1, n))


def _modulated_norm(x, nw, shift, scale):
    ms = jnp.mean(x * x, axis=-1, keepdims=True)
    return (x * lax.rsqrt(ms + EPS) * nw) * (1.0 + scale) + shift


def _inproj_kernel(x_ref, mod_ref, nw_ref, w_ref, g_ref, cos_ref, sin_ref, bd_ref,
                   q_ref, k_ref, vt_ref, *, n_q, n_k, n_v, d_v):
    tm = x_ref.shape[1]
    h = _modulated_norm(x_ref[0], nw_ref[...], mod_ref[0, 0, 0:1, :], mod_ref[0, 0, 1:2, :])
    y = _dot(h.astype(BF16), w_ref[...])
    cos = cos_ref[...]
    sin = sin_ref[...]
    lane = lax.broadcasted_iota(jnp.int32, (tm, 128), 1)
    first_half = (lane % HEAD_DIM) < (HEAD_DIM // 2)
    n_norm = n_q + n_k * HEAD_DIM
    for c in range(n_norm // 256):
        yc = y[:, 256 * c:256 * (c + 1)]
        ss = _dot((yc * yc).astype(BF16), bd_ref[...])
        z = yc * lax.rsqrt(ss * (1.0 / HEAD_DIM) + EPS) * g_ref[:, 256 * c:256 * (c + 1)]
        for half in range(2):
            zc = z[:, 128 * half:128 * (half + 1)]
            partner = jnp.where(first_half, pltpu.roll(zc, 96, 1), pltpu.roll(zc, 32, 1))
            o = zc * cos + partner * sin
            col = 256 * c + 128 * half
            if col < n_q:
                q_ref[0, :, col:col + 128] = (o * Q_SCALE).astype(BF16)
            else:
                kh = (col - n_q) // HEAD_DIM
                k_ref[0, kh] = o[:, :HEAD_DIM].astype(BF16)
                k_ref[0, kh + 1] = o[:, HEAD_DIM:].astype(BF16)
    vt = y[:, n_norm:].T.astype(BF16)
    ones_rows = (lax.broadcasted_iota(jnp.int32, (SUM_ROWS, tm), 0) == 0).astype(BF16)
    for hv in range(n_v):
        vt_ref[0, hv, 0:d_v, :] = vt[d_v * hv:d_v * (hv + 1), :]
        vt_ref[0, hv, d_v:d_v + SUM_ROWS, :] = ones_rows


def _inproj(xs, mod_tab, nw, w, gains, cos_t, sin_t, bd, *, n_k, n_v, d_v):
    b, tb, d = xs.shape
    n_q = D_MODEL
    n_in = w.shape[1]
    tm = ROW_TILE
    kern = functools.partial(_inproj_kernel, n_q=n_q, n_k=n_k, n_v=n_v, d_v=d_v)
    return pl.pallas_call(
        kern,
        grid=(b, tb // tm),
        in_specs=[
            pl.BlockSpec((1, tm, d), lambda bi, i: (bi, i, 0)),
            pl.BlockSpec((1, 1, MOD_ROWS, d), lambda bi, i: (bi, jnp.minimum(i, 1), 0, 0)),
            pl.BlockSpec((1, d), lambda bi, i: (0, 0)),
            pl.BlockSpec((d, n_in), lambda bi, i: (0, 0)),
            pl.BlockSpec((1, gains.shape[1]), lambda bi, i: (0, 0)),
            pl.BlockSpec((tm, 128), lambda bi, i: (i, 0)),
            pl.BlockSpec((tm, 128), lambda bi, i: (i, 0)),
            pl.BlockSpec((256, 256), lambda bi, i: (0, 0)),
        ],
        out_specs=[
            pl.BlockSpec((1, tm, n_q), lambda bi, i: (bi, i, 0)),
            pl.BlockSpec((1, n_k, tm, HEAD_DIM), lambda bi, i: (bi, 0, i, 0)),
            pl.BlockSpec((1, n_v, d_v + SUM_ROWS, tm), lambda bi, i: (bi, 0, 0, i)),
        ],
        out_shape=[
            jax.ShapeDtypeStruct((b, tb, n_q), BF16),
            jax.ShapeDtypeStruct((b, n_k, tb, HEAD_DIM), BF16),
            jax.ShapeDtypeStruct((b, n_v, d_v + SUM_ROWS, tb), BF16),
        ],
        compiler_params=_params(("arbitrary", "arbitrary")),
        name="inproj",
    )(xs, mod_tab, nw, w, gains, cos_t, sin_t, bd)


def _softmax_step(s_ref, p_ref, m, chunk_max):
    m_new = chunk_max if m is None else jnp.maximum(m, chunk_max)
    for r in range(0, s_ref.shape[0], SOFTMAX_ROWS):
        p_ref[r:r + SOFTMAX_ROWS, :] = jnp.exp2(s_ref[r:r + SOFTMAX_ROWS, :] - m_new).astype(BF16)
    return m_new, (None if m is None else jnp.exp2(m - m_new))


def _flash_chunks(chunks, scores, vt_slice, scratch, finish):
    n_streams = len(scores)
    nbuf = n_streams * FLASH_BUFFERS
    s_refs, p_refs = scratch[:nbuf], scratch[nbuf:]

    def buf(refs, j, c):
        return refs[j * FLASH_BUFFERS + c % FLASH_BUFFERS].at[0:chunks[c][1], :]

    def write_scores(j, c):
        s = scores[j](*chunks[c])
        buf(s_refs, j, c)[...] = s
        return jnp.max(s, axis=0, keepdims=True)

    m, acc = [None] * n_streams, [None] * n_streams
    cmax = {(j, c): write_scores(j, c) for c in range(min(2, len(chunks))) for j in range(n_streams)}
    for c, (lo, size, _) in enumerate(chunks):
        for j in range(n_streams):
            if c + 2 < len(chunks):
                cmax[j, c + 2] = write_scores(j, c + 2)
            p_ref = buf(p_refs, j, c)
            m[j], alpha = _softmax_step(buf(s_refs, j, c), p_ref, m[j], cmax.pop((j, c)))
            pv = _dot(vt_slice(lo, size), p_ref[...])
            acc[j] = pv if alpha is None else alpha * acc[j] + pv
    finish(acc, m)


def _key_chunks(n_keys, tk):
    return [(0, CTX_LEN, "ctx")] + [(lo, tk, "lat") for lo in range(CTX_LEN, n_keys, tk)]


def _window_kernel(q_ref, k_ref, vt_ref, sink_ref, o_ref, *scratch):
    tq = q_ref.shape[1]
    tb = k_ref.shape[2]
    span = tq + 2 * WINDOW
    t = pl.program_id(2)
    q = q_ref[0]
    ws = pl.multiple_of(jnp.clip(tq * t - WINDOW, 0, tb - span), 128)
    kpos = ws - CTX_LEN + lax.broadcasted_iota(jnp.int32, (span, tq), 0)
    qpos = tq * t - CTX_LEN + lax.broadcasted_iota(jnp.int32, (span, tq), 1)
    ok = (jnp.abs(qpos - kpos) <= WINDOW) & (kpos >= 0) & (qpos >= 0)

    def make_scores(g):
        qh = q[:, HEAD_DIM * g:HEAD_DIM * (g + 1)]

        def score(lo, size, tag):
            s = _nt_dot(k_ref[0, 0, pl.ds(lo, size), :], qh)
            return jnp.where(ok, s, NEG_INF) if tag == "win" else s
        return score

    def vt_slice(lo, size):
        return vt_ref[0, 0, :, pl.ds(lo, size)]

    def finish(accs, ms):
        outs = []
        for g, (acc, m) in enumerate(zip(accs, ms)):
            sink = sink_ref[0, :, tq * g:tq * (g + 1)]
            m_all = jnp.maximum(m, sink)
            scale = jnp.exp2(m - m_all)
            l = acc[HEAD_DIM:HEAD_DIM + 1, :] * scale + jnp.exp2(sink - m_all)
            outs.append((acc[:HEAD_DIM, :] * scale / l).T)
        o_ref[0] = jnp.concatenate(outs, axis=1).astype(BF16)

    _flash_chunks([(0, CTX_LEN, "ctx"), (ws, span, "win")], [make_scores(g) for g in range(4)],
                  vt_slice, scratch, finish)


def _window_attention(q, k, vt, sink_rows, *, tq):
    b, tb, _ = q.shape
    return pl.pallas_call(
        _window_kernel,
        grid=(b, 2, tb // tq),
        in_specs=[
            pl.BlockSpec((1, tq, 256), lambda bi, h, i: (bi, i, h)),
            pl.BlockSpec((1, 1, tb, HEAD_DIM), lambda bi, h, i: (bi, h, 0, 0)),
            pl.BlockSpec((1, 1, HEAD_DIM + SUM_ROWS, tb), lambda bi, h, i: (bi, h, 0, 0)),
            pl.BlockSpec((1, 1, 4 * tq), lambda bi, h, i: (h, 0, 0)),
        ],
        out_specs=pl.BlockSpec((1, tq, 256), lambda bi, h, i: (bi, i, h)),
        out_shape=jax.ShapeDtypeStruct((b, tb, 512), BF16),
        scratch_shapes=_flash_scratch(tq + 2 * WINDOW, tq, 4),
        compiler_params=_params(("arbitrary", "arbitrary", "arbitrary")),
        name="window_attention",
    )(q, k, vt, sink_rows)


def _flash_scratch(tk, nq, n_streams):
    n = n_streams * FLASH_BUFFERS
    return [pltpu.VMEM((tk, nq), F32)] * n + [pltpu.VMEM((tk, nq), BF16)] * n


def _dense_kernel(q_ref, k_ref, vt_ref, o_ref, *scratch, tk):
    tq = q_ref.shape[1]
    tb = k_ref.shape[2]
    q = q_ref[0]
    q_heads = [q[:, HEAD_DIM * g:HEAD_DIM * (g + 1)] for g in range(4)]
    scores = [lambda lo, size, tag, qh=qh: _nt_dot(k_ref[0, 0, lo:lo + size, :], qh) for qh in q_heads]

    def vt_slice(lo, size):
        return vt_ref[0, 0, :, lo:lo + size]

    def finish(accs, ms):
        outs = [(acc[:HEAD_DIM, :] / acc[HEAD_DIM:HEAD_DIM + 1, :]).T for acc in accs]
        o_ref[0] = jnp.concatenate(outs, axis=1).astype(BF16)

    chunks = _key_chunks(tb, tk)
    is_latent = pl.program_id(2) * tq >= CTX_LEN
    pl.when(is_latent)(lambda: _flash_chunks(chunks, scores, vt_slice, scratch, finish))
    pl.when(jnp.logical_not(is_latent))(lambda: _flash_chunks(chunks[:1], scores, vt_slice, scratch, finish))


def _dense_attention(q, k, vt, *, tq, tk):
    b, tb, _ = q.shape
    assert (tb - CTX_LEN) % tk == 0
    return pl.pallas_call(
        functools.partial(_dense_kernel, tk=tk),
        grid=(b, 2, tb // tq),
        in_specs=[
            pl.BlockSpec((1, tq, 256), lambda bi, h, i: (bi, i, 2 + h)),
            pl.BlockSpec((1, 1, tb, HEAD_DIM), lambda bi, h, i: (bi, 2 + h, 0, 0)),
            pl.BlockSpec((1, 1, HEAD_DIM + SUM_ROWS, tb), lambda bi, h, i: (bi, 2 + h, 0, 0)),
        ],
        out_specs=pl.BlockSpec((1, tq, 256), lambda bi, h, i: (bi, i, h)),
        out_shape=jax.ShapeDtypeStruct((b, tb, 512), BF16),
        scratch_shapes=_flash_scratch(tk, tq, 4),
        compiler_params=_params(("arbitrary", "arbitrary", "arbitrary")),
        name="dense_attention",
    )(q, k, vt)


def _diff_kernel(q_ref, k_ref, vt_ref, lam_ref, subw_ref, o_ref, *scratch, tk, lam_init):
    tq = q_ref.shape[1]
    tb = k_ref.shape[2]
    q = q_ref[0]
    q1 = q[:, :HEAD_DIM]
    q2 = q[:, HEAD_DIM:]

    scores = [lambda lo, size, tag: _nt_dot(k_ref[0, 0, lo:lo + size, :], q1),
              lambda lo, size, tag: _nt_dot(k_ref[0, 1, lo:lo + size, :], q2)]

    def vt_slice(lo, size):
        return vt_ref[0, 0, :, lo:lo + size]

    def finish(accs, ms):
        ots = [acc[:2 * HEAD_DIM, :] / acc[2 * HEAD_DIM:2 * HEAD_DIM + 1, :] for acc in accs]
        lam =(jnp.exp(jnp.sum(lam_ref[0:1, :] * lam_ref[1:2, :], axis=-1, keepdims=True))
               - jnp.exp(jnp.sum(lam_ref[2:3, :] * lam_ref[3:4, :], axis=-1, keepdims=True)) + lam_init)
        o = ots[0] - lam * ots[1]
        ms = jnp.mean(o * o, axis=0, keepdims=True)
        o = (o * lax.rsqrt(ms + EPS) * subw_ref[...]) * (1.0 - lam_init)
        o_ref[0] = o.T.astype(BF16)

    _flash_chunks(_key_chunks(tb, tk), scores, vt_slice, scratch, finish)


def _diff_attention(q, k, vt, lam_vecs, subw, *, tq, tk, lam_init):
    b, tb, _ = q.shape
    n_lat = tb - CTX_LEN
    n_heads = vt.shape[1]
    d_v = vt.shape[2] - SUM_ROWS
    q_off = CTX_LEN // tq
    return pl.pallas_call(
        functools.partial(_diff_kernel, tk=tk, lam_init=lam_init),
        grid=(b, n_heads, n_lat // tq),
        in_specs=[
            pl.BlockSpec((1, tq, 2 * HEAD_DIM), lambda bi, h, i: (bi, i + q_off, h)),
            pl.BlockSpec((1, 2, tb, HEAD_DIM), lambda bi, h, i: (bi, h, 0, 0)),
            pl.BlockSpec((1, 1, d_v + SUM_ROWS, tb), lambda bi, h, i: (bi, h, 0, 0)),
            pl.BlockSpec((4, HEAD_DIM), lambda bi, h, i: (0, 0)),
            pl.BlockSpec((d_v, 1), lambda bi, h, i: (0, 0)),
        ],
        out_specs=pl.BlockSpec((1, tq, d_v), lambda bi, h, i: (bi, i, h)),
        out_shape=jax.ShapeDtypeStruct((b, n_lat, n_heads * d_v), BF16),
        scratch_shapes=_flash_scratch(tk, tq, 2),
        compiler_params=_params(("arbitrary", "arbitrary", "arbitrary")),
        name="diff_attention",
    )(q, k, vt, lam_vecs, subw)


def _post_kernel(*refs, n_attn):
    x_ref, mod_ref = refs[0], refs[1]
    o_refs = refs[2:2 + n_attn]
    wo_ref, nw_ref, wi_ref, wf_ref, out_ref = refs[2 + n_attn:]
    mod = mod_ref[0, 0]
    kw = wo_ref.shape[0] // n_attn
    a = _dot(o_refs[0][0], wo_ref[0:kw, :])
    for j in range(1, n_attn):
        a = a + _dot(o_refs[j][0], wo_ref[kw * j:kw * (j + 1), :])
    x1 = x_ref[0] + mod[2:3, :] * a
    h = _modulated_norm(x1, nw_ref[...], mod[3:4, :], mod[4:5, :])
    u = _dot(h.astype(BF16), wi_ref[...])
    f = wf_ref.shape[0]
    gate = u[:, :f]
    act = (gate / (1.0 + jnp.exp(-gate))) * u[:, f:]
    y = _dot(act.astype(BF16), wf_ref[...])
    out_ref[0] = x1 + mod[5:6, :] * y


def _post(xs, mod_tab, attn_outs, wo, nw, wi, wf, *, latent_only):
    b, tb, d = xs.shape
    tm = ROW_TILE
    off = CTX_LEN // tm if latent_only else 0
    n_rows = tb - CTX_LEN if latent_only else tb
    n_attn = len(attn_outs)

    def const(shape):
        return pl.BlockSpec(shape, lambda bi, i: (0,) * len(shape))

    in_specs = [
        pl.BlockSpec((1, tm, d), lambda bi, i: (bi, i + off, 0)),
        pl.BlockSpec((1, 1, MOD_ROWS, d), lambda bi, i: (bi, jnp.minimum(i + off, 1), 0, 0)),
    ]
    for o in attn_outs:
        in_specs.append(pl.BlockSpec((1, tm, o.shape[2]), lambda bi, i: (bi, i, 0)))
    in_specs += [const(wo.shape), const(nw.shape), const(wi.shape), const(wf.shape)]
    return pl.pallas_call(
        functools.partial(_post_kernel, n_attn=n_attn),
        grid=(b, n_rows // tm),
        in_specs=in_specs,
        out_specs=pl.BlockSpec((1, tm, d), lambda bi, i: (bi, i, 0)),
        out_shape=jax.ShapeDtypeStruct((b, n_rows, d), F32),
        compiler_params=_params(("arbitrary", "arbitrary")),
        name="post",
    )(xs, mod_tab, *attn_outs, wo, nw, wi, wf)


def _deinterleave_perm(n_heads):
    one = jnp.concatenate([jnp.arange(0, HEAD_DIM, 2), jnp.arange(1, HEAD_DIM, 2)])
    return (jnp.arange(n_heads)[:, None] * HEAD_DIM + one[None, :]).reshape(-1)


def _rope_tables(n_lat):
    rows = n_lat // GRID_W
    row = jnp.repeat(jnp.arange(rows, dtype=F32), GRID_W)
    col = jnp.tile(jnp.arange(GRID_W, dtype=F32), rows)
    n_freq = HEAD_DIM // 4
    inv = ROPE_THETA ** (-jnp.arange(n_freq, dtype=F32) / n_freq)
    ang = jnp.concatenate([row[:, None] * inv, col[:, None] * inv], axis=-1)
    cos, sin = jnp.cos(ang), jnp.sin(ang)
    cos = jnp.concatenate([jnp.ones((CTX_LEN, HEAD_DIM // 2), F32), cos], axis=0)
    sin = jnp.concatenate([jnp.zeros((CTX_LEN, HEAD_DIM // 2), F32), sin], axis=0)
    cos_t = jnp.tile(jnp.concatenate([cos, cos], axis=-1), (1, 2))
    sin_t = jnp.tile(jnp.concatenate([-sin, sin], axis=-1), (1, 2))
    return cos_t, sin_t


def _gain_row(parts):
    one = _deinterleave_perm(1)
    return jnp.concatenate([jnp.tile(g[one], n) for g, n in parts])[None, :].astype(F32)


def kernel(x, c, ctx, c_ctx, mod_w, mod_b, norm_mix_w, norm_ffn_w, ev_w_in, ev_w_out, ev_qn_a, ev_kn_a,
           ev_qn_b, ev_kn_b, ev_sink_a, od_w_in, od_w_out, od_qn, od_kn, od_lq1, od_lk1, od_lq2, od_lk2,
           od_subln, ffn_w_in, ffn_w_out):
    b, n_lat, d = x.shape
    assert d == D_MODEL and ctx.shape[1] == CTX_LEN and b < MOD_ROWS and n_lat % 512 == 0
    hd = HEAD_DIM

    xs = jnp.concatenate([ctx, x], axis=1)

    cc = jnp.zeros((MOD_ROWS, d), F32).at[:b].set(c).at[b].set(c_ctx)
    mod = _modulation(cc, mod_w, mod_b).reshape(DEPTH, MOD_ROWS, 6, d)
    mod_lat = mod[:, :b]
    mod_ctx = jnp.broadcast_to(mod[:, b:b + 1], mod_lat.shape)
    mod_tab = jnp.stack([mod_ctx, mod_lat], axis=2)
    mod_tab = jnp.pad(mod_tab, ((0, 0), (0, 0), (0, 0), (0, MOD_ROWS - 6), (0, 0)))

    cos_t, sin_t = _rope_tables(n_lat)
    bd = jnp.kron(jnp.eye(256 // hd, dtype=F32), jnp.ones((hd, hd), F32)).astype(BF16)

    w = ev_w_in[0]
    qa, ka, va, qb, kb, vb = jnp.split(w, [512, 640, 768, 1280, 1408], axis=1)
    p8, p2 = _deinterleave_perm(8), _deinterleave_perm(2)
    w0 = jnp.concatenate([qa[:, p8], qb[:, p8], ka[:, p2], kb[:, p2], va, vb], axis=1).astype(BF16)
    g0 = _gain_row([(ev_qn_a[0], 8), (ev_qn_b[0], 8), (ev_kn_a[0], 2), (ev_kn_b[0], 2)])
    q0, k0, vt0 = _inproj(xs, mod_tab[0], norm_mix_w[0][None, :], w0, g0, cos_t, sin_t, bd,
                          n_k=4, n_v=4, d_v=hd)
    tq_a = 256
    sink_rows = jnp.repeat(ev_sink_a[0].astype(F32) * LOG2E, tq_a).reshape(2, 1, 4 * tq_a)
    o_a = _window_attention(q0, k0, vt0, sink_rows, tq=tq_a)
    o_b = _dense_attention(q0, k0, vt0, tq=256, tk=512)
    xs = _post(xs, mod_tab[0], [o_a, o_b], ev_w_out[0].astype(BF16), norm_ffn_w[0][None, :],
               ffn_w_in[0].astype(BF16), ffn_w_out[0].astype(BF16), latent_only=False)

    w = od_w_in[0]
    p16 = _deinterleave_perm(16)
    w1 = jnp.concatenate([w[:, :1024][:, p16], w[:, 1024:2048][:, p16], w[:, 2048:]], axis=1).astype(BF16)
    g1 = _gain_row([(od_qn[0], 16), (od_kn[0], 16)])
    q1, k1, vt1 = _inproj(xs, mod_tab[1], norm_mix_w[1][None, :], w1, g1, cos_t, sin_t, bd,
                          n_k=16, n_v=8, d_v=2 * hd)
    lam_init = 0.8 - 0.6 * math.exp(-0.3 * 1)
    lam_vecs = jnp.stack([od_lq1[0], od_lk1[0], od_lq2[0], od_lk2[0]]).astype(F32)
    o_c = _diff_attention(q1, k1, vt1, lam_vecs, od_subln[0].astype(F32)[:, None],
                          tq=256, tk=512, lam_init=lam_init)
    return _post(xs, mod_tab[1], [o_c], od_w_out[0].astype(BF16), norm_ffn_w[1][None, :],
                 ffn_w_in[1].astype(BF16), ffn_w_out[1].astype(BF16), latent_only=True)
```

```python
import functools
import math

import jax
import jax.numpy as jnp
from jax import lax
from jax.experimental import pallas as pl
from jax.experimental.pallas import tpu as pltpu

D_MODEL = 1024
HEAD_DIM = 64
CTX_LEN = 256
GRID_W = 64
WINDOW = 128
FFN_HIDDEN = 2816
DEPTH = 2
ROPE_THETA = 10000.0
EPS = 1e-6
NEG_INF = -1e30
LOG2E = 1.4426950408889634
Q_SCALE = HEAD_DIM ** -0.5 * LOG2E
MOD_ROWS = 8
ROW_TILE = 256
FLASH_BUFFERS = 4
SOFTMAX_ROWS = 64
SUM_ROWS = 16
VMEM_LIMIT = 56 * 1024 * 1024

F32 = jnp.float32
BF16 = jnp.bfloat16


def _nt_dot(a, b):
    return lax.dot_general(a, b, (((1,), (1,)), ((), ())), preferred_element_type=F32)


def _dot(a, b):
    return jnp.dot(a, b, preferred_element_type=F32)


def _params(sem):
    return pltpu.CompilerParams(dimension_semantics=sem, vmem_limit_bytes=VMEM_LIMIT)


def _mod_kernel(cc_ref, w_ref, b_ref, o_ref):
    a = cc_ref[...]
    a = a / (1.0 + jnp.exp(-a))
    o_ref[0] = _dot(a.astype(BF16), w_ref[0].astype(BF16)) + b_ref[0]


def _modulation(cc, mod_w, mod_b):
    depth, d, n = mod_w.shape
    tn = 1536
    return pl.pallas_call(
        _mod_kernel,
        grid=(depth, n // tn),
        in_specs=[
            pl.BlockSpec((MOD_ROWS, d), lambda l, j: (0, 0)),
            pl.BlockSpec((1, d, tn), lambda l, j: (l, 0, j)),
            pl.BlockSpec((1, 1, tn), lambda l, j: (l, 0, j)),
        ],
        out_specs=pl.BlockSpec((1, MOD_ROWS, tn), lambda l, j: (l, 0, j)),
        out_shape=jax.ShapeDtypeStruct((depth, MOD_ROWS, n), F32),
        compiler_params=_params(("arbitrary", "arbitrary")),
        name="modulation",
    )(cc, mod_w, mod_b.reshape(depth, 1, n))


def _modulated_norm(x, nw, shift, scale):
    ms = jnp.mean(x * x, axis=-1, keepdims=True)
    return (x * lax.rsqrt(ms + EPS) * nw) * (1.0 + scale) + shift


def _stream_specs(xs, tm, off=0):
    if len(xs) == 1:
        return [pl.BlockSpec((1, tm, xs[0].shape[2]), lambda bi, i: (bi, i + off, 0))]
    assert tm == CTX_LEN and off == 0
    d = xs[0].shape[2]
    return [pl.BlockSpec((1, tm, d), lambda bi, i: (bi, 0, 0)),
            pl.BlockSpec((1, tm, d), lambda bi, i: (bi, jnp.maximum(i - 1, 0), 0))]


def _stream_tile(x_refs):
    if len(x_refs) == 1:
        return x_refs[0][0]
    return jnp.where(pl.program_id(1) == 0, x_refs[0][0], x_refs[1][0])


def _inproj_kernel(*refs, n_q, n_k, n_v, d_v):
    x_refs = refs[:-10]
    mod_ref, nw_ref, w_ref, g_ref, cos_ref, sin_ref, bd_ref, q_ref, k_ref, vt_ref = refs[-10:]
    tm = q_ref.shape[1]
    h = _modulated_norm(_stream_tile(x_refs), nw_ref[...], mod_ref[0, 0, 0:1, :], mod_ref[0, 0, 1:2, :])
    y = _dot(h.astype(BF16), w_ref[...])
    cos = cos_ref[...]
    sin = sin_ref[...]
    lane = lax.broadcasted_iota(jnp.int32, (tm, 128), 1)
    first_half = (lane % HEAD_DIM) < (HEAD_DIM // 2)
    n_norm = n_q + n_k * HEAD_DIM
    for c in range(n_norm // 256):
        yc = y[:, 256 * c:256 * (c + 1)]
        ss = _dot((yc * yc).astype(BF16), bd_ref[...])
        z = yc * lax.rsqrt(ss * (1.0 / HEAD_DIM) + EPS) * g_ref[:, 256 * c:256 * (c + 1)]
        for half in range(2):
            zc = z[:, 128 * half:128 * (half + 1)]
            partner = jnp.where(first_half, pltpu.roll(zc, 96, 1), pltpu.roll(zc, 32, 1))
            o = zc * cos + partner * sin
            col = 256 * c + 128 * half
            if col < n_q:
                q_ref[0, :, col:col + 128] = (o * Q_SCALE).astype(BF16)
            else:
                kh = (col - n_q) // HEAD_DIM
                k_ref[0, kh] = o[:, :HEAD_DIM].astype(BF16)
                k_ref[0, kh + 1] = o[:, HEAD_DIM:].astype(BF16)
    vt = y[:, n_norm:].T.astype(BF16)
    ones_rows = (lax.broadcasted_iota(jnp.int32, (SUM_ROWS, tm), 0) == 0).astype(BF16)
    for hv in range(n_v):
        vt_ref[0, hv, 0:d_v, :] = vt[d_v * hv:d_v * (hv + 1), :]
        vt_ref[0, hv, d_v:d_v + SUM_ROWS, :] = ones_rows


def _inproj(xs, mod_tab, nw, w, gains, cos_t, sin_t, bd, *, n_k, n_v, d_v):
    b, d = xs[0].shape[0], xs[0].shape[2]
    tb = sum(a.shape[1] for a in xs)
    n_q = D_MODEL
    n_in = w.shape[1]
    tm = ROW_TILE
    kern = functools.partial(_inproj_kernel, n_q=n_q, n_k=n_k, n_v=n_v, d_v=d_v)
    return pl.pallas_call(
        kern,
        grid=(b, tb // tm),
        in_specs=_stream_specs(xs, tm) + [
            pl.BlockSpec((1, 1, MOD_ROWS, d), lambda bi, i: (bi, jnp.minimum(i, 1), 0, 0)),
            pl.BlockSpec((1, d), lambda bi, i: (0, 0)),
            pl.BlockSpec((d, n_in), lambda bi, i: (0, 0)),
            pl.BlockSpec((1, gains.shape[1]), lambda bi, i: (0, 0)),
            pl.BlockSpec((tm, 128), lambda bi, i: (i, 0)),
            pl.BlockSpec((tm, 128), lambda bi, i: (i, 0)),
            pl.BlockSpec((256, 256), lambda bi, i: (0, 0)),
        ],
        out_specs=[
            pl.BlockSpec((1, tm, n_q), lambda bi, i: (bi, i, 0)),
            pl.BlockSpec((1, n_k, tm, HEAD_DIM), lambda bi, i: (bi, 0, i, 0)),
            pl.BlockSpec((1, n_v, d_v + SUM_ROWS, tm), lambda bi, i: (bi, 0, 0, i)),
        ],
        out_shape=[
            jax.ShapeDtypeStruct((b, tb, n_q), BF16),
            jax.ShapeDtypeStruct((b, n_k, tb, HEAD_DIM), BF16),
            jax.ShapeDtypeStruct((b, n_v, d_v + SUM_ROWS, tb), BF16),
        ],
        compiler_params=_params(("arbitrary", "arbitrary")),
        name="inproj",
    )(*xs, mod_tab, nw, w, gains, cos_t, sin_t, bd)


def _softmax_step(s_ref, p_ref, m, chunk_max):
    m_new = chunk_max if m is None else jnp.maximum(m, chunk_max)
    for r in range(0, s_ref.shape[0], SOFTMAX_ROWS):
        p_ref[r:r + SOFTMAX_ROWS, :] = jnp.exp2(s_ref[r:r + SOFTMAX_ROWS, :] - m_new).astype(BF16)
    return m_new, (None if m is None else jnp.exp2(m - m_new))


def _flash_chunks(chunks, scores, vt_slice, scratch, finish):
    n_streams = len(scores)
    nbuf = n_streams * FLASH_BUFFERS
    s_refs, p_refs = scratch[:nbuf], scratch[nbuf:]

    def buf(refs, j, c):
        return refs[j * FLASH_BUFFERS + c % FLASH_BUFFERS].at[0:chunks[c][1], :]

    def write_scores(j, c):
        s = scores[j](*chunks[c])
        buf(s_refs, j, c)[...] = s
        return jnp.max(s, axis=0, keepdims=True)

    m, acc = [None] * n_streams, [None] * n_streams
    cmax = {(j, c): write_scores(j, c) for c in range(min(2, len(chunks))) for j in range(n_streams)}
    for c, (lo, size, _) in enumerate(chunks):
        for j in range(n_streams):
            if c + 2 < len(chunks):
                cmax[j, c + 2] = write_scores(j, c + 2)
            p_ref = buf(p_refs, j, c)
            m[j], alpha = _softmax_step(buf(s_refs, j, c), p_ref, m[j], cmax.pop((j, c)))
            pv = _dot(vt_slice(lo, size), p_ref[...])
            acc[j] = pv if alpha is None else alpha * acc[j] + pv
    finish(acc, m)


def _key_chunks(n_keys, tk):
    return [(0, CTX_LEN, "ctx")] + [(lo, tk, "lat") for lo in range(CTX_LEN, n_keys, tk)]


def _window_kernel(q_ref, k_ref, vt_ref, sink_ref, o_ref, *scratch):
    tq = q_ref.shape[1]
    tb = k_ref.shape[2]
    span = tq + 2 * WINDOW
    t = pl.program_id(2)
    q = q_ref[0]
    ws = pl.multiple_of(jnp.clip(tq * t - WINDOW, 0, tb - span), 128)
    kpos = ws - CTX_LEN + lax.broadcasted_iota(jnp.int32, (span, tq), 0)
    qpos = tq * t - CTX_LEN + lax.broadcasted_iota(jnp.int32, (span, tq), 1)
    ok = (jnp.abs(qpos - kpos) <= WINDOW) & (kpos >= 0) & (qpos >= 0)

    def make_scores(g):
        qh = q[:, HEAD_DIM * g:HEAD_DIM * (g + 1)]

        def score(lo, size, tag):
            s = _nt_dot(k_ref[0, 0, pl.ds(lo, size), :], qh)
            return jnp.where(ok, s, NEG_INF) if tag == "win" else s
        return score

    def vt_slice(lo, size):
        return vt_ref[0, 0, :, pl.ds(lo, size)]

    def finish(accs, ms):
        outs = []
        for g, (acc, m) in enumerate(zip(accs, ms)):
            sink = sink_ref[0, :, tq * g:tq * (g + 1)]
            m_all = jnp.maximum(m, sink)
            scale = jnp.exp2(m - m_all)
            l = acc[HEAD_DIM:HEAD_DIM + 1, :] * scale + jnp.exp2(sink - m_all)
            outs.append((acc[:HEAD_DIM, :] * scale / l).T)
        o_ref[0] = jnp.concatenate(outs, axis=1).astype(BF16)

    _flash_chunks([(0, CTX_LEN, "ctx"), (ws, span, "win")], [make_scores(g) for g in range(4)],
                  vt_slice, scratch, finish)


def _window_attention(q, k, vt, sink_rows, *, tq):
    b, tb, _ = q.shape
    return pl.pallas_call(
        _window_kernel,
        grid=(b, 2, tb // tq),
        in_specs=[
            pl.BlockSpec((1, tq, 256), lambda bi, h, i: (bi, i, h)),
            pl.BlockSpec((1, 1, tb, HEAD_DIM), lambda bi, h, i: (bi, h, 0, 0)),
            pl.BlockSpec((1, 1, HEAD_DIM + SUM_ROWS, tb), lambda bi, h, i: (bi, h, 0, 0)),
            pl.BlockSpec((1, 1, 4 * tq), lambda bi, h, i: (h, 0, 0)),
        ],
        out_specs=pl.BlockSpec((1, tq, 256), lambda bi, h, i: (bi, i, h)),
        out_shape=jax.ShapeDtypeStruct((b, tb, 512), BF16),
        scratch_shapes=_flash_scratch(tq + 2 * WINDOW, tq, 4),
        compiler_params=_params(("arbitrary", "arbitrary", "arbitrary")),
        name="window_attention",
    )(q, k, vt, sink_rows)


def _flash_scratch(tk, nq, n_streams):
    n = n_streams * FLASH_BUFFERS
    return [pltpu.VMEM((tk, nq), F32)] * n + [pltpu.VMEM((tk, nq), BF16)] * n


def _dense_kernel(q_ref, k_ref, vt_ref, o_ref, *scratch, tk):
    tq = q_ref.shape[1]
    tb = k_ref.shape[2]
    q = q_ref[0]
    q_heads = [q[:, HEAD_DIM * g:HEAD_DIM * (g + 1)] for g in range(4)]
    scores = [lambda lo, size, tag, qh=qh: _nt_dot(k_ref[0, 0, lo:lo + size, :], qh) for qh in q_heads]

    def vt_slice(lo, size):
        return vt_ref[0, 0, :, lo:lo + size]

    def finish(accs, ms):
        outs = [(acc[:HEAD_DIM, :] / acc[HEAD_DIM:HEAD_DIM + 1, :]).T for acc in accs]
        o_ref[0] = jnp.concatenate(outs, axis=1).astype(BF16)

    chunks = _key_chunks(tb, tk)
    is_latent = pl.program_id(2) * tq >= CTX_LEN
    pl.when(is_latent)(lambda: _flash_chunks(chunks, scores, vt_slice, scratch, finish))
    pl.when(jnp.logical_not(is_latent))(lambda: _flash_chunks(chunks[:1], scores, vt_slice, scratch, finish))


def _dense_attention(q, k, vt, *, tq, tk):
    b, tb, _ = q.shape
    assert (tb - CTX_LEN) % tk == 0
    return pl.pallas_call(
        functools.partial(_dense_kernel, tk=tk),
        grid=(b, 2, tb // tq),
        in_specs=[
            pl.BlockSpec((1, tq, 256), lambda bi, h, i: (bi, i, 2 + h)),
            pl.BlockSpec((1, 1, tb, HEAD_DIM), lambda bi, h, i: (bi, 2 + h, 0, 0)),
            pl.BlockSpec((1, 1, HEAD_DIM + SUM_ROWS, tb), lambda bi, h, i: (bi, 2 + h, 0, 0)),
        ],
        out_specs=pl.BlockSpec((1, tq, 256), lambda bi, h, i: (bi, i, h)),
        out_shape=jax.ShapeDtypeStruct((b, tb, 512), BF16),
        scratch_shapes=_flash_scratch(tk, tq, 4),
        compiler_params=_params(("arbitrary", "arbitrary", "arbitrary")),
        name="dense_attention",
    )(q, k, vt)


def _diff_kernel(qa_ref, qb_ref, k_ref, vt_ref, lam_ref, subw_ref, o_ref, *scratch, tk, lam_init):
    tb = k_ref.shape[2]
    qs = [(a, q_ref[0, :, HEAD_DIM * a:HEAD_DIM * (a + 1)]) for a in range(2) for q_ref in (qa_ref, qb_ref)]
    scores = [lambda lo, size, tag, a=a, qh=qh: _nt_dot(k_ref[0, a, lo:lo + size, :], qh) for a, qh in qs]

    def vt_slice(lo, size):
        return vt_ref[0, 0, :, lo:lo + size]

    def finish(accs, ms):
        ots = [acc[:2 * HEAD_DIM, :] / acc[2 * HEAD_DIM:2 * HEAD_DIM + 1, :] for acc in accs]
        ots = [jnp.concatenate(ots[0:2], axis=1), jnp.concatenate(ots[2:4], axis=1)]
        lam = (jnp.exp(jnp.sum(lam_ref[0:1, :] * lam_ref[1:2, :], axis=-1, keepdims=True))
               - jnp.exp(jnp.sum(lam_ref[2:3, :] * lam_ref[3:4, :], axis=-1, keepdims=True)) + lam_init)
        o = ots[0] - lam * ots[1]
        ms = jnp.mean(o * o, axis=0, keepdims=True)
        o = (o * lax.rsqrt(ms + EPS) * subw_ref[...]) * (1.0 - lam_init)
        o_ref[0] = o.T.astype(BF16)

    _flash_chunks(_key_chunks(tb, tk), scores, vt_slice, scratch, finish)


def _diff_attention(q, k, vt, lam_vecs, subw, *, tq, tk, lam_init):
    b, tb, _ = q.shape
    n_lat = tb - CTX_LEN
    n_heads = vt.shape[1]
    d_v = vt.shape[2] - SUM_ROWS
    hq = tq // 2
    assert CTX_LEN % hq == 0
    q_off = CTX_LEN // hq
    return pl.pallas_call(
        functools.partial(_diff_kernel, tk=tk, lam_init=lam_init),
        grid=(b, n_heads, n_lat // tq),
        in_specs=[
            pl.BlockSpec((1, hq, 2 * HEAD_DIM), lambda bi, h, i: (bi, 2 * i + q_off, h)),
            pl.BlockSpec((1, hq, 2 * HEAD_DIM), lambda bi, h, i: (bi, 2 * i + 1 + q_off, h)),
            pl.BlockSpec((1, 2, tb, HEAD_DIM), lambda bi, h, i: (bi, h, 0, 0)),
            pl.BlockSpec((1, 1, d_v + SUM_ROWS, tb), lambda bi, h, i: (bi, h, 0, 0)),
            pl.BlockSpec((4, HEAD_DIM), lambda bi, h, i: (0, 0)),
            pl.BlockSpec((d_v, 1), lambda bi, h, i: (0, 0)),
        ],
        out_specs=pl.BlockSpec((1, tq, d_v), lambda bi, h, i: (bi, i, h)),
        out_shape=jax.ShapeDtypeStruct((b, n_lat, n_heads * d_v), BF16),
        scratch_shapes=_flash_scratch(tk, tq // 2, 4),
        compiler_params=_params(("arbitrary", "arbitrary", "arbitrary")),
        name="diff_attention",
    )(q, q, k, vt, lam_vecs, subw)


def _post_kernel(*refs, n_attn):
    n_x = len(refs) - n_attn - 6
    x_refs, mod_ref = refs[:n_x], refs[n_x]
    o_refs = refs[n_x + 1:n_x + 1 + n_attn]
    wo_ref, nw_ref, wi_ref, wf_ref, out_ref = refs[n_x + 1 + n_attn:]
    mod = mod_ref[0, 0]
    kw = wo_ref.shape[0] // n_attn
    a = _dot(o_refs[0][0], wo_ref[0:kw, :])
    for j in range(1, n_attn):
        a = a + _dot(o_refs[j][0], wo_ref[kw * j:kw * (j + 1), :])
    x1 = _stream_tile(x_refs) + mod[2:3, :] * a
    h = _modulated_norm(x1, nw_ref[...], mod[3:4, :], mod[4:5, :])
    u = _dot(h.astype(BF16), wi_ref[...])
    f = wf_ref.shape[0]
    gate = u[:, :f]
    act = (gate / (1.0 + jnp.exp(-gate))) * u[:, f:]
    y = _dot(act.astype(BF16), wf_ref[...])
    out_ref[0] = x1 + mod[5:6, :] * y


def _post(xs, mod_tab, attn_outs, wo, nw, wi, wf, *, latent_only):
    b, d = xs[0].shape[0], xs[0].shape[2]
    tb = sum(a.shape[1] for a in xs)
    tm = ROW_TILE
    off = CTX_LEN // tm if latent_only else 0
    n_rows = tb - CTX_LEN if latent_only else tb
    n_attn = len(attn_outs)

    def const(shape):
        return pl.BlockSpec(shape, lambda bi, i: (0,) * len(shape))

    in_specs = _stream_specs(xs, tm, off) + [
        pl.BlockSpec((1, 1, MOD_ROWS, d), lambda bi, i: (bi, jnp.minimum(i + off, 1), 0, 0)),
    ]
    for o in attn_outs:
        in_specs.append(pl.BlockSpec((1, tm, o.shape[2]), lambda bi, i: (bi, i, 0)))
    in_specs += [const(wo.shape), const(nw.shape), const(wi.shape), const(wf.shape)]
    return pl.pallas_call(
        functools.partial(_post_kernel, n_attn=n_attn),
        grid=(b, n_rows // tm),
        in_specs=in_specs,
        out_specs=pl.BlockSpec((1, tm, d), lambda bi, i: (bi, i, 0)),
        out_shape=jax.ShapeDtypeStruct((b, n_rows, d), F32),
        compiler_params=_params(("arbitrary", "arbitrary")),
        name="post",
    )(*xs, mod_tab, *attn_outs, wo, nw, wi, wf)


def _deinterleave_perm(n_heads):
    one = jnp.concatenate([jnp.arange(0, HEAD_DIM, 2), jnp.arange(1, HEAD_DIM, 2)])
    return (jnp.arange(n_heads)[:, None] * HEAD_DIM + one[None, :]).reshape(-1)


def _rope_tables(n_lat):
    rows = n_lat // GRID_W
    row = jnp.repeat(jnp.arange(rows, dtype=F32), GRID_W)
    col = jnp.tile(jnp.arange(GRID_W, dtype=F32), rows)
    n_freq = HEAD_DIM // 4
    inv = ROPE_THETA ** (-jnp.arange(n_freq, dtype=F32) / n_freq)
    ang = jnp.concatenate([row[:, None] * inv, col[:, None] * inv], axis=-1)
    cos, sin = jnp.cos(ang), jnp.sin(ang)
    cos = jnp.concatenate([jnp.ones((CTX_LEN, HEAD_DIM // 2), F32), cos], axis=0)
    sin = jnp.concatenate([jnp.zeros((CTX_LEN, HEAD_DIM // 2), F32), sin], axis=0)
    cos_t = jnp.tile(jnp.concatenate([cos, cos], axis=-1), (1, 2))
    sin_t = jnp.tile(jnp.concatenate([-sin, sin], axis=-1), (1, 2))
    return cos_t, sin_t


def _gain_row(parts):
    one = _deinterleave_perm(1)
    return jnp.concatenate([jnp.tile(g[one], n) for g, n in parts])[None, :].astype(F32)


def kernel(x, c, ctx, c_ctx, mod_w, mod_b, norm_mix_w, norm_ffn_w, ev_w_in, ev_w_out, ev_qn_a, ev_kn_a,
           ev_qn_b, ev_kn_b, ev_sink_a, od_w_in, od_w_out, od_qn, od_kn, od_lq1, od_lk1, od_lq2, od_lk2,
           od_subln, ffn_w_in, ffn_w_out):
    b, n_lat, d = x.shape
    assert d == D_MODEL and ctx.shape[1] == CTX_LEN and b < MOD_ROWS and n_lat % 512 == 0
    hd = HEAD_DIM

    cc = jnp.zeros((MOD_ROWS, d), F32).at[:b].set(c).at[b].set(c_ctx)
    mod = _modulation(cc, mod_w, mod_b).reshape(DEPTH, MOD_ROWS, 6, d)
    mod_lat = mod[:, :b]
    mod_ctx = jnp.broadcast_to(mod[:, b:b + 1], mod_lat.shape)
    mod_tab = jnp.stack([mod_ctx, mod_lat], axis=2)
    mod_tab = jnp.pad(mod_tab, ((0, 0), (0, 0), (0, 0), (0, MOD_ROWS - 6), (0, 0)))

    cos_t, sin_t = _rope_tables(n_lat)
    bd = jnp.kron(jnp.eye(256 // hd, dtype=F32), jnp.ones((hd, hd), F32)).astype(BF16)

    w = ev_w_in[0]
    qa, ka, va, qb, kb, vb = jnp.split(w, [512, 640, 768, 1280, 1408], axis=1)
    p8, p2 = _deinterleave_perm(8), _deinterleave_perm(2)
    w0 = jnp.concatenate([qa[:, p8], qb[:, p8], ka[:, p2], kb[:, p2], va, vb], axis=1).astype(BF16)
    g0 = _gain_row([(ev_qn_a[0], 8), (ev_qn_b[0], 8), (ev_kn_a[0], 2), (ev_kn_b[0], 2)])
    q0, k0, vt0 = _inproj((ctx, x), mod_tab[0], norm_mix_w[0][None, :], w0, g0, cos_t, sin_t, bd,
                          n_k=4, n_v=4, d_v=hd)
    tq_a = 256
    sink_rows = jnp.repeat(ev_sink_a[0].astype(F32) * LOG2E, tq_a).reshape(2, 1, 4 * tq_a)
    o_a = _window_attention(q0, k0, vt0, sink_rows, tq=tq_a)
    o_b = _dense_attention(q0, k0, vt0, tq=256, tk=256)
    xs = _post((ctx, x), mod_tab[0], [o_a, o_b], ev_w_out[0].astype(BF16), norm_ffn_w[0][None, :],
               ffn_w_in[0].astype(BF16), ffn_w_out[0].astype(BF16), latent_only=False)

    w = od_w_in[0]
    p16 = _deinterleave_perm(16)
    w1 = jnp.concatenate([w[:, :1024][:, p16], w[:, 1024:2048][:, p16], w[:, 2048:]], axis=1).astype(BF16)
    g1 = _gain_row([(od_qn[0], 16), (od_kn[0], 16)])
    q1, k1, vt1 = _inproj((xs,), mod_tab[1], norm_mix_w[1][None, :], w1, g1, cos_t, sin_t, bd,
                          n_k=16, n_v=8, d_v=2 * hd)
    lam_init = 0.8 - 0.6 * math.exp(-0.3 * 1)
    lam_vecs = jnp.stack([od_lq1[0], od_lk1[0], od_lq2[0], od_lk2[0]]).astype(F32)
    o_c = _diff_attention(q1, k1, vt1, lam_vecs, od_subln[0].astype(F32)[:, None],
                          tq=512, tk=256, lam_init=lam_init)
    return _post((xs,), mod_tab[1], [o_c], od_w_out[0].astype(BF16), norm_ffn_w[1][None, :],
                 ffn_w_in[1].astype(BF16), ffn_w_out[1].astype(BF16), latent_only=True)
```

```python
import functools
import math

import jax
import jax.numpy as jnp
from jax import lax
from jax.experimental import pallas as pl
from jax.experimental.pallas import tpu as pltpu

D_MODEL = 1024
HEAD_DIM = 64
CTX_LEN = 256
GRID_W = 64
WINDOW = 128
FFN_HIDDEN = 2816
DEPTH = 2
ROPE_THETA = 10000.0
EPS = 1e-6
NEG_INF = -1e30
LOG2E = 1.4426950408889634
Q_SCALE = HEAD_DIM ** -0.5 * LOG2E
MOD_ROWS = 8
ROW_TILE = 256
FLASH_BUFFERS = 4
SOFTMAX_ROWS = 64
SUM_ROWS = 16
VMEM_LIMIT = 56 * 1024 * 1024

F32 = jnp.float32
BF16 = jnp.bfloat16


def _nt_dot(a, b):
    return lax.dot_general(a, b, (((1,), (1,)), ((), ())), preferred_element_type=F32)


def _dot(a, b):
    return jnp.dot(a, b, preferred_element_type=F32)


def _params(sem):
    return pltpu.CompilerParams(dimension_semantics=sem, vmem_limit_bytes=VMEM_LIMIT)


def _mod_kernel(cc_ref, w_ref, b_ref, o_ref):
    a = cc_ref[...]
    a = a / (1.0 + jnp.exp(-a))
    o_ref[0] = _dot(a.astype(BF16), w_ref[0].astype(BF16)) + b_ref[0]


def _modulation(cc, mod_w, mod_b):
    depth, d, n = mod_w.shape
    tn = 1536
    return pl.pallas_call(
        _mod_kernel,
        grid=(depth, n // tn),
        in_specs=[
            pl.BlockSpec((MOD_ROWS, d), lambda l, j: (0, 0)),
            pl.BlockSpec((1, d, tn), lambda l, j: (l, 0, j)),
            pl.BlockSpec((1, 1, tn), lambda l, j: (l, 0, j)),
        ],
        out_specs=pl.BlockSpec((1, MOD_ROWS, tn), lambda l, j: (l, 0, j)),
        out_shape=jax.ShapeDtypeStruct((depth, MOD_ROWS, n), F32),
        compiler_params=_params(("arbitrary", "arbitrary")),
        name="modulation",
    )(cc, mod_w, mod_b.reshape(depth, 1, n))


def _modulated_norm(x, nw, shift, scale):
    ms = jnp.mean(x * x, axis=-1, keepdims=True)
    return (x * lax.rsqrt(ms + EPS) * nw) * (1.0 + scale) + shift


def _stream_specs(xs, tm, off=0):
    if len(xs) == 1:
        return [pl.BlockSpec((1, tm, xs[0].shape[2]), lambda bi, i: (bi, i + off, 0))]
    assert tm == CTX_LEN and off == 0
    d = xs[0].shape[2]
    return [pl.BlockSpec((1, tm, d), lambda bi, i: (bi, 0, 0)),
            pl.BlockSpec((1, tm, d), lambda bi, i: (bi, jnp.maximum(i - 1, 0), 0))]


def _stream_tile(x_refs):
    if len(x_refs) == 1:
        return x_refs[0][0]
    return jnp.where(pl.program_id(1) == 0, x_refs[0][0], x_refs[1][0])


def _inproj_kernel(*refs, n_q, n_k, n_v, d_v):
    x_refs = refs[:-11]
    mod_ref, nw_ref, w_ref, wvt_ref, g_ref, cos_ref, sin_ref, bd_ref, q_ref, k_ref, vt_ref = refs[-11:]
    tm = q_ref.shape[1]
    h = _modulated_norm(_stream_tile(x_refs), nw_ref[...], mod_ref[0, 0, 0:1, :], mod_ref[0, 0, 1:2, :])
    hb = h.astype(BF16)
    y = _dot(hb, w_ref[...])
    cos = cos_ref[...]
    sin = sin_ref[...]
    lane = lax.broadcasted_iota(jnp.int32, (tm, 128), 1)
    first_half = (lane % HEAD_DIM) < (HEAD_DIM // 2)
    n_norm = n_q + n_k * HEAD_DIM
    for c in range(n_norm // 256):
        yc = y[:, 256 * c:256 * (c + 1)]
        ss = _dot((yc * yc).astype(BF16), bd_ref[...])
        z = yc * lax.rsqrt(ss * (1.0 / HEAD_DIM) + EPS) * g_ref[:, 256 * c:256 * (c + 1)]
        for half in range(2):
            zc = z[:, 128 * half:128 * (half + 1)]
            partner = jnp.where(first_half, pltpu.roll(zc, 96, 1), pltpu.roll(zc, 32, 1))
            o = zc * cos + partner * sin
            col = 256 * c + 128 * half
            if col < n_q:
                q_ref[0, :, col:col + 128] = (o * Q_SCALE).astype(BF16)
            else:
                kh = (col - n_q) // HEAD_DIM
                k_ref[0, kh] = o[:, :HEAD_DIM].astype(BF16)
                k_ref[0, kh + 1] = o[:, HEAD_DIM:].astype(BF16)
    vt = _nt_dot(wvt_ref[...], hb).astype(BF16)
    ones_rows = (lax.broadcasted_iota(jnp.int32, (SUM_ROWS, tm), 0) == 0).astype(BF16)
    for hv in range(n_v):
        vt_ref[0, hv, 0:d_v, :] = vt[d_v * hv:d_v * (hv + 1), :]
        vt_ref[0, hv, d_v:d_v + SUM_ROWS, :] = ones_rows


def _inproj(xs, mod_tab, nw, w, wvt, gains, cos_t, sin_t, bd, *, n_k, n_v, d_v):
    b, d = xs[0].shape[0], xs[0].shape[2]
    tb = sum(a.shape[1] for a in xs)
    n_q = D_MODEL
    n_in = w.shape[1]
    tm = ROW_TILE
    kern = functools.partial(_inproj_kernel, n_q=n_q, n_k=n_k, n_v=n_v, d_v=d_v)
    return pl.pallas_call(
        kern,
        grid=(b, tb // tm),
        in_specs=_stream_specs(xs, tm) + [
            pl.BlockSpec((1, 1, MOD_ROWS, d), lambda bi, i: (bi, jnp.minimum(i, 1), 0, 0)),
            pl.BlockSpec((1, d), lambda bi, i: (0, 0)),
            pl.BlockSpec((d, n_in), lambda bi, i: (0, 0)),
            pl.BlockSpec((n_v * d_v, d), lambda bi, i: (0, 0)),
            pl.BlockSpec((1, gains.shape[1]), lambda bi, i: (0, 0)),
            pl.BlockSpec((tm, 128), lambda bi, i: (i, 0)),
            pl.BlockSpec((tm, 128), lambda bi, i: (i, 0)),
            pl.BlockSpec((256, 256), lambda bi, i: (0, 0)),
        ],
        out_specs=[
            pl.BlockSpec((1, tm, n_q), lambda bi, i: (bi, i, 0)),
            pl.BlockSpec((1, n_k, tm, HEAD_DIM), lambda bi, i: (bi, 0, i, 0)),
            pl.BlockSpec((1, n_v, d_v + SUM_ROWS, tm), lambda bi, i: (bi, 0, 0, i)),
        ],
        out_shape=[
            jax.ShapeDtypeStruct((b, tb, n_q), BF16),
            jax.ShapeDtypeStruct((b, n_k, tb, HEAD_DIM), BF16),
            jax.ShapeDtypeStruct((b, n_v, d_v + SUM_ROWS, tb), BF16),
        ],
        compiler_params=_params(("arbitrary", "arbitrary")),
        name="inproj",
    )(*xs, mod_tab, nw, w, wvt, gains, cos_t, sin_t, bd)


def _softmax_step(s_ref, p_ref, m, chunk_max):
    m_new = chunk_max if m is None else jnp.maximum(m, chunk_max)
    for r in range(0, s_ref.shape[0], SOFTMAX_ROWS):
        p_ref[r:r + SOFTMAX_ROWS, :] = jnp.exp2(s_ref[r:r + SOFTMAX_ROWS, :] - m_new).astype(BF16)
    return m_new, (None if m is None else jnp.exp2(m - m_new))


def _flash_chunks(chunks, scores, vt_slice, scratch, finish):
    n_streams = len(scores)
    nbuf = n_streams * FLASH_BUFFERS
    s_refs, p_refs = scratch[:nbuf], scratch[nbuf:]

    def buf(refs, j, c):
        return refs[j * FLASH_BUFFERS + c % FLASH_BUFFERS].at[0:chunks[c][1], :]

    def write_scores(j, c):
        s = scores[j](*chunks[c])
        buf(s_refs, j, c)[...] = s
        return jnp.max(s, axis=0, keepdims=True)

    m, acc = [None] * n_streams, [None] * n_streams
    cmax = {(j, c): write_scores(j, c) for c in range(min(2, len(chunks))) for j in range(n_streams)}
    for c, (lo, size, _) in enumerate(chunks):
        for j in range(n_streams):
            if c + 2 < len(chunks):
                cmax[j, c + 2] = write_scores(j, c + 2)
            p_ref = buf(p_refs, j, c)
            m[j], alpha = _softmax_step(buf(s_refs, j, c), p_ref, m[j], cmax.pop((j, c)))
            pv = _dot(vt_slice(lo, size), p_ref[...])
            acc[j] = pv if alpha is None else alpha * acc[j] + pv
    finish(acc, m)


def _key_chunks(n_keys, tk):
    return [(0, CTX_LEN, "ctx")] + [(lo, tk, "lat") for lo in range(CTX_LEN, n_keys, tk)]


def _window_kernel(q_ref, k_ref, vt_ref, sink_ref, o_ref, *scratch):
    tq = q_ref.shape[1]
    tb = k_ref.shape[2]
    span = tq + 2 * WINDOW
    t = pl.program_id(2)
    q = q_ref[0]
    ws = pl.multiple_of(jnp.clip(tq * t - WINDOW, 0, tb - span), 128)
    kpos = ws - CTX_LEN + lax.broadcasted_iota(jnp.int32, (span, tq), 0)
    qpos = tq * t - CTX_LEN + lax.broadcasted_iota(jnp.int32, (span, tq), 1)
    ok = (jnp.abs(qpos - kpos) <= WINDOW) & (kpos >= 0) & (qpos >= 0)

    def make_scores(g):
        qh = q[:, HEAD_DIM * g:HEAD_DIM * (g + 1)]

        def score(lo, size, tag):
            s = _nt_dot(k_ref[0, 0, pl.ds(lo, size), :], qh)
            return jnp.where(ok, s, NEG_INF) if tag == "win" else s
        return score

    def vt_slice(lo, size):
        return vt_ref[0, 0, :, pl.ds(lo, size)]

    def finish(accs, ms):
        outs = []
        for g, (acc, m) in enumerate(zip(accs, ms)):
            sink = sink_ref[0, :, tq * g:tq * (g + 1)]
            m_all = jnp.maximum(m, sink)
            scale = jnp.exp2(m - m_all)
            l = acc[HEAD_DIM:HEAD_DIM + 1, :] * scale + jnp.exp2(sink - m_all)
            outs.append((acc[:HEAD_DIM, :] * scale / l).T)
        o_ref[0] = jnp.concatenate(outs, axis=1).astype(BF16)

    _flash_chunks([(0, CTX_LEN, "ctx"), (ws, span, "win")], [make_scores(g) for g in range(4)],
                  vt_slice, scratch, finish)


def _window_attention(q, k, vt, sink_rows, *, tq):
    b, tb, _ = q.shape
    return pl.pallas_call(
        _window_kernel,
        grid=(b, 2, tb // tq),
        in_specs=[
            pl.BlockSpec((1, tq, 256), lambda bi, h, i: (bi, i, h)),
            pl.BlockSpec((1, 1, tb, HEAD_DIM), lambda bi, h, i: (bi, h, 0, 0)),
            pl.BlockSpec((1, 1, HEAD_DIM + SUM_ROWS, tb), lambda bi, h, i: (bi, h, 0, 0)),
            pl.BlockSpec((1, 1, 4 * tq), lambda bi, h, i: (h, 0, 0)),
        ],
        out_specs=pl.BlockSpec((1, tq, 256), lambda bi, h, i: (bi, i, h)),
        out_shape=jax.ShapeDtypeStruct((b, tb, 512), BF16),
        scratch_shapes=_flash_scratch(tq + 2 * WINDOW, tq, 4),
        compiler_params=_params(("arbitrary", "arbitrary", "arbitrary")),
        name="window_attention",
    )(q, k, vt, sink_rows)


def _flash_scratch(tk, nq, n_streams):
    n = n_streams * FLASH_BUFFERS
    return [pltpu.VMEM((tk, nq), F32)] * n + [pltpu.VMEM((tk, nq), BF16)] * n


def _dense_kernel(q_ref, k_ref, vt_ref, o_ref, *scratch, tk):
    tq = q_ref.shape[1]
    tb = k_ref.shape[2]
    q = q_ref[0]
    q_heads = [q[:, HEAD_DIM * g:HEAD_DIM * (g + 1)] for g in range(4)]
    scores = [lambda lo, size, tag, qh=qh: _nt_dot(k_ref[0, 0, lo:lo + size, :], qh) for qh in q_heads]

    def vt_slice(lo, size):
        return vt_ref[0, 0, :, lo:lo + size]

    def finish(accs, ms):
        outs = [(acc[:HEAD_DIM, :] / acc[HEAD_DIM:HEAD_DIM + 1, :]).T for acc in accs]
        o_ref[0] = jnp.concatenate(outs, axis=1).astype(BF16)

    chunks = _key_chunks(tb, tk)
    is_latent = pl.program_id(2) * tq >= CTX_LEN
    pl.when(is_latent)(lambda: _flash_chunks(chunks, scores, vt_slice, scratch, finish))
    pl.when(jnp.logical_not(is_latent))(lambda: _flash_chunks(chunks[:1], scores, vt_slice, scratch, finish))


def _dense_attention(q, k, vt, *, tq, tk):
    b, tb, _ = q.shape
    assert (tb - CTX_LEN) % tk == 0
    return pl.pallas_call(
        functools.partial(_dense_kernel, tk=tk),
        grid=(b, 2, tb // tq),
        in_specs=[
            pl.BlockSpec((1, tq, 256), lambda bi, h, i: (bi, i, 2 + h)),
            pl.BlockSpec((1, 1, tb, HEAD_DIM), lambda bi, h, i: (bi, 2 + h, 0, 0)),
            pl.BlockSpec((1, 1, HEAD_DIM + SUM_ROWS, tb), lambda bi, h, i: (bi, 2 + h, 0, 0)),
        ],
        out_specs=pl.BlockSpec((1, tq, 256), lambda bi, h, i: (bi, i, h)),
        out_shape=jax.ShapeDtypeStruct((b, tb, 512), BF16),
        scratch_shapes=_flash_scratch(tk, tq, 4),
        compiler_params=_params(("arbitrary", "arbitrary", "arbitrary")),
        name="dense_attention",
    )(q, k, vt)


def _diff_kernel(qa_ref, qb_ref, k_ref, vt_ref, lam_ref, subw_ref, o_ref, *scratch, tk, lam_init):
    tb = k_ref.shape[2]
    qs = [(a, q_ref[0, :, HEAD_DIM * a:HEAD_DIM * (a + 1)]) for a in range(2) for q_ref in (qa_ref, qb_ref)]
    scores = [lambda lo, size, tag, a=a, qh=qh: _nt_dot(k_ref[0, a, lo:lo + size, :], qh) for a, qh in qs]

    def vt_slice(lo, size):
        return vt_ref[0, 0, :, lo:lo + size]

    def finish(accs, ms):
        ots = [acc[:2 * HEAD_DIM, :] / acc[2 * HEAD_DIM:2 * HEAD_DIM + 1, :] for acc in accs]
        ots = [jnp.concatenate(ots[0:2], axis=1), jnp.concatenate(ots[2:4], axis=1)]
        lam = (jnp.exp(jnp.sum(lam_ref[0:1, :] * lam_ref[1:2, :], axis=-1, keepdims=True))
               - jnp.exp(jnp.sum(lam_ref[2:3, :] * lam_ref[3:4, :], axis=-1, keepdims=True)) + lam_init)
        o = ots[0] - lam * ots[1]
        ms = jnp.mean(o * o, axis=0, keepdims=True)
        o = (o * lax.rsqrt(ms + EPS) * subw_ref[...]) * (1.0 - lam_init)
        o_ref[0] = o.T.astype(BF16)

    _flash_chunks(_key_chunks(tb, tk), scores, vt_slice, scratch, finish)


def _diff_attention(q, k, vt, lam_vecs, subw, *, tq, tk, lam_init):
    b, tb, _ = q.shape
    n_lat = tb - CTX_LEN
    n_heads = vt.shape[1]
    d_v = vt.shape[2] - SUM_ROWS
    hq = tq // 2
    assert CTX_LEN % hq == 0
    q_off = CTX_LEN // hq
    return pl.pallas_call(
        functools.partial(_diff_kernel, tk=tk, lam_init=lam_init),
        grid=(b, n_heads, n_lat // tq),
        in_specs=[
            pl.BlockSpec((1, hq, 2 * HEAD_DIM), lambda bi, h, i: (bi, 2 * i + q_off, h)),
            pl.BlockSpec((1, hq, 2 * HEAD_DIM), lambda bi, h, i: (bi, 2 * i + 1 + q_off, h)),
            pl.BlockSpec((1, 2, tb, HEAD_DIM), lambda bi, h, i: (bi, h, 0, 0)),
            pl.BlockSpec((1, 1, d_v + SUM_ROWS, tb), lambda bi, h, i: (bi, h, 0, 0)),
            pl.BlockSpec((4, HEAD_DIM), lambda bi, h, i: (0, 0)),
            pl.BlockSpec((d_v, 1), lambda bi, h, i: (0, 0)),
        ],
        out_specs=pl.BlockSpec((1, tq, d_v), lambda bi, h, i: (bi, i, h)),
        out_shape=jax.ShapeDtypeStruct((b, n_lat, n_heads * d_v), BF16),
        scratch_shapes=_flash_scratch(tk, tq // 2, 4),
        compiler_params=_params(("arbitrary", "arbitrary", "arbitrary")),
        name="diff_attention",
    )(q, q, k, vt, lam_vecs, subw)


def _post_kernel(*refs, n_attn):
    n_x = len(refs) - n_attn - 6
    x_refs, mod_ref = refs[:n_x], refs[n_x]
    o_refs = refs[n_x + 1:n_x + 1 + n_attn]
    wo_ref, nw_ref, wi_ref, wf_ref, out_ref = refs[n_x + 1 + n_attn:]
    mod = mod_ref[0, 0]
    kw = wo_ref.shape[0] // n_attn
    a = _dot(o_refs[0][0], wo_ref[0:kw, :])
    for j in range(1, n_attn):
        a = a + _dot(o_refs[j][0], wo_ref[kw * j:kw * (j + 1), :])
    x1 = _stream_tile(x_refs) + mod[2:3, :] * a
    h = _modulated_norm(x1, nw_ref[...], mod[3:4, :], mod[4:5, :])
    u = _dot(h.astype(BF16), wi_ref[...])
    f = wf_ref.shape[0]
    gate = u[:, :f]
    act = (gate / (1.0 + jnp.exp(-gate))) * u[:, f:]
    y = _dot(act.astype(BF16), wf_ref[...])
    out_ref[0] = x1 + mod[5:6, :] * y


def _post(xs, mod_tab, attn_outs, wo, nw, wi, wf, *, latent_only):
    b, d = xs[0].shape[0], xs[0].shape[2]
    tb = sum(a.shape[1] for a in xs)
    tm = ROW_TILE
    off = CTX_LEN // tm if latent_only else 0
    n_rows = tb - CTX_LEN if latent_only else tb
    n_attn = len(attn_outs)

    def const(shape):
        return pl.BlockSpec(shape, lambda bi, i: (0,) * len(shape))

    in_specs = _stream_specs(xs, tm, off) + [
        pl.BlockSpec((1, 1, MOD_ROWS, d), lambda bi, i: (bi, jnp.minimum(i + off, 1), 0, 0)),
    ]
    for o in attn_outs:
        in_specs.append(pl.BlockSpec((1, tm, o.shape[2]), lambda bi, i: (bi, i, 0)))
    in_specs += [const(wo.shape), const(nw.shape), const(wi.shape), const(wf.shape)]
    return pl.pallas_call(
        functools.partial(_post_kernel, n_attn=n_attn),
        grid=(b, n_rows // tm),
        in_specs=in_specs,
        out_specs=pl.BlockSpec((1, tm, d), lambda bi, i: (bi, i, 0)),
        out_shape=jax.ShapeDtypeStruct((b, n_rows, d), F32),
        compiler_params=_params(("arbitrary", "arbitrary")),
        name="post",
    )(*xs, mod_tab, *attn_outs, wo, nw, wi, wf)


def _deinterleave_perm(n_heads):
    one = jnp.concatenate([jnp.arange(0, HEAD_DIM, 2), jnp.arange(1, HEAD_DIM, 2)])
    return (jnp.arange(n_heads)[:, None] * HEAD_DIM + one[None, :]).reshape(-1)


def _rope_tables(n_lat):
    rows = n_lat // GRID_W
    row = jnp.repeat(jnp.arange(rows, dtype=F32), GRID_W)
    col = jnp.tile(jnp.arange(GRID_W, dtype=F32), rows)
    n_freq = HEAD_DIM // 4
    inv = ROPE_THETA ** (-jnp.arange(n_freq, dtype=F32) / n_freq)
    ang = jnp.concatenate([row[:, None] * inv, col[:, None] * inv], axis=-1)
    cos, sin = jnp.cos(ang), jnp.sin(ang)
    cos = jnp.concatenate([jnp.ones((CTX_LEN, HEAD_DIM // 2), F32), cos], axis=0)
    sin = jnp.concatenate([jnp.zeros((CTX_LEN, HEAD_DIM // 2), F32), sin], axis=0)
    cos_t = jnp.tile(jnp.concatenate([cos, cos], axis=-1), (1, 2))
    sin_t = jnp.tile(jnp.concatenate([-sin, sin], axis=-1), (1, 2))
    return cos_t, sin_t


def _gain_row(parts):
    one = _deinterleave_perm(1)
    return jnp.concatenate([jnp.tile(g[one], n) for g, n in parts])[None, :].astype(F32)


def kernel(x, c, ctx, c_ctx, mod_w, mod_b, norm_mix_w, norm_ffn_w, ev_w_in, ev_w_out, ev_qn_a, ev_kn_a,
           ev_qn_b, ev_kn_b, ev_sink_a, od_w_in, od_w_out, od_qn, od_kn, od_lq1, od_lk1, od_lq2, od_lk2,
           od_subln, ffn_w_in, ffn_w_out):
    b, n_lat, d = x.shape
    assert d == D_MODEL and ctx.shape[1] == CTX_LEN and b < MOD_ROWS and n_lat % 512 == 0
    hd = HEAD_DIM

    cc = jnp.zeros((MOD_ROWS, d), F32).at[:b].set(c).at[b].set(c_ctx)
    mod = _modulation(cc, mod_w, mod_b).reshape(DEPTH, MOD_ROWS, 6, d)
    mod_lat = mod[:, :b]
    mod_ctx = jnp.broadcast_to(mod[:, b:b + 1], mod_lat.shape)
    mod_tab = jnp.stack([mod_ctx, mod_lat], axis=2)
    mod_tab = jnp.pad(mod_tab, ((0, 0), (0, 0), (0, 0), (0, MOD_ROWS - 6), (0, 0)))

    cos_t, sin_t = _rope_tables(n_lat)
    bd = jnp.kron(jnp.eye(256 // hd, dtype=F32), jnp.ones((hd, hd), F32)).astype(BF16)

    w = ev_w_in[0]
    qa, ka, va, qb, kb, vb = jnp.split(w, [512, 640, 768, 1280, 1408], axis=1)
    p8, p2 = _deinterleave_perm(8), _deinterleave_perm(2)
    w0 = jnp.concatenate([qa[:, p8], qb[:, p8], ka[:, p2], kb[:, p2]], axis=1).astype(BF16)
    wvt0 = jnp.concatenate([va, vb], axis=1).T.astype(BF16)
    g0 = _gain_row([(ev_qn_a[0], 8), (ev_qn_b[0], 8), (ev_kn_a[0], 2), (ev_kn_b[0], 2)])
    q0, k0, vt0 = _inproj((ctx, x), mod_tab[0], norm_mix_w[0][None, :], w0, wvt0, g0, cos_t, sin_t, bd,
                          n_k=4, n_v=4, d_v=hd)
    tq_a = 256
    sink_rows = jnp.repeat(ev_sink_a[0].astype(F32) * LOG2E, tq_a).reshape(2, 1, 4 * tq_a)
    o_a = _window_attention(q0, k0, vt0, sink_rows, tq=tq_a)
    o_b = _dense_attention(q0, k0, vt0, tq=256, tk=256)
    xs = _post((ctx, x), mod_tab[0], [o_a, o_b], ev_w_out[0].astype(BF16), norm_ffn_w[0][None, :],
               ffn_w_in[0].astype(BF16), ffn_w_out[0].astype(BF16), latent_only=False)

    w = od_w_in[0]
    p16 = _deinterleave_perm(16)
    w1 = jnp.concatenate([w[:, :1024][:, p16], w[:, 1024:2048][:, p16]], axis=1).astype(BF16)
    wvt1 = w[:, 2048:].T.astype(BF16)
    g1 = _gain_row([(od_qn[0], 16), (od_kn[0], 16)])
    q1, k1, vt1 = _inproj((xs,), mod_tab[1], norm_mix_w[1][None, :], w1, wvt1, g1, cos_t, sin_t, bd,
                          n_k=16, n_v=8, d_v=2 * hd)
    lam_init = 0.8 - 0.6 * math.exp(-0.3 * 1)
    lam_vecs = jnp.stack([od_lq1[0], od_lk1[0], od_lq2[0], od_lk2[0]]).astype(F32)
    o_c = _diff_attention(q1, k1, vt1, lam_vecs, od_subln[0].astype(F32)[:, None],
                          tq=512, tk=256, lam_init=lam_init)
    return _post((xs,), mod_tab[1], [o_c], od_w_out[0].astype(BF16), norm_ffn_w[1][None, :],
                 ffn_w_in[1].astype(BF16), ffn_w_out[1].astype(BF16), latent_only=True)
```

```python
import functools
import math

import jax
import jax.numpy as jnp
from jax import lax
from jax.experimental import pallas as pl
from jax.experimental.pallas import tpu as pltpu

D_MODEL = 1024
HEAD_DIM = 64
CTX_LEN = 256
GRID_W = 64
WINDOW = 128
FFN_HIDDEN = 2816
DEPTH = 2
ROPE_THETA = 10000.0
EPS = 1e-6
NEG_INF = -1e30
LOG2E = 1.4426950408889634
Q_SCALE = HEAD_DIM ** -0.5 * LOG2E
MOD_ROWS = 8
ROW_TILE = 256
FLASH_BUFFERS = 4
SOFTMAX_ROWS = 64
SUM_ROWS = 16
VMEM_LIMIT = 56 * 1024 * 1024

F32 = jnp.float32
BF16 = jnp.bfloat16


def _nt_dot(a, b):
    return lax.dot_general(a, b, (((1,), (1,)), ((), ())), preferred_element_type=F32)


def _dot(a, b):
    return jnp.dot(a, b, preferred_element_type=F32)


def _params(sem):
    return pltpu.CompilerParams(dimension_semantics=sem, vmem_limit_bytes=VMEM_LIMIT)


def _mod_kernel(cc_ref, w_ref, b_ref, o_ref):
    a = cc_ref[...]
    a = a / (1.0 + jnp.exp(-a))
    o_ref[0] = _dot(a.astype(BF16), w_ref[0].astype(BF16)) + b_ref[0]


def _modulation(cc, mod_w, mod_b):
    depth, d, n = mod_w.shape
    tn = 1536
    return pl.pallas_call(
        _mod_kernel,
        grid=(depth, n // tn),
        in_specs=[
            pl.BlockSpec((MOD_ROWS, d), lambda l, j: (0, 0)),
            pl.BlockSpec((1, d, tn), lambda l, j: (l, 0, j)),
            pl.BlockSpec((1, 1, tn), lambda l, j: (l, 0, j)),
        ],
        out_specs=pl.BlockSpec((1, MOD_ROWS, tn), lambda l, j: (l, 0, j)),
        out_shape=jax.ShapeDtypeStruct((depth, MOD_ROWS, n), F32),
        compiler_params=_params(("arbitrary", "arbitrary")),
        name="modulation",
    )(cc, mod_w, mod_b.reshape(depth, 1, n))


def _modulated_norm(x, nw, shift, scale):
    ms = jnp.mean(x * x, axis=-1, keepdims=True)
    return (x * lax.rsqrt(ms + EPS) * nw) * (1.0 + scale) + shift


def _stream_specs(xs, tm, off=0):
    if len(xs) == 1:
        return [pl.BlockSpec((1, tm, xs[0].shape[2]), lambda bi, i: (bi, i + off, 0))]
    assert tm == CTX_LEN and off == 0
    d = xs[0].shape[2]
    return [pl.BlockSpec((1, tm, d), lambda bi, i: (bi, 0, 0)),
            pl.BlockSpec((1, tm, d), lambda bi, i: (bi, jnp.maximum(i - 1, 0), 0))]


def _stream_tile(x_refs):
    if len(x_refs) == 1:
        return x_refs[0][0]
    return jnp.where(pl.program_id(1) == 0, x_refs[0][0], x_refs[1][0])


def _inproj_kernel(*refs, n_q, n_k, n_v, d_v):
    x_refs = refs[:-11]
    mod_ref, nw_ref, w_ref, wvt_ref, g_ref, cos_ref, sin_ref, bd_ref, q_ref, k_ref, vt_ref = refs[-11:]
    tm = q_ref.shape[1]
    h = _modulated_norm(_stream_tile(x_refs), nw_ref[...], mod_ref[0, 0, 0:1, :], mod_ref[0, 0, 1:2, :])
    hb = h.astype(BF16)
    y = _dot(hb, w_ref[...])
    cos = cos_ref[...]
    sin = sin_ref[...]
    lane = lax.broadcasted_iota(jnp.int32, (tm, 128), 1)
    first_half = (lane % HEAD_DIM) < (HEAD_DIM // 2)
    n_norm = n_q + n_k * HEAD_DIM
    for c in range(n_norm // 256):
        yc = y[:, 256 * c:256 * (c + 1)]
        ss = _dot((yc * yc).astype(BF16), bd_ref[...])
        z = yc * lax.rsqrt(ss * (1.0 / HEAD_DIM) + EPS) * g_ref[:, 256 * c:256 * (c + 1)]
        for half in range(2):
            zc = z[:, 128 * half:128 * (half + 1)]
            partner = jnp.where(first_half, pltpu.roll(zc, 96, 1), pltpu.roll(zc, 32, 1))
            o = zc * cos + partner * sin
            col = 256 * c + 128 * half
            if col < n_q:
                q_ref[0, :, col:col + 128] = (o * Q_SCALE).astype(BF16)
            else:
                kh = (col - n_q) // HEAD_DIM
                k_ref[0, kh] = o[:, :HEAD_DIM].astype(BF16)
                k_ref[0, kh + 1] = o[:, HEAD_DIM:].astype(BF16)
    vt = _nt_dot(wvt_ref[...], hb).astype(BF16)
    ones_rows = (lax.broadcasted_iota(jnp.int32, (SUM_ROWS, tm), 0) == 0).astype(BF16)
    for hv in range(n_v):
        vt_ref[0, hv, 0:d_v, :] = vt[d_v * hv:d_v * (hv + 1), :]
        vt_ref[0, hv, d_v:d_v + SUM_ROWS, :] = ones_rows


def _inproj(xs, mod_tab, nw, w, wvt, gains, cos_t, sin_t, bd, *, n_k, n_v, d_v):
    b, d = xs[0].shape[0], xs[0].shape[2]
    tb = sum(a.shape[1] for a in xs)
    n_q = D_MODEL
    n_in = w.shape[1]
    tm = ROW_TILE
    kern = functools.partial(_inproj_kernel, n_q=n_q, n_k=n_k, n_v=n_v, d_v=d_v)
    return pl.pallas_call(
        kern,
        grid=(b, tb // tm),
        in_specs=_stream_specs(xs, tm) + [
            pl.BlockSpec((1, 1, MOD_ROWS, d), lambda bi, i: (bi, jnp.minimum(i, 1), 0, 0)),
            pl.BlockSpec((1, d), lambda bi, i: (0, 0)),
            pl.BlockSpec((d, n_in), lambda bi, i: (0, 0)),
            pl.BlockSpec((n_v * d_v, d), lambda bi, i: (0, 0)),
            pl.BlockSpec((1, gains.shape[1]), lambda bi, i: (0, 0)),
            pl.BlockSpec((tm, 128), lambda bi, i: (i, 0)),
            pl.BlockSpec((tm, 128), lambda bi, i: (i, 0)),
            pl.BlockSpec((256, 256), lambda bi, i: (0, 0)),
        ],
        out_specs=[
            pl.BlockSpec((1, tm, n_q), lambda bi, i: (bi, i, 0)),
            pl.BlockSpec((1, n_k, tm, HEAD_DIM), lambda bi, i: (bi, 0, i, 0)),
            pl.BlockSpec((1, n_v, d_v + SUM_ROWS, tm), lambda bi, i: (bi, 0, 0, i)),
        ],
        out_shape=[
            jax.ShapeDtypeStruct((b, tb, n_q), BF16),
            jax.ShapeDtypeStruct((b, n_k, tb, HEAD_DIM), BF16),
            jax.ShapeDtypeStruct((b, n_v, d_v + SUM_ROWS, tb), BF16),
        ],
        compiler_params=_params(("arbitrary", "arbitrary")),
        name="inproj",
    )(*xs, mod_tab, nw, w, wvt, gains, cos_t, sin_t, bd)


def _softmax_step(s_ref, p_ref, m, chunk_max):
    m_new = chunk_max if m is None else jnp.maximum(m, chunk_max)
    for r in range(0, s_ref.shape[0], SOFTMAX_ROWS):
        p_ref[r:r + SOFTMAX_ROWS, :] = jnp.exp2(s_ref[r:r + SOFTMAX_ROWS, :] - m_new).astype(BF16)
    return m_new, (None if m is None else jnp.exp2(m - m_new))


def _flash_chunks(chunks, scores, vt_slice, scratch, finish):
    n_streams = len(scores)
    nbuf = n_streams * FLASH_BUFFERS
    s_refs, p_refs = scratch[:nbuf], scratch[nbuf:]

    def buf(refs, j, c):
        return refs[j * FLASH_BUFFERS + c % FLASH_BUFFERS].at[0:chunks[c][1], :]

    def write_scores(j, c):
        s = scores[j](*chunks[c])
        buf(s_refs, j, c)[...] = s
        return jnp.max(s, axis=0, keepdims=True)

    m, acc = [None] * n_streams, [None] * n_streams
    cmax = {(j, c): write_scores(j, c) for c in range(min(2, len(chunks))) for j in range(n_streams)}
    for c, (lo, size, _) in enumerate(chunks):
        for j in range(n_streams):
            if c + 2 < len(chunks):
                cmax[j, c + 2] = write_scores(j, c + 2)
            p_ref = buf(p_refs, j, c)
            m[j], alpha = _softmax_step(buf(s_refs, j, c), p_ref, m[j], cmax.pop((j, c)))
            pv = _dot(vt_slice(lo, size), p_ref[...])
            acc[j] = pv if alpha is None else alpha * acc[j] + pv
    finish(acc, m)


def _key_chunks(n_keys, tk):
    return [(0, CTX_LEN, "ctx")] + [(lo, tk, "lat") for lo in range(CTX_LEN, n_keys, tk)]


def _window_kernel(q_ref, k_ref, vt_ref, sink_ref, o_ref, *scratch):
    tq = q_ref.shape[1]
    tb = k_ref.shape[2]
    span = tq + 2 * WINDOW
    t = pl.program_id(2)
    q = q_ref[0]
    ws = pl.multiple_of(jnp.clip(tq * t - WINDOW, 0, tb - span), 128)
    kpos = ws - CTX_LEN + lax.broadcasted_iota(jnp.int32, (span, tq), 0)
    qpos = tq * t - CTX_LEN + lax.broadcasted_iota(jnp.int32, (span, tq), 1)
    ok = (jnp.abs(qpos - kpos) <= WINDOW) & (kpos >= 0) & (qpos >= 0)

    def make_scores(g):
        qh = q[:, HEAD_DIM * g:HEAD_DIM * (g + 1)]

        def score(lo, size, tag):
            s = _nt_dot(k_ref[0, 0, pl.ds(lo, size), :], qh)
            return jnp.where(ok, s, NEG_INF) if tag == "win" else s
        return score

    def vt_slice(lo, size):
        return vt_ref[0, 0, :, pl.ds(lo, size)]

    def finish(accs, ms):
        outs = []
        for g, (acc, m) in enumerate(zip(accs, ms)):
            sink = sink_ref[0, :, tq * g:tq * (g + 1)]
            m_all = jnp.maximum(m, sink)
            scale = jnp.exp2(m - m_all)
            l = acc[HEAD_DIM:HEAD_DIM + 1, :] * scale + jnp.exp2(sink - m_all)
            outs.append((acc[:HEAD_DIM, :] * scale / l).T)
        o_ref[0] = jnp.concatenate(outs, axis=1).astype(BF16)

    _flash_chunks([(0, CTX_LEN, "ctx"), (ws, span, "win")], [make_scores(g) for g in range(4)],
                  vt_slice, scratch, finish)


def _window_attention(q, k, vt, sink_rows, *, tq):
    b, tb, _ = q.shape
    return pl.pallas_call(
        _window_kernel,
        grid=(b, 2, tb // tq),
        in_specs=[
            pl.BlockSpec((1, tq, 256), lambda bi, h, i: (bi, i, h)),
            pl.BlockSpec((1, 1, tb, HEAD_DIM), lambda bi, h, i: (bi, h, 0, 0)),
            pl.BlockSpec((1, 1, HEAD_DIM + SUM_ROWS, tb), lambda bi, h, i: (bi, h, 0, 0)),
            pl.BlockSpec((1, 1, 4 * tq), lambda bi, h, i: (h, 0, 0)),
        ],
        out_specs=pl.BlockSpec((1, tq, 256), lambda bi, h, i: (bi, i, h)),
        out_shape=jax.ShapeDtypeStruct((b, tb, 512), BF16),
        scratch_shapes=_flash_scratch(tq + 2 * WINDOW, tq, 4),
        compiler_params=_params(("arbitrary", "arbitrary", "arbitrary")),
        name="window_attention",
    )(q, k, vt, sink_rows)


def _flash_scratch(tk, nq, n_streams):
    n = n_streams * FLASH_BUFFERS
    return [pltpu.VMEM((tk, nq), F32)] * n + [pltpu.VMEM((tk, nq), BF16)] * n


def _dense_kernel(q_ref, k_ref, vt_ref, o_ref, *scratch, tk):
    tq = q_ref.shape[1]
    tb = k_ref.shape[2]
    q = q_ref[0]
    q_heads = [q[:, HEAD_DIM * g:HEAD_DIM * (g + 1)].astype(F32).T.astype(BF16) for g in range(4)]
    scores = [lambda lo, size, tag, qt=qt: _dot(k_ref[0, 0, lo:lo + size, :], qt) for qt in q_heads]

    def vt_slice(lo, size):
        return vt_ref[0, 0, :, lo:lo + size]

    def finish(accs, ms):
        outs = [(acc[:HEAD_DIM, :] / acc[HEAD_DIM:HEAD_DIM + 1, :]).T for acc in accs]
        o_ref[0] = jnp.concatenate(outs, axis=1).astype(BF16)

    chunks = _key_chunks(tb, tk)
    is_latent = pl.program_id(2) * tq >= CTX_LEN
    pl.when(is_latent)(lambda: _flash_chunks(chunks, scores, vt_slice, scratch, finish))
    pl.when(jnp.logical_not(is_latent))(lambda: _flash_chunks(chunks[:1], scores, vt_slice, scratch, finish))


def _dense_attention(q, k, vt, *, tq, tk):
    b, tb, _ = q.shape
    assert (tb - CTX_LEN) % tk == 0
    return pl.pallas_call(
        functools.partial(_dense_kernel, tk=tk),
        grid=(b, 2, tb // tq),
        in_specs=[
            pl.BlockSpec((1, tq, 256), lambda bi, h, i: (bi, i, 2 + h)),
            pl.BlockSpec((1, 1, tb, HEAD_DIM), lambda bi, h, i: (bi, 2 + h, 0, 0)),
            pl.BlockSpec((1, 1, HEAD_DIM + SUM_ROWS, tb), lambda bi, h, i: (bi, 2 + h, 0, 0)),
        ],
        out_specs=pl.BlockSpec((1, tq, 256), lambda bi, h, i: (bi, i, h)),
        out_shape=jax.ShapeDtypeStruct((b, tb, 512), BF16),
        scratch_shapes=_flash_scratch(tk, tq, 4),
        compiler_params=_params(("arbitrary", "arbitrary", "arbitrary")),
        name="dense_attention",
    )(q, k, vt)


def _diff_kernel(qa_ref, qb_ref, k_ref, vt_ref, lam_ref, subw_ref, o_ref, *scratch, tk, lam_init):
    tb = k_ref.shape[2]
    qs = [(a, q_ref[0, :, HEAD_DIM * a:HEAD_DIM * (a + 1)]) for a in range(2) for q_ref in (qa_ref, qb_ref)]
    scores = [lambda lo, size, tag, a=a, qh=qh: _nt_dot(k_ref[0, a, lo:lo + size, :], qh) for a, qh in qs]

    def vt_slice(lo, size):
        return vt_ref[0, 0, :, lo:lo + size]

    def finish(accs, ms):
        ots = [acc[:2 * HEAD_DIM, :] / acc[2 * HEAD_DIM:2 * HEAD_DIM + 1, :] for acc in accs]
        ots = [jnp.concatenate(ots[0:2], axis=1), jnp.concatenate(ots[2:4], axis=1)]
        lam = (jnp.exp(jnp.sum(lam_ref[0:1, :] * lam_ref[1:2, :], axis=-1, keepdims=True))
               - jnp.exp(jnp.sum(lam_ref[2:3, :] * lam_ref[3:4, :], axis=-1, keepdims=True)) + lam_init)
        o = ots[0] - lam * ots[1]
        ms = jnp.mean(o * o, axis=0, keepdims=True)
        o = (o * lax.rsqrt(ms + EPS) * subw_ref[...]) * (1.0 - lam_init)
        o_ref[0] = o.T.astype(BF16)

    _flash_chunks(_key_chunks(tb, tk), scores, vt_slice, scratch, finish)


def _diff_attention(q, k, vt, lam_vecs, subw, *, tq, tk, lam_init):
    b, tb, _ = q.shape
    n_lat = tb - CTX_LEN
    n_heads = vt.shape[1]
    d_v = vt.shape[2] - SUM_ROWS
    hq = tq // 2
    assert CTX_LEN % hq == 0
    q_off = CTX_LEN // hq
    return pl.pallas_call(
        functools.partial(_diff_kernel, tk=tk, lam_init=lam_init),
        grid=(b, n_heads, n_lat // tq),
        in_specs=[
            pl.BlockSpec((1, hq, 2 * HEAD_DIM), lambda bi, h, i: (bi, 2 * i + q_off, h)),
            pl.BlockSpec((1, hq, 2 * HEAD_DIM), lambda bi, h, i: (bi, 2 * i + 1 + q_off, h)),
            pl.BlockSpec((1, 2, tb, HEAD_DIM), lambda bi, h, i: (bi, h, 0, 0)),
            pl.BlockSpec((1, 1, d_v + SUM_ROWS, tb), lambda bi, h, i: (bi, h, 0, 0)),
            pl.BlockSpec((4, HEAD_DIM), lambda bi, h, i: (0, 0)),
            pl.BlockSpec((d_v, 1), lambda bi, h, i: (0, 0)),
        ],
        out_specs=pl.BlockSpec((1, tq, d_v), lambda bi, h, i: (bi, i, h)),
        out_shape=jax.ShapeDtypeStruct((b, n_lat, n_heads * d_v), BF16),
        scratch_shapes=_flash_scratch(tk, tq // 2, 4),
        compiler_params=_params(("arbitrary", "arbitrary", "arbitrary")),
        name="diff_attention",
    )(q, q, k, vt, lam_vecs, subw)


def _post_kernel(*refs, n_attn):
    n_x = len(refs) - n_attn - 6
    x_refs, mod_ref = refs[:n_x], refs[n_x]
    o_refs = refs[n_x + 1:n_x + 1 + n_attn]
    wo_ref, nw_ref, wi_ref, wf_ref, out_ref = refs[n_x + 1 + n_attn:]
    mod = mod_ref[0, 0]
    kw = wo_ref.shape[0] // n_attn
    a = _dot(o_refs[0][0], wo_ref[0:kw, :])
    for j in range(1, n_attn):
        a = a + _dot(o_refs[j][0], wo_ref[kw * j:kw * (j + 1), :])
    x1 = _stream_tile(x_refs) + mod[2:3, :] * a
    h = _modulated_norm(x1, nw_ref[...], mod[3:4, :], mod[4:5, :])
    u = _dot(h.astype(BF16), wi_ref[...])
    f = wf_ref.shape[0]
    gate = u[:, :f]
    act = (gate / (1.0 + jnp.exp(-gate))) * u[:, f:]
    y = _dot(act.astype(BF16), wf_ref[...])
    out_ref[0] = x1 + mod[5:6, :] * y


def _post(xs, mod_tab, attn_outs, wo, nw, wi, wf, *, latent_only):
    b, d = xs[0].shape[0], xs[0].shape[2]
    tb = sum(a.shape[1] for a in xs)
    tm = ROW_TILE
    off = CTX_LEN // tm if latent_only else 0
    n_rows = tb - CTX_LEN if latent_only else tb
    n_attn = len(attn_outs)

    def const(shape):
        return pl.BlockSpec(shape, lambda bi, i: (0,) * len(shape))

    in_specs = _stream_specs(xs, tm, off) + [
        pl.BlockSpec((1, 1, MOD_ROWS, d), lambda bi, i: (bi, jnp.minimum(i + off, 1), 0, 0)),
    ]
    for o in attn_outs:
        in_specs.append(pl.BlockSpec((1, tm, o.shape[2]), lambda bi, i: (bi, i, 0)))
    in_specs += [const(wo.shape), const(nw.shape), const(wi.shape), const(wf.shape)]
    return pl.pallas_call(
        functools.partial(_post_kernel, n_attn=n_attn),
        grid=(b, n_rows // tm),
        in_specs=in_specs,
        out_specs=pl.BlockSpec((1, tm, d), lambda bi, i: (bi, i, 0)),
        out_shape=jax.ShapeDtypeStruct((b, n_rows, d), F32),
        compiler_params=_params(("arbitrary", "arbitrary")),
        name="post",
    )(*xs, mod_tab, *attn_outs, wo, nw, wi, wf)


def _deinterleave_perm(n_heads):
    one = jnp.concatenate([jnp.arange(0, HEAD_DIM, 2), jnp.arange(1, HEAD_DIM, 2)])
    return (jnp.arange(n_heads)[:, None] * HEAD_DIM + one[None, :]).reshape(-1)


def _rope_tables(n_lat):
    rows = n_lat // GRID_W
    row = jnp.repeat(jnp.arange(rows, dtype=F32), GRID_W)
    col = jnp.tile(jnp.arange(GRID_W, dtype=F32), rows)
    n_freq = HEAD_DIM // 4
    inv = ROPE_THETA ** (-jnp.arange(n_freq, dtype=F32) / n_freq)
    ang = jnp.concatenate([row[:, None] * inv, col[:, None] * inv], axis=-1)
    cos, sin = jnp.cos(ang), jnp.sin(ang)
    cos = jnp.concatenate([jnp.ones((CTX_LEN, HEAD_DIM // 2), F32), cos], axis=0)
    sin = jnp.concatenate([jnp.zeros((CTX_LEN, HEAD_DIM // 2), F32), sin], axis=0)
    cos_t = jnp.tile(jnp.concatenate([cos, cos], axis=-1), (1, 2))
    sin_t = jnp.tile(jnp.concatenate([-sin, sin], axis=-1), (1, 2))
    return cos_t, sin_t


def _gain_row(parts):
    one = _deinterleave_perm(1)
    return jnp.concatenate([jnp.tile(g[one], n) for g, n in parts])[None, :].astype(F32)


def kernel(x, c, ctx, c_ctx, mod_w, mod_b, norm_mix_w, norm_ffn_w, ev_w_in, ev_w_out, ev_qn_a, ev_kn_a,
           ev_qn_b, ev_kn_b, ev_sink_a, od_w_in, od_w_out, od_qn, od_kn, od_lq1, od_lk1, od_lq2, od_lk2,
           od_subln, ffn_w_in, ffn_w_out):
    b, n_lat, d = x.shape
    assert d == D_MODEL and ctx.shape[1] == CTX_LEN and b < MOD_ROWS and n_lat % 512 == 0
    hd = HEAD_DIM

    cc = jnp.zeros((MOD_ROWS, d), F32).at[:b].set(c).at[b].set(c_ctx)
    mod = _modulation(cc, mod_w, mod_b).reshape(DEPTH, MOD_ROWS, 6, d)
    mod_lat = mod[:, :b]
    mod_ctx = jnp.broadcast_to(mod[:, b:b + 1], mod_lat.shape)
    mod_tab = jnp.stack([mod_ctx, mod_lat], axis=2)
    mod_tab = jnp.pad(mod_tab, ((0, 0), (0, 0), (0, 0), (0, MOD_ROWS - 6), (0, 0)))

    cos_t, sin_t = _rope_tables(n_lat)
    bd = jnp.kron(jnp.eye(256 // hd, dtype=F32), jnp.ones((hd, hd), F32)).astype(BF16)

    w = ev_w_in[0]
    qa, ka, va, qb, kb, vb = jnp.split(w, [512, 640, 768, 1280, 1408], axis=1)
    p8, p2 = _deinterleave_perm(8), _deinterleave_perm(2)
    w0 = jnp.concatenate([qa[:, p8], qb[:, p8], ka[:, p2], kb[:, p2]], axis=1).astype(BF16)
    wvt0 = jnp.concatenate([va, vb], axis=1).T.astype(BF16)
    g0 = _gain_row([(ev_qn_a[0], 8), (ev_qn_b[0], 8), (ev_kn_a[0], 2), (ev_kn_b[0], 2)])
    q0, k0, vt0 = _inproj((ctx, x), mod_tab[0], norm_mix_w[0][None, :], w0, wvt0, g0, cos_t, sin_t, bd,
                          n_k=4, n_v=4, d_v=hd)
    tq_a = 256
    sink_rows = jnp.repeat(ev_sink_a[0].astype(F32) * LOG2E, tq_a).reshape(2, 1, 4 * tq_a)
    o_a = _window_attention(q0, k0, vt0, sink_rows, tq=tq_a)
    o_b = _dense_attention(q0, k0, vt0, tq=256, tk=256)
    xs = _post((ctx, x), mod_tab[0], [o_a, o_b], ev_w_out[0].astype(BF16), norm_ffn_w[0][None, :],
               ffn_w_in[0].astype(BF16), ffn_w_out[0].astype(BF16), latent_only=False)

    w = od_w_in[0]
    p16 = _deinterleave_perm(16)
    w1 = jnp.concatenate([w[:, :1024][:, p16], w[:, 1024:2048][:, p16]], axis=1).astype(BF16)
    wvt1 = w[:, 2048:].T.astype(BF16)
    g1 = _gain_row([(od_qn[0], 16), (od_kn[0], 16)])
    q1, k1, vt1 = _inproj((xs,), mod_tab[1], norm_mix_w[1][None, :], w1, wvt1, g1, cos_t, sin_t, bd,
                          n_k=16, n_v=8, d_v=2 * hd)
    lam_init = 0.8 - 0.6 * math.exp(-0.3 * 1)
    lam_vecs = jnp.stack([od_lq1[0], od_lk1[0], od_lq2[0], od_lk2[0]]).astype(F32)
    o_c = _diff_attention(q1, k1, vt1, lam_vecs, od_subln[0].astype(F32)[:, None],
                          tq=512, tk=256, lam_init=lam_init)
    return _post((xs,), mod_tab[1], [o_c], od_w_out[0].astype(BF16), norm_ffn_w[1][None, :],
                 ffn_w_in[1].astype(BF16), ffn_w_out[1].astype(BF16), latent_only=True)
```

```python
import functools
import math

import jax
import jax.numpy as jnp
from jax import lax
from jax.experimental import pallas as pl
from jax.experimental.pallas import tpu as pltpu

D_MODEL = 1024
HEAD_DIM = 64
CTX_LEN = 256
GRID_W = 64
WINDOW = 128
FFN_HIDDEN = 2816
DEPTH = 2
ROPE_THETA = 10000.0
EPS = 1e-6
NEG_INF = -1e30
LOG2E = 1.4426950408889634
Q_SCALE = HEAD_DIM ** -0.5 * LOG2E
MOD_ROWS = 8
ROW_TILE = 256
MOD_COL_TILE = 1536
KEY_CHUNK = 256
QUERY_TILE = 256
DIFF_QUERY_TILES = 2
FLASH_BUFFERS = 4
WINDOW_BUFFERS = 2
SOFTMAX_ROWS = 64
SUM_ROWS = 16
VMEM_LIMIT = 56 * 1024 * 1024

F32 = jnp.float32
BF16 = jnp.bfloat16


def _nt_dot(a, b):
    return lax.dot_general(a, b, (((1,), (1,)), ((), ())), preferred_element_type=F32)


def _dot(a, b):
    return jnp.dot(a, b, preferred_element_type=F32)


def _params(sem):
    return pltpu.CompilerParams(dimension_semantics=sem, vmem_limit_bytes=VMEM_LIMIT)


def _mod_kernel(cc_ref, w_ref, b_ref, o_ref):
    a = cc_ref[...]
    a = a / (1.0 + jnp.exp(-a))
    o_ref[0] = _dot(a.astype(BF16), w_ref[0].astype(BF16)) + b_ref[0]


def _modulation(cc, mod_w, mod_b):
    depth, d, n = mod_w.shape
    tn = MOD_COL_TILE
    return pl.pallas_call(
        _mod_kernel,
        grid=(depth, n // tn),
        in_specs=[
            pl.BlockSpec((MOD_ROWS, d), lambda l, j: (0, 0)),
            pl.BlockSpec((1, d, tn), lambda l, j: (l, 0, j)),
            pl.BlockSpec((1, 1, tn), lambda l, j: (l, 0, j)),
        ],
        out_specs=pl.BlockSpec((1, MOD_ROWS, tn), lambda l, j: (l, 0, j)),
        out_shape=jax.ShapeDtypeStruct((depth, MOD_ROWS, n), F32),
        compiler_params=_params(("arbitrary", "arbitrary")),
        name="modulation",
    )(cc, mod_w, mod_b.reshape(depth, 1, n))


def _modulated_norm(x, nw, shift, scale):
    ms = jnp.mean(x * x, axis=-1, keepdims=True)
    return (x * lax.rsqrt(ms + EPS) * nw) * (1.0 + scale) + shift


def _stream_specs(xs, tm, off=0):
    if len(xs) == 1:
        return [pl.BlockSpec((1, tm, xs[0].shape[2]), lambda bi, i: (bi, i + off, 0))]
    assert tm == CTX_LEN and off == 0
    d = xs[0].shape[2]
    return [pl.BlockSpec((1, tm, d), lambda bi, i: (bi, 0, 0)),
            pl.BlockSpec((1, tm, d), lambda bi, i: (bi, jnp.maximum(i - 1, 0), 0))]


def _stream_tile(x_refs):
    if len(x_refs) == 1:
        return x_refs[0][0]
    return jnp.where(pl.program_id(1) == 0, x_refs[0][0], x_refs[1][0])


def _inproj_kernel(*refs, n_q, n_k, n_v, d_v):
    x_refs = refs[:-11]
    mod_ref, nw_ref, w_ref, wvt_ref, g_ref, cos_ref, sin_ref, bd_ref, q_ref, k_ref, vt_ref = refs[-11:]
    tm = q_ref.shape[1]
    h = _modulated_norm(_stream_tile(x_refs), nw_ref[...], mod_ref[0, 0, 0:1, :], mod_ref[0, 0, 1:2, :])
    hb = h.astype(BF16)
    y = _dot(hb, w_ref[...])
    cos = cos_ref[...]
    sin = sin_ref[...]
    lane = lax.broadcasted_iota(jnp.int32, (tm, 128), 1)
    first_half = (lane % HEAD_DIM) < (HEAD_DIM // 2)
    n_norm = n_q + n_k * HEAD_DIM
    for c in range(n_norm // 256):
        yc = y[:, 256 * c:256 * (c + 1)]
        ss = _dot((yc * yc).astype(BF16), bd_ref[...])
        z = yc * lax.rsqrt(ss * (1.0 / HEAD_DIM) + EPS) * g_ref[:, 256 * c:256 * (c + 1)]
        for half in range(2):
            zc = z[:, 128 * half:128 * (half + 1)]
            partner = jnp.where(first_half, pltpu.roll(zc, 96, 1), pltpu.roll(zc, 32, 1))
            o = zc * cos + partner * sin
            col = 256 * c + 128 * half
            if col < n_q:
                q_ref[0, :, col:col + 128] = (o * Q_SCALE).astype(BF16)
            else:
                kh = (col - n_q) // HEAD_DIM
                k_ref[0, kh] = o[:, :HEAD_DIM].astype(BF16)
                k_ref[0, kh + 1] = o[:, HEAD_DIM:].astype(BF16)
    vt = _nt_dot(wvt_ref[...], hb).astype(BF16)
    ones_rows = (lax.broadcasted_iota(jnp.int32, (SUM_ROWS, tm), 0) == 0).astype(BF16)
    for hv in range(n_v):
        vt_ref[0, hv, 0:d_v, :] = vt[d_v * hv:d_v * (hv + 1), :]
        vt_ref[0, hv, d_v:d_v + SUM_ROWS, :] = ones_rows


def _inproj(xs, mod_tab, nw, w, wvt, gains, cos_t, sin_t, bd, *, n_k, n_v, d_v):
    b, d = xs[0].shape[0], xs[0].shape[2]
    tb = sum(a.shape[1] for a in xs)
    n_q = D_MODEL
    n_in = w.shape[1]
    tm = ROW_TILE
    kern = functools.partial(_inproj_kernel, n_q=n_q, n_k=n_k, n_v=n_v, d_v=d_v)
    return pl.pallas_call(
        kern,
        grid=(b, tb // tm),
        in_specs=_stream_specs(xs, tm) + [
            pl.BlockSpec((1, 1, MOD_ROWS, d), lambda bi, i: (bi, jnp.minimum(i, 1), 0, 0)),
            pl.BlockSpec((1, d), lambda bi, i: (0, 0)),
            pl.BlockSpec((d, n_in), lambda bi, i: (0, 0)),
            pl.BlockSpec((n_v * d_v, d), lambda bi, i: (0, 0)),
            pl.BlockSpec((1, gains.shape[1]), lambda bi, i: (0, 0)),
            pl.BlockSpec((tm, 128), lambda bi, i: (i, 0)),
            pl.BlockSpec((tm, 128), lambda bi, i: (i, 0)),
            pl.BlockSpec((256, 256), lambda bi, i: (0, 0)),
        ],
        out_specs=[
            pl.BlockSpec((1, tm, n_q), lambda bi, i: (bi, i, 0)),
            pl.BlockSpec((1, n_k, tm, HEAD_DIM), lambda bi, i: (bi, 0, i, 0)),
            pl.BlockSpec((1, n_v, d_v + SUM_ROWS, tm), lambda bi, i: (bi, 0, 0, i)),
        ],
        out_shape=[
            jax.ShapeDtypeStruct((b, tb, n_q), BF16),
            jax.ShapeDtypeStruct((b, n_k, tb, HEAD_DIM), BF16),
            jax.ShapeDtypeStruct((b, n_v, d_v + SUM_ROWS, tb), BF16),
        ],
        compiler_params=_params(("arbitrary", "arbitrary")),
        name="inproj",
    )(*xs, mod_tab, nw, w, wvt, gains, cos_t, sin_t, bd)


def _softmax_step(s_ref, p_ref, m, chunk_max):
    m_new = chunk_max if m is None else jnp.maximum(m, chunk_max)
    for r in range(0, s_ref.shape[0], SOFTMAX_ROWS):
        p_ref[r:r + SOFTMAX_ROWS, :] = jnp.exp2(s_ref[r:r + SOFTMAX_ROWS, :] - m_new).astype(BF16)
    return m_new, (None if m is None else jnp.exp2(m - m_new))


class _Stream:
    def __init__(self, chunks, score, vt_slice, s_bufs, p_bufs):
        self.chunks, self.score, self.vt_slice = chunks, score, vt_slice
        self.s_bufs, self.p_bufs = s_bufs, p_bufs
        self.m = self.acc = None
        self.cmax = {}

    def buf(self, refs, c):
        return refs[c % len(refs)].at[0:self.chunks[c][1], :]

    def write_scores(self, c):
        if c < len(self.chunks):
            s = self.score(*self.chunks[c])
            self.buf(self.s_bufs, c)[...] = s
            self.cmax[c] = jnp.max(s, axis=0, keepdims=True)

    def step(self, c):
        if c >= len(self.chunks):
            return
        self.write_scores(c + 2)
        lo, size, _ = self.chunks[c]
        p_ref = self.buf(self.p_bufs, c)
        self.m, alpha = _softmax_step(self.buf(self.s_bufs, c), p_ref, self.m, self.cmax.pop(c))
        pv = _dot(self.vt_slice(lo, size), p_ref[...])
        self.acc = pv if alpha is None else alpha * self.acc + pv


def _run_streams(streams):
    for c in range(2):
        for st in streams:
            st.write_scores(c)
    for c in range(max(len(st.chunks) for st in streams)):
        for st in streams:
            st.step(c)


def _key_chunks(n_keys, tk):
    return [(0, CTX_LEN, "ctx")] + [(lo, tk, "lat") for lo in range(CTX_LEN, n_keys, tk)]


def _stream_scratch(n_keys, nq, n_streams, n_bufs):
    n = n_streams * n_bufs
    return [pltpu.VMEM((n_keys, nq), F32)] * n + [pltpu.VMEM((n_keys, nq), BF16)] * n


def _split_scratch(scratch, n_streams, n_bufs):
    n = n_streams * n_bufs
    return [(scratch[j * n_bufs:(j + 1) * n_bufs], scratch[n + j * n_bufs:n + (j + 1) * n_bufs])
            for j in range(n_streams)]


def _layer0_attention_kernel(qa_ref, qb_ref, ka_ref, vta_ref, kb_ref, vtb_ref, sink_ref, oa_ref, ob_ref,
                             *scratch, tk):
    tq = qa_ref.shape[1]
    tb = ka_ref.shape[2]
    span = tq + 2 * WINDOW
    t = pl.program_id(2)
    n_dense = 2 * 4 * FLASH_BUFFERS
    dense_bufs = _split_scratch(scratch[:n_dense], 4, FLASH_BUFFERS)
    window_bufs = _split_scratch(scratch[n_dense:], 4, WINDOW_BUFFERS)

    qa = qa_ref[0]
    ws = pl.multiple_of(jnp.clip(tq * t - WINDOW, 0, tb - span), 128)
    kpos = ws - CTX_LEN + lax.broadcasted_iota(jnp.int32, (span, tq), 0)
    qpos = tq * t - CTX_LEN + lax.broadcasted_iota(jnp.int32, (span, tq), 1)
    ok = (jnp.abs(qpos - kpos) <= WINDOW) & (kpos >= 0) & (qpos >= 0)

    def window_score(g):
        qh = qa[:, HEAD_DIM * g:HEAD_DIM * (g + 1)]

        def score(lo, size, tag):
            s = _nt_dot(ka_ref[0, 0, pl.ds(lo, size), :], qh)
            return jnp.where(ok, s, NEG_INF) if tag == "win" else s
        return score

    def window_streams():
        return [_Stream([(0, CTX_LEN, "ctx"), (ws, span, "win")], window_score(g),
                        lambda lo, size: vta_ref[0, 0, :, pl.ds(lo, size)], *window_bufs[g])
                for g in range(4)]

    qb = qb_ref[0]
    qts = [qb[:, HEAD_DIM * g:HEAD_DIM * (g + 1)].astype(F32).T.astype(BF16) for g in range(4)]

    def dense_streams(chunks):
        return [_Stream(chunks, lambda lo, size, tag, qt=qt: _dot(kb_ref[0, 0, lo:lo + size, :], qt),
                        lambda lo, size: vtb_ref[0, 0, :, lo:lo + size], *dense_bufs[g])
                for g, qt in enumerate(qts)]

    def run(chunks):
        win, dense = window_streams(), dense_streams(chunks)
        _run_streams(win + dense)
        outs = []
        for g, st in enumerate(win):
            sink = sink_ref[0, :, tq * g:tq * (g + 1)]
            m_all = jnp.maximum(st.m, sink)
            scale = jnp.exp2(st.m - m_all)
            l = st.acc[HEAD_DIM:HEAD_DIM + 1, :] * scale + jnp.exp2(sink - m_all)
            outs.append((st.acc[:HEAD_DIM, :] * scale / l).T)
        oa_ref[0] = jnp.concatenate(outs, axis=1).astype(BF16)
        outs = [(st.acc[:HEAD_DIM, :] / st.acc[HEAD_DIM:HEAD_DIM + 1, :]).T for st in dense]
        ob_ref[0] = jnp.concatenate(outs, axis=1).astype(BF16)

    chunks = _key_chunks(tb, tk)
    is_latent = t * tq >= CTX_LEN
    pl.when(is_latent)(lambda: run(chunks))
    pl.when(jnp.logical_not(is_latent))(lambda: run(chunks[:1]))


def _layer0_attention(q, k, vt, sink_rows, *, tq, tk):
    b, tb, _ = q.shape
    assert (tb - CTX_LEN) % tk == 0

    def kv_specs(first_head):
        return [pl.BlockSpec((1, 1, tb, HEAD_DIM), lambda bi, h, i: (bi, first_head + h, 0, 0)),
                pl.BlockSpec((1, 1, HEAD_DIM + SUM_ROWS, tb), lambda bi, h, i: (bi, first_head + h, 0, 0))]

    out = jax.ShapeDtypeStruct((b, tb, 512), BF16)
    return pl.pallas_call(
        functools.partial(_layer0_attention_kernel, tk=tk),
        grid=(b, 2, tb // tq),
        in_specs=[pl.BlockSpec((1, tq, 256), lambda bi, h, i: (bi, i, h)),
                  pl.BlockSpec((1, tq, 256), lambda bi, h, i: (bi, i, 2 + h))]
        + kv_specs(0) + kv_specs(2) + [pl.BlockSpec((1, 1, 4 * tq), lambda bi, h, i: (h, 0, 0))],
        out_specs=[pl.BlockSpec((1, tq, 256), lambda bi, h, i: (bi, i, h))] * 2,
        out_shape=[out, out],
        scratch_shapes=(_stream_scratch(tk, tq, 4, FLASH_BUFFERS)
                        + _stream_scratch(tq + 2 * WINDOW, tq, 4, WINDOW_BUFFERS)),
        compiler_params=_params(("arbitrary", "arbitrary", "arbitrary")),
        name="layer0_attention",
    )(q, q, k, vt, k, vt, sink_rows)


def _diff_kernel(qa_ref, qb_ref, k_ref, vt_ref, lam_ref, subw_ref, o_ref, *scratch, tk, lam_init):
    tb = k_ref.shape[2]
    d_v = 2 * HEAD_DIM
    chunks = _key_chunks(tb, tk)
    bufs = _split_scratch(scratch, 4, FLASH_BUFFERS)
    streams = []
    for a in range(2):
        for q_ref in (qa_ref, qb_ref):
            qh = q_ref[0, :, HEAD_DIM * a:HEAD_DIM * (a + 1)]
            streams.append(_Stream(chunks, lambda lo, size, tag, a=a, qh=qh: _nt_dot(k_ref[0, a, lo:lo + size, :], qh),
                                   lambda lo, size: vt_ref[0, 0, :, lo:lo + size], *bufs[len(streams)]))
    _run_streams(streams)
    ots = [st.acc[:d_v, :] / st.acc[d_v:d_v + 1, :] for st in streams]
    ots = [jnp.concatenate(ots[0:2], axis=1), jnp.concatenate(ots[2:4], axis=1)]
    lam = (jnp.exp(jnp.sum(lam_ref[0:1, :] * lam_ref[1:2, :], axis=-1, keepdims=True))
           - jnp.exp(jnp.sum(lam_ref[2:3, :] * lam_ref[3:4, :], axis=-1, keepdims=True)) + lam_init)
    o = ots[0] - lam * ots[1]
    ms = jnp.mean(o * o, axis=0, keepdims=True)
    o = (o * lax.rsqrt(ms + EPS) * subw_ref[...]) * (1.0 - lam_init)
    o_ref[0] = o.T.astype(BF16)


def _diff_attention(q, k, vt, lam_vecs, subw, *, tq, tk, lam_init):
    b, tb, _ = q.shape
    n_lat = tb - CTX_LEN
    n_heads = vt.shape[1]
    d_v = vt.shape[2] - SUM_ROWS
    hq = tq // 2
    assert CTX_LEN % hq == 0
    q_off = CTX_LEN // hq
    return pl.pallas_call(
        functools.partial(_diff_kernel, tk=tk, lam_init=lam_init),
        grid=(b, n_heads, n_lat // tq),
        in_specs=[
            pl.BlockSpec((1, hq, 2 * HEAD_DIM), lambda bi, h, i: (bi, 2 * i + q_off, h)),
            pl.BlockSpec((1, hq, 2 * HEAD_DIM), lambda bi, h, i: (bi, 2 * i + 1 + q_off, h)),
            pl.BlockSpec((1, 2, tb, HEAD_DIM), lambda bi, h, i: (bi, h, 0, 0)),
            pl.BlockSpec((1, 1, d_v + SUM_ROWS, tb), lambda bi, h, i: (bi, h, 0, 0)),
            pl.BlockSpec((4, HEAD_DIM), lambda bi, h, i: (0, 0)),
            pl.BlockSpec((d_v, 1), lambda bi, h, i: (0, 0)),
        ],
        out_specs=pl.BlockSpec((1, tq, d_v), lambda bi, h, i: (bi, i, h)),
        out_shape=jax.ShapeDtypeStruct((b, n_lat, n_heads * d_v), BF16),
        scratch_shapes=_stream_scratch(tk, tq // 2, 4, FLASH_BUFFERS),
        compiler_params=_params(("arbitrary", "arbitrary", "arbitrary")),
        name="diff_attention",
    )(q, q, k, vt, lam_vecs, subw)


def _post_kernel(*refs, n_attn):
    n_x = len(refs) - n_attn - 6
    x_refs, mod_ref = refs[:n_x], refs[n_x]
    o_refs = refs[n_x + 1:n_x + 1 + n_attn]
    wo_ref, nw_ref, wi_ref, wf_ref, out_ref = refs[n_x + 1 + n_attn:]
    mod = mod_ref[0, 0]
    kw = wo_ref.shape[0] // n_attn
    a = _dot(o_refs[0][0], wo_ref[0:kw, :])
    for j in range(1, n_attn):
        a = a + _dot(o_refs[j][0], wo_ref[kw * j:kw * (j + 1), :])
    x1 = _stream_tile(x_refs) + mod[2:3, :] * a
    h = _modulated_norm(x1, nw_ref[...], mod[3:4, :], mod[4:5, :])
    u = _dot(h.astype(BF16), wi_ref[...])
    f = wf_ref.shape[0]
    gate = u[:, :f]
    act = (gate / (1.0 + jnp.exp(-gate))) * u[:, f:]
    y = _dot(act.astype(BF16), wf_ref[...])
    out_ref[0] = x1 + mod[5:6, :] * y


def _post(xs, mod_tab, attn_outs, wo, nw, wi, wf, *, latent_only):
    b, d = xs[0].shape[0], xs[0].shape[2]
    tb = sum(a.shape[1] for a in xs)
    tm = ROW_TILE
    off = CTX_LEN // tm if latent_only else 0
    n_rows = tb - CTX_LEN if latent_only else tb
    n_attn = len(attn_outs)

    def const(shape):
        return pl.BlockSpec(shape, lambda bi, i: (0,) * len(shape))

    in_specs = _stream_specs(xs, tm, off) + [
        pl.BlockSpec((1, 1, MOD_ROWS, d), lambda bi, i: (bi, jnp.minimum(i + off, 1), 0, 0)),
    ]
    for o in attn_outs:
        in_specs.append(pl.BlockSpec((1, tm, o.shape[2]), lambda bi, i: (bi, i, 0)))
    in_specs += [const(wo.shape), const(nw.shape), const(wi.shape), const(wf.shape)]
    return pl.pallas_call(
        functools.partial(_post_kernel, n_attn=n_attn),
        grid=(b, n_rows // tm),
        in_specs=in_specs,
        out_specs=pl.BlockSpec((1, tm, d), lambda bi, i: (bi, i, 0)),
        out_shape=jax.ShapeDtypeStruct((b, n_rows, d), F32),
        compiler_params=_params(("arbitrary", "arbitrary")),
        name="post",
    )(*xs, mod_tab, *attn_outs, wo, nw, wi, wf)


def _deinterleave_perm(n_heads):
    one = jnp.concatenate([jnp.arange(0, HEAD_DIM, 2), jnp.arange(1, HEAD_DIM, 2)])
    return (jnp.arange(n_heads)[:, None] * HEAD_DIM + one[None, :]).reshape(-1)


def _rope_tables(n_lat):
    rows = n_lat // GRID_W
    row = jnp.repeat(jnp.arange(rows, dtype=F32), GRID_W)
    col = jnp.tile(jnp.arange(GRID_W, dtype=F32), rows)
    n_freq = HEAD_DIM // 4
    inv = ROPE_THETA ** (-jnp.arange(n_freq, dtype=F32) / n_freq)
    ang = jnp.concatenate([row[:, None] * inv, col[:, None] * inv], axis=-1)
    cos, sin = jnp.cos(ang), jnp.sin(ang)
    cos = jnp.concatenate([jnp.ones((CTX_LEN, HEAD_DIM // 2), F32), cos], axis=0)
    sin = jnp.concatenate([jnp.zeros((CTX_LEN, HEAD_DIM // 2), F32), sin], axis=0)
    cos_t = jnp.tile(jnp.concatenate([cos, cos], axis=-1), (1, 2))
    sin_t = jnp.tile(jnp.concatenate([-sin, sin], axis=-1), (1, 2))
    return cos_t, sin_t


def _gain_row(parts):
    one = _deinterleave_perm(1)
    return jnp.concatenate([jnp.tile(g[one], n) for g, n in parts])[None, :].astype(F32)


def kernel(x, c, ctx, c_ctx, mod_w, mod_b, norm_mix_w, norm_ffn_w, ev_w_in, ev_w_out, ev_qn_a, ev_kn_a,
           ev_qn_b, ev_kn_b, ev_sink_a, od_w_in, od_w_out, od_qn, od_kn, od_lq1, od_lk1, od_lq2, od_lk2,
           od_subln, ffn_w_in, ffn_w_out):
    b, n_lat, d = x.shape
    assert d == D_MODEL and ctx.shape[1] == CTX_LEN and b < MOD_ROWS
    assert n_lat % (DIFF_QUERY_TILES * QUERY_TILE) == 0 and n_lat % KEY_CHUNK == 0 and n_lat % GRID_W == 0
    hd = HEAD_DIM

    cc = jnp.zeros((MOD_ROWS, d), F32).at[:b].set(c).at[b].set(c_ctx)
    mod = _modulation(cc, mod_w, mod_b).reshape(DEPTH, MOD_ROWS, 6, d)
    mod_lat = mod[:, :b]
    mod_ctx = jnp.broadcast_to(mod[:, b:b + 1], mod_lat.shape)
    mod_tab = jnp.stack([mod_ctx, mod_lat], axis=2)
    mod_tab = jnp.pad(mod_tab, ((0, 0), (0, 0), (0, 0), (0, MOD_ROWS - 6), (0, 0)))

    cos_t, sin_t = _rope_tables(n_lat)
    bd = jnp.kron(jnp.eye(256 // hd, dtype=F32), jnp.ones((hd, hd), F32)).astype(BF16)

    w = ev_w_in[0]
    qa, ka, va, qb, kb, vb = jnp.split(w, [512, 640, 768, 1280, 1408], axis=1)
    p8, p2 = _deinterleave_perm(8), _deinterleave_perm(2)
    w0 = jnp.concatenate([qa[:, p8], qb[:, p8], ka[:, p2], kb[:, p2]], axis=1).astype(BF16)
    wvt0 = jnp.concatenate([va, vb], axis=1).T.astype(BF16)
    g0 = _gain_row([(ev_qn_a[0], 8), (ev_qn_b[0], 8), (ev_kn_a[0], 2), (ev_kn_b[0], 2)])
    q0, k0, vt0 = _inproj((ctx, x), mod_tab[0], norm_mix_w[0][None, :], w0, wvt0, g0, cos_t, sin_t, bd,
                          n_k=4, n_v=4, d_v=hd)
    sink_rows = jnp.repeat(ev_sink_a[0].astype(F32) * LOG2E, QUERY_TILE).reshape(2, 1, 4 * QUERY_TILE)
    o_a, o_b = _layer0_attention(q0, k0, vt0, sink_rows, tq=QUERY_TILE, tk=KEY_CHUNK)
    xs = _post((ctx, x), mod_tab[0], [o_a, o_b], ev_w_out[0].astype(BF16), norm_ffn_w[0][None, :],
               ffn_w_in[0].astype(BF16), ffn_w_out[0].astype(BF16), latent_only=False)

    w = od_w_in[0]
    p16 = _deinterleave_perm(16)
    w1 = jnp.concatenate([w[:, :1024][:, p16], w[:, 1024:2048][:, p16]], axis=1).astype(BF16)
    wvt1 = w[:, 2048:].T.astype(BF16)
    g1 = _gain_row([(od_qn[0], 16), (od_kn[0], 16)])
    q1, k1, vt1 = _inproj((xs,), mod_tab[1], norm_mix_w[1][None, :], w1, wvt1, g1, cos_t, sin_t, bd,
                          n_k=16, n_v=8, d_v=2 * hd)
    lam_init = 0.8 - 0.6 * math.exp(-0.3 * 1)
    lam_vecs = jnp.stack([od_lq1[0], od_lk1[0], od_lq2[0], od_lk2[0]]).astype(F32)
    o_c = _diff_attention(q1, k1, vt1, lam_vecs, od_subln[0].astype(F32)[:, None],
                          tq=DIFF_QUERY_TILES * QUERY_TILE, tk=KEY_CHUNK, lam_init=lam_init)
    return _post((xs,), mod_tab[1], [o_c], od_w_out[0].astype(BF16), norm_ffn_w[1][None, :],
                 ffn_w_in[1].astype(BF16), ffn_w_out[1].astype(BF16), latent_only=True)
```

```python
import functools
import math

import jax
import jax.numpy as jnp
from jax import lax
from jax.experimental import pallas as pl
from jax.experimental.pallas import tpu as pltpu

LANES = 128
MXU_TILE = 256
D_MODEL = 1024
HEAD_DIM = 64
CTX_LEN = 256
GRID_W = 64
WINDOW = 128
FFN_HIDDEN = 2816
DEPTH = 2
ROPE_THETA = 10000.0
EPS = 1e-6
NEG_INF = -1e30
LOG2E = 1.4426950408889634
Q_SCALE = HEAD_DIM ** -0.5 * LOG2E
MOD_ROWS = 8
ROW_TILE = 256
MOD_COL_TILE = 1536
KEY_CHUNK = 256
QUERY_TILE = 256
DIFF_QUERY_TILES = 2
FLASH_BUFFERS = 4
WINDOW_BUFFERS = 2
SOFTMAX_ROWS = 64
SUM_ROWS = 16
VMEM_LIMIT = 56 * 1024 * 1024

F32 = jnp.float32
BF16 = jnp.bfloat16


def _nt_dot(a, b):
    return lax.dot_general(a, b, (((1,), (1,)), ((), ())), preferred_element_type=F32)


def _dot(a, b):
    return jnp.dot(a, b, preferred_element_type=F32)


def _params(sem):
    return pltpu.CompilerParams(dimension_semantics=sem, vmem_limit_bytes=VMEM_LIMIT)


def _mod_kernel(cc_ref, w_ref, b_ref, o_ref):
    a = cc_ref[...]
    a = a / (1.0 + jnp.exp(-a))
    o_ref[0] = _dot(a.astype(BF16), w_ref[0].astype(BF16)) + b_ref[0]


def _modulation(cc, mod_w, mod_b):
    depth, d, n = mod_w.shape
    tn = MOD_COL_TILE
    return pl.pallas_call(
        _mod_kernel,
        grid=(depth, n // tn),
        in_specs=[
            pl.BlockSpec((MOD_ROWS, d), lambda l, j: (0, 0)),
            pl.BlockSpec((1, d, tn), lambda l, j: (l, 0, j)),
            pl.BlockSpec((1, 1, tn), lambda l, j: (l, 0, j)),
        ],
        out_specs=pl.BlockSpec((1, MOD_ROWS, tn), lambda l, j: (l, 0, j)),
        out_shape=jax.ShapeDtypeStruct((depth, MOD_ROWS, n), F32),
        compiler_params=_params(("arbitrary", "arbitrary")),
        name="modulation",
    )(cc, mod_w, mod_b.reshape(depth, 1, n))


def _modulated_norm(x, nw, shift, scale):
    ms = jnp.mean(x * x, axis=-1, keepdims=True)
    return (x * lax.rsqrt(ms + EPS) * nw) * (1.0 + scale) + shift


def _stream_specs(xs, tm, off=0):
    if len(xs) == 1:
        return [pl.BlockSpec((1, tm, xs[0].shape[2]), lambda bi, i: (bi, i + off, 0))]
    assert tm == CTX_LEN and off == 0
    d = xs[0].shape[2]
    return [pl.BlockSpec((1, tm, d), lambda bi, i: (bi, 0, 0)),
            pl.BlockSpec((1, tm, d), lambda bi, i: (bi, jnp.maximum(i - 1, 0), 0))]


def _stream_tile(x_refs):
    if len(x_refs) == 1:
        return x_refs[0][0]
    return jnp.where(pl.program_id(1) == 0, x_refs[0][0], x_refs[1][0])


def _inproj_kernel(*refs, n_q, n_k, n_v, d_v):
    x_refs = refs[:-11]
    mod_ref, nw_ref, w_ref, wvt_ref, g_ref, cos_ref, sin_ref, bd_ref, q_ref, k_ref, vt_ref = refs[-11:]
    tm = q_ref.shape[1]
    h = _modulated_norm(_stream_tile(x_refs), nw_ref[...], mod_ref[0, 0, 0:1, :], mod_ref[0, 0, 1:2, :])
    hb = h.astype(BF16)
    y = _dot(hb, w_ref[...])
    cos = cos_ref[...]
    sin = sin_ref[...]
    lane = lax.broadcasted_iota(jnp.int32, (tm, LANES), 1)
    first_half = (lane % HEAD_DIM) < (HEAD_DIM // 2)
    n_norm = n_q + n_k * HEAD_DIM
    for c in range(n_norm // MXU_TILE):
        yc = y[:, MXU_TILE * c:MXU_TILE * (c + 1)]
        ss = _dot((yc * yc).astype(BF16), bd_ref[...])
        z = yc * lax.rsqrt(ss * (1.0 / HEAD_DIM) + EPS) * g_ref[:, MXU_TILE * c:MXU_TILE * (c + 1)]
        for half in range(MXU_TILE // LANES):
            zc = z[:, LANES * half:LANES * (half + 1)]
            partner = jnp.where(first_half, pltpu.roll(zc, LANES - HEAD_DIM // 2, 1),
                                pltpu.roll(zc, HEAD_DIM // 2, 1))
            o = zc * cos + partner * sin
            col = MXU_TILE * c + LANES * half
            if col < n_q:
                q_ref[0, :, col:col + LANES] = (o * Q_SCALE).astype(BF16)
            else:
                kh = (col - n_q) // HEAD_DIM
                k_ref[0, kh] = o[:, :HEAD_DIM].astype(BF16)
                k_ref[0, kh + 1] = o[:, HEAD_DIM:].astype(BF16)
    vt = _nt_dot(wvt_ref[...], hb).astype(BF16)
    ones_rows = (lax.broadcasted_iota(jnp.int32, (SUM_ROWS, tm), 0) == 0).astype(BF16)
    for hv in range(n_v):
        vt_ref[0, hv, 0:d_v, :] = vt[d_v * hv:d_v * (hv + 1), :]
        vt_ref[0, hv, d_v:d_v + SUM_ROWS, :] = ones_rows


def _inproj(xs, mod_tab, nw, w, wvt, gains, cos_t, sin_t, bd, *, n_k, n_v, d_v):
    b, d = xs[0].shape[0], xs[0].shape[2]
    tb = sum(a.shape[1] for a in xs)
    n_q = D_MODEL
    n_in = w.shape[1]
    tm = ROW_TILE
    kern = functools.partial(_inproj_kernel, n_q=n_q, n_k=n_k, n_v=n_v, d_v=d_v)
    return pl.pallas_call(
        kern,
        grid=(b, tb // tm),
        in_specs=_stream_specs(xs, tm) + [
            pl.BlockSpec((1, 1, MOD_ROWS, d), lambda bi, i: (bi, jnp.minimum(i, 1), 0, 0)),
            pl.BlockSpec((1, d), lambda bi, i: (0, 0)),
            pl.BlockSpec((d, n_in), lambda bi, i: (0, 0)),
            pl.BlockSpec((n_v * d_v, d), lambda bi, i: (0, 0)),
            pl.BlockSpec((1, gains.shape[1]), lambda bi, i: (0, 0)),
            pl.BlockSpec((tm, LANES), lambda bi, i: (i, 0)),
            pl.BlockSpec((tm, LANES), lambda bi, i: (i, 0)),
            pl.BlockSpec((MXU_TILE, MXU_TILE), lambda bi, i: (0, 0)),
        ],
        out_specs=[
            pl.BlockSpec((1, tm, n_q), lambda bi, i: (bi, i, 0)),
            pl.BlockSpec((1, n_k, tm, HEAD_DIM), lambda bi, i: (bi, 0, i, 0)),
            pl.BlockSpec((1, n_v, d_v + SUM_ROWS, tm), lambda bi, i: (bi, 0, 0, i)),
        ],
        out_shape=[
            jax.ShapeDtypeStruct((b, tb, n_q), BF16),
            jax.ShapeDtypeStruct((b, n_k, tb, HEAD_DIM), BF16),
            jax.ShapeDtypeStruct((b, n_v, d_v + SUM_ROWS, tb), BF16),
        ],
        compiler_params=_params(("arbitrary", "arbitrary")),
        name="inproj",
    )(*xs, mod_tab, nw, w, wvt, gains, cos_t, sin_t, bd)


def _softmax_step(s_ref, p_ref, m, chunk_max):
    m_new = chunk_max if m is None else jnp.maximum(m, chunk_max)
    for r in range(0, s_ref.shape[0], SOFTMAX_ROWS):
        p_ref[r:r + SOFTMAX_ROWS, :] = jnp.exp2(s_ref[r:r + SOFTMAX_ROWS, :] - m_new).astype(BF16)
    return m_new, (None if m is None else jnp.exp2(m - m_new))


class _Stream:
    def __init__(self, chunks, score, vt_slice, s_bufs, p_bufs):
        self.chunks, self.score, self.vt_slice = chunks, score, vt_slice
        self.s_bufs, self.p_bufs = s_bufs, p_bufs
        self.m = self.acc = None
        self.cmax = {}

    def buf(self, refs, c):
        return refs[c % len(refs)].at[0:self.chunks[c][1], :]

    def write_scores(self, c):
        if c < len(self.chunks):
            s = self.score(*self.chunks[c])
            self.buf(self.s_bufs, c)[...] = s
            self.cmax[c] = jnp.max(s, axis=0, keepdims=True)

    def step(self, c):
        if c >= len(self.chunks):
            return
        self.write_scores(c + 2)
        lo, size, _ = self.chunks[c]
        p_ref = self.buf(self.p_bufs, c)
        self.m, alpha = _softmax_step(self.buf(self.s_bufs, c), p_ref, self.m, self.cmax.pop(c))
        pv = _dot(self.vt_slice(lo, size), p_ref[...])
        self.acc = pv if alpha is None else alpha * self.acc + pv


def _run_streams(streams):
    for c in range(2):
        for st in streams:
            st.write_scores(c)
    for c in range(max(len(st.chunks) for st in streams)):
        for st in streams:
            st.step(c)


def _key_chunks(n_keys, tk):
    return [(0, CTX_LEN, "ctx")] + [(lo, tk, "lat") for lo in range(CTX_LEN, n_keys, tk)]


def _stream_scratch(n_keys, nq, n_streams, n_bufs):
    n = n_streams * n_bufs
    return [pltpu.VMEM((n_keys, nq), F32)] * n + [pltpu.VMEM((n_keys, nq), BF16)] * n


def _split_scratch(scratch, n_streams, n_bufs):
    n = n_streams * n_bufs
    return [(scratch[j * n_bufs:(j + 1) * n_bufs], scratch[n + j * n_bufs:n + (j + 1) * n_bufs])
            for j in range(n_streams)]


def _layer0_attention_kernel(qa_ref, qb_ref, ka_ref, vta_ref, kb_ref, vtb_ref, sink_ref, oa_ref, ob_ref,
                             *scratch, tk):
    tq = qa_ref.shape[1]
    tb = ka_ref.shape[2]
    span = tq + 2 * WINDOW
    t = pl.program_id(2)
    n_dense = 2 * 4 * FLASH_BUFFERS
    dense_bufs = _split_scratch(scratch[:n_dense], 4, FLASH_BUFFERS)
    window_bufs = _split_scratch(scratch[n_dense:], 4, WINDOW_BUFFERS)

    def run(chunks):
        qa = qa_ref[0]
        ws = pl.multiple_of(jnp.clip(tq * t - WINDOW, 0, tb - span), LANES)
        kpos = ws - CTX_LEN + lax.broadcasted_iota(jnp.int32, (span, tq), 0)
        qpos = tq * t - CTX_LEN + lax.broadcasted_iota(jnp.int32, (span, tq), 1)
        ok = (jnp.abs(qpos - kpos) <= WINDOW) & (kpos >= 0) & (qpos >= 0)

        def window_score(g):
            qh = qa[:, HEAD_DIM * g:HEAD_DIM * (g + 1)]

            def score(lo, size, tag):
                s = _nt_dot(ka_ref[0, 0, pl.ds(lo, size), :], qh)
                return jnp.where(ok, s, NEG_INF) if tag == "win" else s
            return score

        win = [_Stream([(0, CTX_LEN, "ctx"), (ws, span, "win")], window_score(g),
                       lambda lo, size: vta_ref[0, 0, :, pl.ds(lo, size)], *window_bufs[g])
               for g in range(4)]

        qb = qb_ref[0]
        qts = [qb[:, HEAD_DIM * g:HEAD_DIM * (g + 1)].astype(F32).T.astype(BF16) for g in range(4)]
        dense = [_Stream(chunks, lambda lo, size, tag, qt=qt: _dot(kb_ref[0, 0, lo:lo + size, :], qt),
                         lambda lo, size: vtb_ref[0, 0, :, lo:lo + size], *dense_bufs[g])
                 for g, qt in enumerate(qts)]

        _run_streams(win + dense)
        outs = []
        for g, st in enumerate(win):
            sink = sink_ref[0, :, tq * g:tq * (g + 1)]
            m_all = jnp.maximum(st.m, sink)
            scale = jnp.exp2(st.m - m_all)
            l = st.acc[HEAD_DIM:HEAD_DIM + 1, :] * scale + jnp.exp2(sink - m_all)
            outs.append((st.acc[:HEAD_DIM, :] * scale / l).T)
        oa_ref[0] = jnp.concatenate(outs, axis=1).astype(BF16)
        outs = [(st.acc[:HEAD_DIM, :] / st.acc[HEAD_DIM:HEAD_DIM + 1, :]).T for st in dense]
        ob_ref[0] = jnp.concatenate(outs, axis=1).astype(BF16)

    chunks = _key_chunks(tb, tk)
    is_latent = t * tq >= CTX_LEN
    pl.when(is_latent)(lambda: run(chunks))
    pl.when(jnp.logical_not(is_latent))(lambda: run(chunks[:1]))


def _layer0_attention(q, k, vt, sink_rows, *, tq, tk):
    b, tb, _ = q.shape
    assert (tb - CTX_LEN) % tk == 0

    def kv_specs(first_head):
        return [pl.BlockSpec((1, 1, tb, HEAD_DIM), lambda bi, h, i: (bi, first_head + h, 0, 0)),
                pl.BlockSpec((1, 1, HEAD_DIM + SUM_ROWS, tb), lambda bi, h, i: (bi, first_head + h, 0, 0))]

    out = jax.ShapeDtypeStruct((b, tb, 512), BF16)
    return pl.pallas_call(
        functools.partial(_layer0_attention_kernel, tk=tk),
        grid=(b, 2, tb // tq),
        in_specs=[pl.BlockSpec((1, tq, 4 * HEAD_DIM), lambda bi, h, i: (bi, i, h)),
                  pl.BlockSpec((1, tq, 4 * HEAD_DIM), lambda bi, h, i: (bi, i, 2 + h))]
        + kv_specs(0) + kv_specs(2) + [pl.BlockSpec((1, 1, 4 * tq), lambda bi, h, i: (h, 0, 0))],
        out_specs=[pl.BlockSpec((1, tq, 4 * HEAD_DIM), lambda bi, h, i: (bi, i, h))] * 2,
        out_shape=[out, out],
        scratch_shapes=(_stream_scratch(tk, tq, 4, FLASH_BUFFERS)
                        + _stream_scratch(tq + 2 * WINDOW, tq, 4, WINDOW_BUFFERS)),
        compiler_params=_params(("arbitrary", "arbitrary", "arbitrary")),
        name="layer0_attention",
    )(q, q, k, vt, k, vt, sink_rows)


def _diff_kernel(qa_ref, qb_ref, k_ref, vt_ref, lam_ref, subw_ref, o_ref, *scratch, tk, lam_init):
    tb = k_ref.shape[2]
    d_v = 2 * HEAD_DIM
    chunks = _key_chunks(tb, tk)
    bufs = _split_scratch(scratch, 4, FLASH_BUFFERS)
    streams = []
    for a in range(2):
        for q_ref in (qa_ref, qb_ref):
            qh = q_ref[0, :, HEAD_DIM * a:HEAD_DIM * (a + 1)]
            streams.append(_Stream(chunks, lambda lo, size, tag, a=a, qh=qh: _nt_dot(k_ref[0, a, lo:lo + size, :], qh),
                                   lambda lo, size: vt_ref[0, 0, :, lo:lo + size], *bufs[len(streams)]))
    _run_streams(streams)
    ots = [st.acc[:d_v, :] / st.acc[d_v:d_v + 1, :] for st in streams]
    ots = [jnp.concatenate(ots[0:2], axis=1), jnp.concatenate(ots[2:4], axis=1)]
    lam = (jnp.exp(jnp.sum(lam_ref[0:1, :] * lam_ref[1:2, :], axis=-1, keepdims=True))
           - jnp.exp(jnp.sum(lam_ref[2:3, :] * lam_ref[3:4, :], axis=-1, keepdims=True)) + lam_init)
    o = ots[0] - lam * ots[1]
    ms = jnp.mean(o * o, axis=0, keepdims=True)
    o = (o * lax.rsqrt(ms + EPS) * subw_ref[...]) * (1.0 - lam_init)
    o_ref[0] = o.T.astype(BF16)


def _diff_attention(q, k, vt, lam_vecs, subw, *, tq, tk, lam_init):
    b, tb, _ = q.shape
    n_lat = tb - CTX_LEN
    n_heads = vt.shape[1]
    d_v = vt.shape[2] - SUM_ROWS
    hq = tq // 2
    assert CTX_LEN % hq == 0
    q_off = CTX_LEN // hq
    return pl.pallas_call(
        functools.partial(_diff_kernel, tk=tk, lam_init=lam_init),
        grid=(b, n_heads, n_lat // tq),
        in_specs=[
            pl.BlockSpec((1, hq, 2 * HEAD_DIM), lambda bi, h, i: (bi, 2 * i + q_off, h)),
            pl.BlockSpec((1, hq, 2 * HEAD_DIM), lambda bi, h, i: (bi, 2 * i + 1 + q_off, h)),
            pl.BlockSpec((1, 2, tb, HEAD_DIM), lambda bi, h, i: (bi, h, 0, 0)),
            pl.BlockSpec((1, 1, d_v + SUM_ROWS, tb), lambda bi, h, i: (bi, h, 0, 0)),
            pl.BlockSpec((4, HEAD_DIM), lambda bi, h, i: (0, 0)),
            pl.BlockSpec((d_v, 1), lambda bi, h, i: (0, 0)),
        ],
        out_specs=pl.BlockSpec((1, tq, d_v), lambda bi, h, i: (bi, i, h)),
        out_shape=jax.ShapeDtypeStruct((b, n_lat, n_heads * d_v), BF16),
        scratch_shapes=_stream_scratch(tk, tq // 2, 4, FLASH_BUFFERS),
        compiler_params=_params(("arbitrary", "arbitrary", "arbitrary")),
        name="diff_attention",
    )(q, q, k, vt, lam_vecs, subw)


def _post_kernel(*refs, n_attn):
    n_x = len(refs) - n_attn - 6
    x_refs, mod_ref = refs[:n_x], refs[n_x]
    o_refs = refs[n_x + 1:n_x + 1 + n_attn]
    wo_ref, nw_ref, wi_ref, wf_ref, out_ref = refs[n_x + 1 + n_attn:]
    mod = mod_ref[0, 0]
    kw = wo_ref.shape[0] // n_attn
    a = _dot(o_refs[0][0], wo_ref[0:kw, :])
    for j in range(1, n_attn):
        a = a + _dot(o_refs[j][0], wo_ref[kw * j:kw * (j + 1), :])
    x1 = _stream_tile(x_refs) + mod[2:3, :] * a
    h = _modulated_norm(x1, nw_ref[...], mod[3:4, :], mod[4:5, :])
    u = _dot(h.astype(BF16), wi_ref[...])
    f = wf_ref.shape[0]
    gate = u[:, :f]
    act = (gate / (1.0 + jnp.exp(-gate))) * u[:, f:]
    y = _dot(act.astype(BF16), wf_ref[...])
    out_ref[0] = x1 + mod[5:6, :] * y


def _post(xs, mod_tab, attn_outs, wo, nw, wi, wf, *, latent_only):
    b, d = xs[0].shape[0], xs[0].shape[2]
    tb = sum(a.shape[1] for a in xs)
    tm = ROW_TILE
    off = CTX_LEN // tm if latent_only else 0
    n_rows = tb - CTX_LEN if latent_only else tb
    n_attn = len(attn_outs)

    def const(shape):
        return pl.BlockSpec(shape, lambda bi, i: (0,) * len(shape))

    in_specs = _stream_specs(xs, tm, off) + [
        pl.BlockSpec((1, 1, MOD_ROWS, d), lambda bi, i: (bi, jnp.minimum(i + off, 1), 0, 0)),
    ]
    for o in attn_outs:
        in_specs.append(pl.BlockSpec((1, tm, o.shape[2]), lambda bi, i: (bi, i, 0)))
    in_specs += [const(wo.shape), const(nw.shape), const(wi.shape), const(wf.shape)]
    return pl.pallas_call(
        functools.partial(_post_kernel, n_attn=n_attn),
        grid=(b, n_rows // tm),
        in_specs=in_specs,
        out_specs=pl.BlockSpec((1, tm, d), lambda bi, i: (bi, i, 0)),
        out_shape=jax.ShapeDtypeStruct((b, n_rows, d), F32),
        compiler_params=_params(("arbitrary", "arbitrary")),
        name="post",
    )(*xs, mod_tab, *attn_outs, wo, nw, wi, wf)


def _deinterleave_perm(n_heads):
    one = jnp.concatenate([jnp.arange(0, HEAD_DIM, 2), jnp.arange(1, HEAD_DIM, 2)])
    return (jnp.arange(n_heads)[:, None] * HEAD_DIM + one[None, :]).reshape(-1)


def _rope_tables(n_lat):
    rows = n_lat // GRID_W
    row = jnp.repeat(jnp.arange(rows, dtype=F32), GRID_W)
    col = jnp.tile(jnp.arange(GRID_W, dtype=F32), rows)
    n_freq = HEAD_DIM // 4
    inv = ROPE_THETA ** (-jnp.arange(n_freq, dtype=F32) / n_freq)
    ang = jnp.concatenate([row[:, None] * inv, col[:, None] * inv], axis=-1)
    cos, sin = jnp.cos(ang), jnp.sin(ang)
    cos = jnp.concatenate([jnp.ones((CTX_LEN, HEAD_DIM // 2), F32), cos], axis=0)
    sin = jnp.concatenate([jnp.zeros((CTX_LEN, HEAD_DIM // 2), F32), sin], axis=0)
    cos_t = jnp.tile(jnp.concatenate([cos, cos], axis=-1), (1, 2))
    sin_t = jnp.tile(jnp.concatenate([-sin, sin], axis=-1), (1, 2))
    return cos_t, sin_t


def _gain_row(parts):
    one = _deinterleave_perm(1)
    return jnp.concatenate([jnp.tile(g[one], n) for g, n in parts])[None, :].astype(F32)


def kernel(x, c, ctx, c_ctx, mod_w, mod_b, norm_mix_w, norm_ffn_w, ev_w_in, ev_w_out, ev_qn_a, ev_kn_a,
           ev_qn_b, ev_kn_b, ev_sink_a, od_w_in, od_w_out, od_qn, od_kn, od_lq1, od_lk1, od_lq2, od_lk2,
           od_subln, ffn_w_in, ffn_w_out):
    b, n_lat, d = x.shape
    assert d == D_MODEL and ctx.shape[1] == CTX_LEN and b < MOD_ROWS
    assert n_lat % (DIFF_QUERY_TILES * QUERY_TILE) == 0 and n_lat % KEY_CHUNK == 0 and n_lat % GRID_W == 0
    hd = HEAD_DIM

    cc = jnp.zeros((MOD_ROWS, d), F32).at[:b].set(c).at[b].set(c_ctx)
    mod = _modulation(cc, mod_w, mod_b).reshape(DEPTH, MOD_ROWS, 6, d)
    mod_lat = mod[:, :b]
    mod_ctx = jnp.broadcast_to(mod[:, b:b + 1], mod_lat.shape)
    mod_tab = jnp.stack([mod_ctx, mod_lat], axis=2)
    mod_tab = jnp.pad(mod_tab, ((0, 0), (0, 0), (0, 0), (0, MOD_ROWS - 6), (0, 0)))

    cos_t, sin_t = _rope_tables(n_lat)
    bd = jnp.kron(jnp.eye(MXU_TILE // hd, dtype=F32), jnp.ones((hd, hd), F32)).astype(BF16)

    w = ev_w_in[0]
    widths = [8 * hd, 2 * hd, 2 * hd, 8 * hd, 2 * hd, 2 * hd]
    qa, ka, va, qb, kb, vb = jnp.split(w, [sum(widths[:n]) for n in range(1, 6)], axis=1)
    p8, p2 = _deinterleave_perm(8), _deinterleave_perm(2)
    w0 = jnp.concatenate([qa[:, p8], qb[:, p8], ka[:, p2], kb[:, p2]], axis=1).astype(BF16)
    wvt0 = jnp.concatenate([va, vb], axis=1).T.astype(BF16)
    g0 = _gain_row([(ev_qn_a[0], 8), (ev_qn_b[0], 8), (ev_kn_a[0], 2), (ev_kn_b[0], 2)])
    q0, k0, vt0 = _inproj((ctx, x), mod_tab[0], norm_mix_w[0][None, :], w0, wvt0, g0, cos_t, sin_t, bd,
                          n_k=4, n_v=4, d_v=hd)
    sink_rows = jnp.repeat(ev_sink_a[0].astype(F32) * LOG2E, QUERY_TILE).reshape(2, 1, 4 * QUERY_TILE)
    o_a, o_b = _layer0_attention(q0, k0, vt0, sink_rows, tq=QUERY_TILE, tk=KEY_CHUNK)
    xs = _post((ctx, x), mod_tab[0], [o_a, o_b], ev_w_out[0].astype(BF16), norm_ffn_w[0][None, :],
               ffn_w_in[0].astype(BF16), ffn_w_out[0].astype(BF16), latent_only=False)

    w = od_w_in[0]
    p16 = _deinterleave_perm(16)
    w1 = jnp.concatenate([w[:, :1024][:, p16], w[:, 1024:2048][:, p16]], axis=1).astype(BF16)
    wvt1 = w[:, 2048:].T.astype(BF16)
    g1 = _gain_row([(od_qn[0], 16), (od_kn[0], 16)])
    q1, k1, vt1 = _inproj((xs,), mod_tab[1], norm_mix_w[1][None, :], w1, wvt1, g1, cos_t, sin_t, bd,
                          n_k=16, n_v=8, d_v=2 * hd)
    lam_init = 0.8 - 0.6 * math.exp(-0.3 * 1)
    lam_vecs = jnp.stack([od_lq1[0], od_lk1[0], od_lq2[0], od_lk2[0]]).astype(F32)
    o_c = _diff_attention(q1, k1, vt1, lam_vecs, od_subln[0].astype(F32)[:, None],
                          tq=DIFF_QUERY_TILES * QUERY_TILE, tk=KEY_CHUNK, lam_init=lam_init)
    return _post((xs,), mod_tab[1], [o_c], od_w_out[0].astype(BF16), norm_ffn_w[1][None, :],
                 ffn_w_in[1].astype(BF16), ffn_w_out[1].astype(BF16), latent_only=True)
```

```python
import functools
import math

import jax
import jax.numpy as jnp
from jax import lax
from jax.experimental import pallas as pl
from jax.experimental.pallas import tpu as pltpu

LANES = 128
MXU_TILE = 256
D_MODEL = 1024
HEAD_DIM = 64
CTX_LEN = 256
GRID_W = 64
WINDOW = 128
FFN_HIDDEN = 2816
DEPTH = 2
ROPE_THETA = 10000.0
EPS = 1e-6
NEG_INF = -1e30
LOG2E = 1.4426950408889634
Q_SCALE = HEAD_DIM ** -0.5 * LOG2E
MOD_ROWS = 8
ROW_TILE = 256
MOD_COL_TILE = 1536
KEY_CHUNK = 256
QUERY_TILE = 256
DIFF_QUERY_TILES = 2
FLASH_BUFFERS = 4
WINDOW_BUFFERS = 2
SOFTMAX_ROWS = 64
SUM_ROWS = 16
VMEM_LIMIT = 56 * 1024 * 1024

F32 = jnp.float32
BF16 = jnp.bfloat16


def _nt_dot(a, b):
    return lax.dot_general(a, b, (((1,), (1,)), ((), ())), preferred_element_type=F32)


def _dot(a, b):
    return jnp.dot(a, b, preferred_element_type=F32)


def _params(sem):
    return pltpu.CompilerParams(dimension_semantics=sem, vmem_limit_bytes=VMEM_LIMIT)


def _mod_kernel(cc_ref, w_ref, b_ref, o_ref):
    a = cc_ref[...]
    a = a / (1.0 + jnp.exp(-a))
    o_ref[0] = _dot(a.astype(BF16), w_ref[0].astype(BF16)) + b_ref[0]


def _modulation(cc, mod_w, mod_b):
    depth, d, n = mod_w.shape
    tn = MOD_COL_TILE
    return pl.pallas_call(
        _mod_kernel,
        grid=(depth, n // tn),
        in_specs=[
            pl.BlockSpec((MOD_ROWS, d), lambda l, j: (0, 0)),
            pl.BlockSpec((1, d, tn), lambda l, j: (l, 0, j)),
            pl.BlockSpec((1, 1, tn), lambda l, j: (l, 0, j)),
        ],
        out_specs=pl.BlockSpec((1, MOD_ROWS, tn), lambda l, j: (l, 0, j)),
        out_shape=jax.ShapeDtypeStruct((depth, MOD_ROWS, n), F32),
        compiler_params=_params(("arbitrary", "arbitrary")),
        name="modulation",
    )(cc, mod_w, mod_b.reshape(depth, 1, n))


def _modulated_norm(x, nw, shift, scale):
    ms = jnp.mean(x * x, axis=-1, keepdims=True)
    return (x * lax.rsqrt(ms + EPS) * nw) * (1.0 + scale) + shift


def _stream_specs(xs, tm, off=0):
    if len(xs) == 1:
        return [pl.BlockSpec((1, tm, xs[0].shape[2]), lambda bi, i: (bi, i + off, 0))]
    assert tm == CTX_LEN and off == 0
    d = xs[0].shape[2]
    return [pl.BlockSpec((1, tm, d), lambda bi, i: (bi, 0, 0)),
            pl.BlockSpec((1, tm, d), lambda bi, i: (bi, jnp.maximum(i - 1, 0), 0))]


def _stream_tile(x_refs):
    if len(x_refs) == 1:
        return x_refs[0][0]
    return jnp.where(pl.program_id(1) == 0, x_refs[0][0], x_refs[1][0])


def _inproj_kernel(*refs, n_q, n_k, n_v, d_v):
    x_refs = refs[:-11]
    mod_ref, nw_ref, w_ref, wvt_ref, g_ref, cos_ref, sin_ref, bd_ref, q_ref, k_ref, vt_ref = refs[-11:]
    tm = q_ref.shape[1]
    h = _modulated_norm(_stream_tile(x_refs), nw_ref[...], mod_ref[0, 0, 0:1, :], mod_ref[0, 0, 1:2, :])
    hb = h.astype(BF16)
    y = _dot(hb, w_ref[...])
    cos = cos_ref[...]
    sin = sin_ref[...]
    lane = lax.broadcasted_iota(jnp.int32, (tm, LANES), 1)
    first_half = (lane % HEAD_DIM) < (HEAD_DIM // 2)
    n_norm = n_q + n_k * HEAD_DIM
    for c in range(n_norm // MXU_TILE):
        yc = y[:, MXU_TILE * c:MXU_TILE * (c + 1)]
        ss = _dot((yc * yc).astype(BF16), bd_ref[...])
        z = yc * lax.rsqrt(ss * (1.0 / HEAD_DIM) + EPS) * g_ref[:, MXU_TILE * c:MXU_TILE * (c + 1)]
        for half in range(MXU_TILE // LANES):
            zc = z[:, LANES * half:LANES * (half + 1)]
            partner = jnp.where(first_half, pltpu.roll(zc, LANES - HEAD_DIM // 2, 1),
                                pltpu.roll(zc, HEAD_DIM // 2, 1))
            o = zc * cos + partner * sin
            col = MXU_TILE * c + LANES * half
            if col < n_q:
                q_ref[0, :, col:col + LANES] = (o * Q_SCALE).astype(BF16)
            else:
                kh = (col - n_q) // HEAD_DIM
                k_ref[0, kh] = o[:, :HEAD_DIM].astype(BF16)
                k_ref[0, kh + 1] = o[:, HEAD_DIM:].astype(BF16)
    vt = _nt_dot(wvt_ref[...], hb).astype(BF16)
    ones_rows = (lax.broadcasted_iota(jnp.int32, (SUM_ROWS, tm), 0) == 0).astype(BF16)
    for hv in range(n_v):
        vt_ref[0, hv, 0:d_v, :] = vt[d_v * hv:d_v * (hv + 1), :]
        vt_ref[0, hv, d_v:d_v + SUM_ROWS, :] = ones_rows


def _inproj(xs, mod_tab, nw, w, wvt, gains, cos_t, sin_t, bd, *, n_k, n_v, d_v):
    b, d = xs[0].shape[0], xs[0].shape[2]
    tb = sum(a.shape[1] for a in xs)
    n_q = D_MODEL
    n_in = w.shape[1]
    tm = ROW_TILE
    kern = functools.partial(_inproj_kernel, n_q=n_q, n_k=n_k, n_v=n_v, d_v=d_v)
    return pl.pallas_call(
        kern,
        grid=(b, tb // tm),
        in_specs=_stream_specs(xs, tm) + [
            pl.BlockSpec((1, 1, MOD_ROWS, d), lambda bi, i: (bi, jnp.minimum(i, 1), 0, 0)),
            pl.BlockSpec((1, d), lambda bi, i: (0, 0)),
            pl.BlockSpec((d, n_in), lambda bi, i: (0, 0)),
            pl.BlockSpec((n_v * d_v, d), lambda bi, i: (0, 0)),
            pl.BlockSpec((1, gains.shape[1]), lambda bi, i: (0, 0)),
            pl.BlockSpec((tm, LANES), lambda bi, i: (i, 0)),
            pl.BlockSpec((tm, LANES), lambda bi, i: (i, 0)),
            pl.BlockSpec((MXU_TILE, MXU_TILE), lambda bi, i: (0, 0)),
        ],
        out_specs=[
            pl.BlockSpec((1, tm, n_q), lambda bi, i: (bi, i, 0)),
            pl.BlockSpec((1, n_k, tm, HEAD_DIM), lambda bi, i: (bi, 0, i, 0)),
            pl.BlockSpec((1, n_v, d_v + SUM_ROWS, tm), lambda bi, i: (bi, 0, 0, i)),
        ],
        out_shape=[
            jax.ShapeDtypeStruct((b, tb, n_q), BF16),
            jax.ShapeDtypeStruct((b, n_k, tb, HEAD_DIM), BF16),
            jax.ShapeDtypeStruct((b, n_v, d_v + SUM_ROWS, tb), BF16),
        ],
        compiler_params=_params(("arbitrary", "arbitrary")),
        name="inproj",
    )(*xs, mod_tab, nw, w, wvt, gains, cos_t, sin_t, bd)


def _softmax_step(s_ref, p_ref, m, chunk_max):
    m_new = chunk_max if m is None else jnp.maximum(m, chunk_max)
    for r in range(0, s_ref.shape[0], SOFTMAX_ROWS):
        p_ref[r:r + SOFTMAX_ROWS, :] = jnp.exp2(s_ref[r:r + SOFTMAX_ROWS, :] - m_new).astype(BF16)
    return m_new, (None if m is None else jnp.exp2(m - m_new))


class _Stream:
    def __init__(self, chunks, score, vt_slice, s_bufs, p_bufs):
        self.chunks, self.score, self.vt_slice = chunks, score, vt_slice
        self.s_bufs, self.p_bufs = s_bufs, p_bufs
        self.m = self.acc = None
        self.cmax = {}

    def buf(self, refs, c):
        return refs[c % len(refs)].at[0:self.chunks[c][1], :]

    def write_scores(self, c):
        if c < len(self.chunks):
            s = self.score(*self.chunks[c])
            self.buf(self.s_bufs, c)[...] = s
            self.cmax[c] = jnp.max(s, axis=0, keepdims=True)

    def step(self, c):
        if c >= len(self.chunks):
            return
        self.write_scores(c + 2)
        lo, size, _ = self.chunks[c]
        p_ref = self.buf(self.p_bufs, c)
        self.m, alpha = _softmax_step(self.buf(self.s_bufs, c), p_ref, self.m, self.cmax.pop(c))
        pv = _dot(self.vt_slice(lo, size), p_ref[...])
        self.acc = pv if alpha is None else alpha * self.acc + pv


def _run_streams(streams):
    for c in range(2):
        for st in streams:
            st.write_scores(c)
    for c in range(max(len(st.chunks) for st in streams)):
        for st in streams:
            st.step(c)


def _key_chunks(n_keys, tk):
    return [(0, CTX_LEN, "ctx")] + [(lo, tk, "lat") for lo in range(CTX_LEN, n_keys, tk)]


def _stream_scratch(n_keys, nq, n_streams, n_bufs):
    n = n_streams * n_bufs
    return [pltpu.VMEM((n_keys, nq), F32)] * n + [pltpu.VMEM((n_keys, nq), BF16)] * n


def _split_scratch(scratch, n_streams, n_bufs):
    n = n_streams * n_bufs
    return [(scratch[j * n_bufs:(j + 1) * n_bufs], scratch[n + j * n_bufs:n + (j + 1) * n_bufs])
            for j in range(n_streams)]


def _layer0_attention_kernel(qa_ref, qb_ref, ka_ref, vta_ref, kb_ref, vtb_ref, sink_ref, oa_ref, ob_ref,
                             *scratch, tk):
    tq = qa_ref.shape[1]
    tb = ka_ref.shape[2]
    span = tq + 2 * WINDOW
    t = pl.program_id(2)
    n_dense = 2 * 4 * FLASH_BUFFERS
    dense_bufs = _split_scratch(scratch[:n_dense], 4, FLASH_BUFFERS)
    window_bufs = _split_scratch(scratch[n_dense:], 4, WINDOW_BUFFERS)

    def run(chunks):
        qa = qa_ref[0]
        ws = pl.multiple_of(jnp.clip(tq * t - WINDOW, 0, tb - span), LANES)
        kpos = ws - CTX_LEN + lax.broadcasted_iota(jnp.int32, (span, tq), 0)
        qpos = tq * t - CTX_LEN + lax.broadcasted_iota(jnp.int32, (span, tq), 1)
        ok = (jnp.abs(qpos - kpos) <= WINDOW) & (kpos >= 0) & (qpos >= 0)

        def window_score(g):
            qh = qa[:, HEAD_DIM * g:HEAD_DIM * (g + 1)]

            def score(lo, size, tag):
                s = _nt_dot(ka_ref[0, 0, pl.ds(lo, size), :], qh)
                return jnp.where(ok, s, NEG_INF) if tag == "win" else s
            return score

        win = [_Stream([(0, CTX_LEN, "ctx"), (ws, span, "win")], window_score(g),
                       lambda lo, size: vta_ref[0, 0, :, pl.ds(lo, size)], *window_bufs[g])
               for g in range(4)]

        qb = qb_ref[0]
        qts = [qb[:, HEAD_DIM * g:HEAD_DIM * (g + 1)].astype(F32).T.astype(BF16) for g in range(4)]
        dense = [_Stream(chunks, lambda lo, size, tag, qt=qt: _dot(kb_ref[0, 0, lo:lo + size, :], qt),
                         lambda lo, size: vtb_ref[0, 0, :, lo:lo + size], *dense_bufs[g])
                 for g, qt in enumerate(qts)]

        _run_streams(win + dense)
        for g, st in enumerate(win):
            sink = sink_ref[0, :, tq * g:tq * (g + 1)]
            m_all = jnp.maximum(st.m, sink)
            scale = jnp.exp2(st.m - m_all)
            l = st.acc[HEAD_DIM:HEAD_DIM + 1, :] * scale + jnp.exp2(sink - m_all)
            oa_ref[0, g] = (st.acc[:HEAD_DIM, :] * scale / l).astype(BF16)
        for g, st in enumerate(dense):
            ob_ref[0, g] = (st.acc[:HEAD_DIM, :] / st.acc[HEAD_DIM:HEAD_DIM + 1, :]).astype(BF16)

    chunks = _key_chunks(tb, tk)
    is_latent = t * tq >= CTX_LEN
    pl.when(is_latent)(lambda: run(chunks))
    pl.when(jnp.logical_not(is_latent))(lambda: run(chunks[:1]))


def _layer0_attention(q, k, vt, sink_rows, *, tq, tk):
    b, tb, _ = q.shape
    assert (tb - CTX_LEN) % tk == 0

    def kv_specs(first_head):
        return [pl.BlockSpec((1, 1, tb, HEAD_DIM), lambda bi, h, i: (bi, first_head + h, 0, 0)),
                pl.BlockSpec((1, 1, HEAD_DIM + SUM_ROWS, tb), lambda bi, h, i: (bi, first_head + h, 0, 0))]

    out = jax.ShapeDtypeStruct((b, 8, HEAD_DIM, tb), BF16)
    return pl.pallas_call(
        functools.partial(_layer0_attention_kernel, tk=tk),
        grid=(b, 2, tb // tq),
        in_specs=[pl.BlockSpec((1, tq, 4 * HEAD_DIM), lambda bi, h, i: (bi, i, h)),
                  pl.BlockSpec((1, tq, 4 * HEAD_DIM), lambda bi, h, i: (bi, i, 2 + h))]
        + kv_specs(0) + kv_specs(2) + [pl.BlockSpec((1, 1, 4 * tq), lambda bi, h, i: (h, 0, 0))],
        out_specs=[pl.BlockSpec((1, 4, HEAD_DIM, tq), lambda bi, h, i: (bi, h, 0, i))] * 2,
        out_shape=[out, out],
        scratch_shapes=(_stream_scratch(tk, tq, 4, FLASH_BUFFERS)
                        + _stream_scratch(tq + 2 * WINDOW, tq, 4, WINDOW_BUFFERS)),
        compiler_params=_params(("arbitrary", "arbitrary", "arbitrary")),
        name="layer0_attention",
    )(q, q, k, vt, k, vt, sink_rows)


def _diff_kernel(qa_ref, qb_ref, k_ref, vt_ref, lam_ref, subw_ref, o_ref, *scratch, tk, lam_init):
    tb = k_ref.shape[2]
    d_v = 2 * HEAD_DIM
    chunks = _key_chunks(tb, tk)
    bufs = _split_scratch(scratch, 4, FLASH_BUFFERS)
    streams = []
    for a in range(2):
        for q_ref in (qa_ref, qb_ref):
            qh = q_ref[0, :, HEAD_DIM * a:HEAD_DIM * (a + 1)]
            streams.append(_Stream(chunks, lambda lo, size, tag, a=a, qh=qh: _nt_dot(k_ref[0, a, lo:lo + size, :], qh),
                                   lambda lo, size: vt_ref[0, 0, :, lo:lo + size], *bufs[len(streams)]))
    _run_streams(streams)
    ots = [st.acc[:d_v, :] / st.acc[d_v:d_v + 1, :] for st in streams]
    ots = [jnp.concatenate(ots[0:2], axis=1), jnp.concatenate(ots[2:4], axis=1)]
    lam = (jnp.exp(jnp.sum(lam_ref[0:1, :] * lam_ref[1:2, :], axis=-1, keepdims=True))
           - jnp.exp(jnp.sum(lam_ref[2:3, :] * lam_ref[3:4, :], axis=-1, keepdims=True)) + lam_init)
    o = ots[0] - lam * ots[1]
    ms = jnp.mean(o * o, axis=0, keepdims=True)
    o = (o * lax.rsqrt(ms + EPS) * subw_ref[...]) * (1.0 - lam_init)
    o_ref[0, 0] = o.astype(BF16)


def _diff_attention(q, k, vt, lam_vecs, subw, *, tq, tk, lam_init):
    b, tb, _ = q.shape
    n_lat = tb - CTX_LEN
    n_heads = vt.shape[1]
    d_v = vt.shape[2] - SUM_ROWS
    hq = tq // 2
    assert CTX_LEN % hq == 0
    q_off = CTX_LEN // hq
    return pl.pallas_call(
        functools.partial(_diff_kernel, tk=tk, lam_init=lam_init),
        grid=(b, n_heads, n_lat // tq),
        in_specs=[
            pl.BlockSpec((1, hq, 2 * HEAD_DIM), lambda bi, h, i: (bi, 2 * i + q_off, h)),
            pl.BlockSpec((1, hq, 2 * HEAD_DIM), lambda bi, h, i: (bi, 2 * i + 1 + q_off, h)),
            pl.BlockSpec((1, 2, tb, HEAD_DIM), lambda bi, h, i: (bi, h, 0, 0)),
            pl.BlockSpec((1, 1, d_v + SUM_ROWS, tb), lambda bi, h, i: (bi, h, 0, 0)),
            pl.BlockSpec((4, HEAD_DIM), lambda bi, h, i: (0, 0)),
            pl.BlockSpec((d_v, 1), lambda bi, h, i: (0, 0)),
        ],
        out_specs=pl.BlockSpec((1, 1, d_v, tq), lambda bi, h, i: (bi, h, 0, i)),
        out_shape=jax.ShapeDtypeStruct((b, n_heads, d_v, n_lat), BF16),
        scratch_shapes=_stream_scratch(tk, tq // 2, 4, FLASH_BUFFERS),
        compiler_params=_params(("arbitrary", "arbitrary", "arbitrary")),
        name="diff_attention",
    )(q, q, k, vt, lam_vecs, subw)


def _post_kernel(*refs, n_attn):
    n_x = len(refs) - n_attn - 6
    x_refs, mod_ref = refs[:n_x], refs[n_x]
    o_refs = refs[n_x + 1:n_x + 1 + n_attn]
    wo_ref, nw_ref, wi_ref, wf_ref, out_ref = refs[n_x + 1 + n_attn:]
    mod = mod_ref[0, 0]
    kw = wo_ref.shape[0] // n_attn
    a = None
    for j, o_ref in enumerate(o_refs):
        ot = o_ref[0].reshape(kw, o_ref.shape[3])
        part = lax.dot_general(ot, wo_ref[kw * j:kw * (j + 1), :], (((0,), (0,)), ((), ())),
                               preferred_element_type=F32)
        a = part if a is None else a + part
    x1 = _stream_tile(x_refs) + mod[2:3, :] * a
    h = _modulated_norm(x1, nw_ref[...], mod[3:4, :], mod[4:5, :])
    u = _dot(h.astype(BF16), wi_ref[...])
    f = wf_ref.shape[0]
    gate = u[:, :f]
    act = (gate / (1.0 + jnp.exp(-gate))) * u[:, f:]
    y = _dot(act.astype(BF16), wf_ref[...])
    out_ref[0] = x1 + mod[5:6, :] * y


def _post(xs, mod_tab, attn_outs, wo, nw, wi, wf, *, latent_only):
    b, d = xs[0].shape[0], xs[0].shape[2]
    tb = sum(a.shape[1] for a in xs)
    tm = ROW_TILE
    off = CTX_LEN // tm if latent_only else 0
    n_rows = tb - CTX_LEN if latent_only else tb
    n_attn = len(attn_outs)

    def const(shape):
        return pl.BlockSpec(shape, lambda bi, i: (0,) * len(shape))

    in_specs = _stream_specs(xs, tm, off) + [
        pl.BlockSpec((1, 1, MOD_ROWS, d), lambda bi, i: (bi, jnp.minimum(i + off, 1), 0, 0)),
    ]
    for o in attn_outs:
        in_specs.append(pl.BlockSpec((1, o.shape[1], o.shape[2], tm), lambda bi, i: (bi, 0, 0, i)))
    in_specs += [const(wo.shape), const(nw.shape), const(wi.shape), const(wf.shape)]
    return pl.pallas_call(
        functools.partial(_post_kernel, n_attn=n_attn),
        grid=(b, n_rows // tm),
        in_specs=in_specs,
        out_specs=pl.BlockSpec((1, tm, d), lambda bi, i: (bi, i, 0)),
        out_shape=jax.ShapeDtypeStruct((b, n_rows, d), F32),
        compiler_params=_params(("arbitrary", "arbitrary")),
        name="post",
    )(*xs, mod_tab, *attn_outs, wo, nw, wi, wf)


def _deinterleave_perm(n_heads):
    one = jnp.concatenate([jnp.arange(0, HEAD_DIM, 2), jnp.arange(1, HEAD_DIM, 2)])
    return (jnp.arange(n_heads)[:, None] * HEAD_DIM + one[None, :]).reshape(-1)


def _rope_tables(n_lat):
    rows = n_lat // GRID_W
    row = jnp.repeat(jnp.arange(rows, dtype=F32), GRID_W)
    col = jnp.tile(jnp.arange(GRID_W, dtype=F32), rows)
    n_freq = HEAD_DIM // 4
    inv = ROPE_THETA ** (-jnp.arange(n_freq, dtype=F32) / n_freq)
    ang = jnp.concatenate([row[:, None] * inv, col[:, None] * inv], axis=-1)
    cos, sin = jnp.cos(ang), jnp.sin(ang)
    cos = jnp.concatenate([jnp.ones((CTX_LEN, HEAD_DIM // 2), F32), cos], axis=0)
    sin = jnp.concatenate([jnp.zeros((CTX_LEN, HEAD_DIM // 2), F32), sin], axis=0)
    cos_t = jnp.tile(jnp.concatenate([cos, cos], axis=-1), (1, 2))
    sin_t = jnp.tile(jnp.concatenate([-sin, sin], axis=-1), (1, 2))
    return cos_t, sin_t


def _gain_row(parts):
    one = _deinterleave_perm(1)
    return jnp.concatenate([jnp.tile(g[one], n) for g, n in parts])[None, :].astype(F32)


def kernel(x, c, ctx, c_ctx, mod_w, mod_b, norm_mix_w, norm_ffn_w, ev_w_in, ev_w_out, ev_qn_a, ev_kn_a,
           ev_qn_b, ev_kn_b, ev_sink_a, od_w_in, od_w_out, od_qn, od_kn, od_lq1, od_lk1, od_lq2, od_lk2,
           od_subln, ffn_w_in, ffn_w_out):
    b, n_lat, d = x.shape
    assert d == D_MODEL and ctx.shape[1] == CTX_LEN and b < MOD_ROWS
    assert n_lat % (DIFF_QUERY_TILES * QUERY_TILE) == 0 and n_lat % KEY_CHUNK == 0 and n_lat % GRID_W == 0
    hd = HEAD_DIM

    cc = jnp.zeros((MOD_ROWS, d), F32).at[:b].set(c).at[b].set(c_ctx)
    mod = _modulation(cc, mod_w, mod_b).reshape(DEPTH, MOD_ROWS, 6, d)
    mod_lat = mod[:, :b]
    mod_ctx = jnp.broadcast_to(mod[:, b:b + 1], mod_lat.shape)
    mod_tab = jnp.stack([mod_ctx, mod_lat], axis=2)
    mod_tab = jnp.pad(mod_tab, ((0, 0), (0, 0), (0, 0), (0, MOD_ROWS - 6), (0, 0)))

    cos_t, sin_t = _rope_tables(n_lat)
    bd = jnp.kron(jnp.eye(MXU_TILE // hd, dtype=F32), jnp.ones((hd, hd), F32)).astype(BF16)

    w = ev_w_in[0]
    widths = [8 * hd, 2 * hd, 2 * hd, 8 * hd, 2 * hd, 2 * hd]
    qa, ka, va, qb, kb, vb = jnp.split(w, [sum(widths[:n]) for n in range(1, 6)], axis=1)
    p8, p2 = _deinterleave_perm(8), _deinterleave_perm(2)
    w0 = jnp.concatenate([qa[:, p8], qb[:, p8], ka[:, p2], kb[:, p2]], axis=1).astype(BF16)
    wvt0 = jnp.concatenate([va, vb], axis=1).T.astype(BF16)
    g0 = _gain_row([(ev_qn_a[0], 8), (ev_qn_b[0], 8), (ev_kn_a[0], 2), (ev_kn_b[0], 2)])
    q0, k0, vt0 = _inproj((ctx, x), mod_tab[0], norm_mix_w[0][None, :], w0, wvt0, g0, cos_t, sin_t, bd,
                          n_k=4, n_v=4, d_v=hd)
    sink_rows = jnp.repeat(ev_sink_a[0].astype(F32) * LOG2E, QUERY_TILE).reshape(2, 1, 4 * QUERY_TILE)
    o_a, o_b = _layer0_attention(q0, k0, vt0, sink_rows, tq=QUERY_TILE, tk=KEY_CHUNK)
    xs = _post((ctx, x), mod_tab[0], [o_a, o_b], ev_w_out[0].astype(BF16), norm_ffn_w[0][None, :],
               ffn_w_in[0].astype(BF16), ffn_w_out[0].astype(BF16), latent_only=False)

    w = od_w_in[0]
    p16 = _deinterleave_perm(16)
    w1 = jnp.concatenate([w[:, :1024][:, p16], w[:, 1024:2048][:, p16]], axis=1).astype(BF16)
    wvt1 = w[:, 2048:].T.astype(BF16)
    g1 = _gain_row([(od_qn[0], 16), (od_kn[0], 16)])
    q1, k1, vt1 = _inproj((xs,), mod_tab[1], norm_mix_w[1][None, :], w1, wvt1, g1, cos_t, sin_t, bd,
                          n_k=16, n_v=8, d_v=2 * hd)
    lam_init = 0.8 - 0.6 * math.exp(-0.3 * 1)
    lam_vecs = jnp.stack([od_lq1[0], od_lk1[0], od_lq2[0], od_lk2[0]]).astype(F32)
    o_c = _diff_attention(q1, k1, vt1, lam_vecs, od_subln[0].astype(F32)[:, None],
                          tq=DIFF_QUERY_TILES * QUERY_TILE, tk=KEY_CHUNK, lam_init=lam_init)
    return _post((xs,), mod_tab[1], [o_c], od_w_out[0].astype(BF16), norm_ffn_w[1][None, :],
                 ffn_w_in[1].astype(BF16), ffn_w_out[1].astype(BF16), latent_only=True)
```

```python
import functools
import math

import jax
import jax.numpy as jnp
from jax import lax
from jax.experimental import pallas as pl
from jax.experimental.pallas import tpu as pltpu

LANES = 128
MXU_TILE = 256
D_MODEL = 1024
HEAD_DIM = 64
CTX_LEN = 256
GRID_W = 64
WINDOW = 128
DEPTH = 2
ROPE_THETA = 10000.0
EPS = 1e-6
NEG_INF = -1e30
LOG2E = 1.4426950408889634
Q_SCALE = HEAD_DIM ** -0.5 * LOG2E
MOD_ROWS = 8
ROW_TILE = 256
MOD_COL_TILE = 1536
KEY_CHUNK = 256
QUERY_TILE = 256
DIFF_QUERY_TILES = 2
FLASH_BUFFERS = 4
WINDOW_BUFFERS = 2
SOFTMAX_ROWS = 64
SUM_ROWS = 16
VMEM_LIMIT = 56 * 1024 * 1024

F32 = jnp.float32
BF16 = jnp.bfloat16


def _nt_dot(a, b):
    return lax.dot_general(a, b, (((1,), (1,)), ((), ())), preferred_element_type=F32)


def _dot(a, b):
    return jnp.dot(a, b, preferred_element_type=F32)


def _params(sem):
    return pltpu.CompilerParams(dimension_semantics=sem, vmem_limit_bytes=VMEM_LIMIT)


def _mod_kernel(cc_ref, w_ref, b_ref, o_ref):
    a = cc_ref[...]
    a = a / (1.0 + jnp.exp(-a))
    o_ref[0] = _dot(a.astype(BF16), w_ref[0].astype(BF16)) + b_ref[0]


def _modulation(cc, mod_w, mod_b):
    depth, d, n = mod_w.shape
    tn = MOD_COL_TILE
    return pl.pallas_call(
        _mod_kernel,
        grid=(depth, n // tn),
        in_specs=[
            pl.BlockSpec((MOD_ROWS, d), lambda l, j: (0, 0)),
            pl.BlockSpec((1, d, tn), lambda l, j: (l, 0, j)),
            pl.BlockSpec((1, 1, tn), lambda l, j: (l, 0, j)),
        ],
        out_specs=pl.BlockSpec((1, MOD_ROWS, tn), lambda l, j: (l, 0, j)),
        out_shape=jax.ShapeDtypeStruct((depth, MOD_ROWS, n), F32),
        compiler_params=_params(("arbitrary", "arbitrary")),
        name="modulation",
    )(cc, mod_w, mod_b.reshape(depth, 1, n))


def _modulated_norm(x, nw, shift, scale):
    ms = jnp.mean(x * x, axis=-1, keepdims=True)
    return (x * lax.rsqrt(ms + EPS) * nw) * (1.0 + scale) + shift


def _stream_specs(xs, tm, off=0):
    if len(xs) == 1:
        return [pl.BlockSpec((1, tm, xs[0].shape[2]), lambda bi, i: (bi, i + off, 0))]
    assert tm == CTX_LEN and off == 0
    d = xs[0].shape[2]
    return [pl.BlockSpec((1, tm, d), lambda bi, i: (bi, 0, 0)),
            pl.BlockSpec((1, tm, d), lambda bi, i: (bi, jnp.maximum(i - 1, 0), 0))]


def _stream_tile(x_refs):
    if len(x_refs) == 1:
        return x_refs[0][0]
    return jnp.where(pl.program_id(1) == 0, x_refs[0][0], x_refs[1][0])


def _inproj_kernel(*refs, n_q, n_k, n_v, d_v, k_pairs):
    x_refs = refs[:-11]
    mod_ref, nw_ref, w_ref, wvt_ref, g_ref, cos_ref, sin_ref, bd_ref, q_ref, k_ref, vt_ref = refs[-11:]
    tm = q_ref.shape[1]
    h = _modulated_norm(_stream_tile(x_refs), nw_ref[...], mod_ref[0, 0, 0:1, :], mod_ref[0, 0, 1:2, :])
    hb = h.astype(BF16)
    y = _dot(hb, w_ref[...])
    cos = cos_ref[...]
    sin = sin_ref[...]
    lane = lax.broadcasted_iota(jnp.int32, (tm, LANES), 1)
    first_half = (lane % HEAD_DIM) < (HEAD_DIM // 2)
    n_norm = n_q + n_k * HEAD_DIM
    for c in range(n_norm // MXU_TILE):
        yc = y[:, MXU_TILE * c:MXU_TILE * (c + 1)]
        ss = _dot((yc * yc).astype(BF16), bd_ref[...])
        z = yc * lax.rsqrt(ss * (1.0 / HEAD_DIM) + EPS) * g_ref[:, MXU_TILE * c:MXU_TILE * (c + 1)]
        for half in range(MXU_TILE // LANES):
            zc = z[:, LANES * half:LANES * (half + 1)]
            partner = jnp.where(first_half, pltpu.roll(zc, LANES - HEAD_DIM // 2, 1),
                                pltpu.roll(zc, HEAD_DIM // 2, 1))
            o = zc * cos + partner * sin
            col = MXU_TILE * c + LANES * half
            if col < n_q:
                q_ref[0, :, col:col + LANES] = (o * Q_SCALE).astype(BF16)
            else:
                kh = (col - n_q) // HEAD_DIM
                if k_pairs:
                    k_ref[0, kh // 2] = o.astype(BF16)
                else:
                    k_ref[0, kh] = o[:, :HEAD_DIM].astype(BF16)
                    k_ref[0, kh + 1] = o[:, HEAD_DIM:].astype(BF16)
    vt = _nt_dot(wvt_ref[...], hb).astype(BF16)
    ones_rows = (lax.broadcasted_iota(jnp.int32, (SUM_ROWS, tm), 0) == 0).astype(BF16)
    for hv in range(n_v):
        vt_ref[0, hv, 0:d_v, :] = vt[d_v * hv:d_v * (hv + 1), :]
        vt_ref[0, hv, d_v:d_v + SUM_ROWS, :] = ones_rows


def _inproj(xs, mod_tab, nw, w, wvt, gains, cos_t, sin_t, bd, *, n_k, n_v, d_v, k_pairs=False):
    b, d = xs[0].shape[0], xs[0].shape[2]
    tb = sum(a.shape[1] for a in xs)
    n_q = D_MODEL
    n_in = w.shape[1]
    tm = ROW_TILE
    kern = functools.partial(_inproj_kernel, n_q=n_q, n_k=n_k, n_v=n_v, d_v=d_v, k_pairs=k_pairs)
    k_blocks, k_width = (n_k // 2, 2 * HEAD_DIM) if k_pairs else (n_k, HEAD_DIM)
    return pl.pallas_call(
        kern,
        grid=(b, tb // tm),
        in_specs=_stream_specs(xs, tm) + [
            pl.BlockSpec((1, 1, MOD_ROWS, d), lambda bi, i: (bi, jnp.minimum(i, 1), 0, 0)),
            pl.BlockSpec((1, d), lambda bi, i: (0, 0)),
            pl.BlockSpec((d, n_in), lambda bi, i: (0, 0)),
            pl.BlockSpec((n_v * d_v, d), lambda bi, i: (0, 0)),
            pl.BlockSpec((1, gains.shape[1]), lambda bi, i: (0, 0)),
            pl.BlockSpec((tm, LANES), lambda bi, i: (i, 0)),
            pl.BlockSpec((tm, LANES), lambda bi, i: (i, 0)),
            pl.BlockSpec((MXU_TILE, MXU_TILE), lambda bi, i: (0, 0)),
        ],
        out_specs=[
            pl.BlockSpec((1, tm, n_q), lambda bi, i: (bi, i, 0)),
            pl.BlockSpec((1, k_blocks, tm, k_width), lambda bi, i: (bi, 0, i, 0)),
            pl.BlockSpec((1, n_v, d_v + SUM_ROWS, tm), lambda bi, i: (bi, 0, 0, i)),
        ],
        out_shape=[
            jax.ShapeDtypeStruct((b, tb, n_q), BF16),
            jax.ShapeDtypeStruct((b, k_blocks, tb, k_width), BF16),
            jax.ShapeDtypeStruct((b, n_v, d_v + SUM_ROWS, tb), BF16),
        ],
        compiler_params=_params(("arbitrary", "arbitrary")),
        name="inproj",
    )(*xs, mod_tab, nw, w, wvt, gains, cos_t, sin_t, bd)


def _softmax_step(s_ref, p_ref, m, chunk_max):
    m_new = chunk_max if m is None else jnp.maximum(m, chunk_max)
    for r in range(0, s_ref.shape[0], SOFTMAX_ROWS):
        p_ref[r:r + SOFTMAX_ROWS, :] = jnp.exp2(s_ref[r:r + SOFTMAX_ROWS, :] - m_new).astype(BF16)
    return m_new, (None if m is None else jnp.exp2(m - m_new))


class _Stream:
    def __init__(self, chunks, score, vt_slice, s_bufs, p_bufs):
        self.chunks, self.score, self.vt_slice = chunks, score, vt_slice
        self.s_bufs, self.p_bufs = s_bufs, p_bufs
        self.m = self.acc = None
        self.cmax = {}

    def buf(self, refs, c):
        return refs[c % len(refs)].at[0:self.chunks[c][1], :]

    def write_scores(self, c):
        if c < len(self.chunks):
            s = self.score(*self.chunks[c])
            self.buf(self.s_bufs, c)[...] = s
            self.cmax[c] = jnp.max(s, axis=0, keepdims=True)

    def step(self, c):
        if c >= len(self.chunks):
            return
        self.write_scores(c + 2)
        lo, size, _ = self.chunks[c]
        p_ref = self.buf(self.p_bufs, c)
        self.m, alpha = _softmax_step(self.buf(self.s_bufs, c), p_ref, self.m, self.cmax.pop(c))
        pv = _dot(self.vt_slice(lo, size), p_ref[...])
        self.acc = pv if alpha is None else alpha * self.acc + pv


def _run_streams(streams):
    for c in range(2):
        for st in streams:
            st.write_scores(c)
    for c in range(max(len(st.chunks) for st in streams)):
        for st in streams:
            st.step(c)


def _key_chunks(n_keys, tk):
    return [(0, CTX_LEN, "ctx")] + [(lo, tk, "lat") for lo in range(CTX_LEN, n_keys, tk)]


def _stream_scratch(n_keys, nq, n_streams, n_bufs):
    n = n_streams * n_bufs
    return [pltpu.VMEM((n_keys, nq), F32)] * n + [pltpu.VMEM((n_keys, nq), BF16)] * n


def _split_scratch(scratch, n_streams, n_bufs):
    n = n_streams * n_bufs
    return [(scratch[j * n_bufs:(j + 1) * n_bufs], scratch[n + j * n_bufs:n + (j + 1) * n_bufs])
            for j in range(n_streams)]


def _layer0_attention_kernel(qa_ref, qb_ref, ka_ref, vta_ref, kb_ref, vtb_ref, sink_ref, oa_ref, ob_ref,
                             *scratch, tk):
    tq = qa_ref.shape[1]
    tb = ka_ref.shape[2]
    span = tq + 2 * WINDOW
    t = pl.program_id(2)
    n_dense = 2 * 4 * FLASH_BUFFERS
    dense_bufs = _split_scratch(scratch[:n_dense], 4, FLASH_BUFFERS)
    window_bufs = _split_scratch(scratch[n_dense:], 4, WINDOW_BUFFERS)

    def run(chunks):
        qa = qa_ref[0]
        ws = pl.multiple_of(jnp.clip(tq * t - WINDOW, 0, tb - span), LANES)
        kpos = ws - CTX_LEN + lax.broadcasted_iota(jnp.int32, (span, tq), 0)
        qpos = tq * t - CTX_LEN + lax.broadcasted_iota(jnp.int32, (span, tq), 1)
        ok = (jnp.abs(qpos - kpos) <= WINDOW) & (kpos >= 0) & (qpos >= 0)

        def window_score(g):
            qh = qa[:, HEAD_DIM * g:HEAD_DIM * (g + 1)]

            def score(lo, size, tag):
                s = _nt_dot(ka_ref[0, 0, pl.ds(lo, size), :], qh)
                return jnp.where(ok, s, NEG_INF) if tag == "win" else s
            return score

        win = [_Stream([(0, CTX_LEN, "ctx"), (ws, span, "win")], window_score(g),
                       lambda lo, size: vta_ref[0, 0, :, pl.ds(lo, size)], *window_bufs[g])
               for g in range(4)]

        qb = qb_ref[0]
        qts = [qb[:, HEAD_DIM * g:HEAD_DIM * (g + 1)].astype(F32).T.astype(BF16) for g in range(4)]
        dense = [_Stream(chunks, lambda lo, size, tag, qt=qt: _dot(kb_ref[0, 0, lo:lo + size, :], qt),
                         lambda lo, size: vtb_ref[0, 0, :, lo:lo + size], *dense_bufs[g])
                 for g, qt in enumerate(qts)]

        _run_streams(win + dense)
        for g, st in enumerate(win):
            sink = sink_ref[0, :, tq * g:tq * (g + 1)]
            m_all = jnp.maximum(st.m, sink)
            scale = jnp.exp2(st.m - m_all)
            l = st.acc[HEAD_DIM:HEAD_DIM + 1, :] * scale + jnp.exp2(sink - m_all)
            oa_ref[0, g] = (st.acc[:HEAD_DIM, :] * scale / l).astype(BF16)
        for g, st in enumerate(dense):
            ob_ref[0, g] = (st.acc[:HEAD_DIM, :] / st.acc[HEAD_DIM:HEAD_DIM + 1, :]).astype(BF16)

    chunks = _key_chunks(tb, tk)
    is_latent = t * tq >= CTX_LEN
    pl.when(is_latent)(lambda: run(chunks))
    pl.when(jnp.logical_not(is_latent))(lambda: run(chunks[:1]))


def _layer0_attention(q, k, vt, sink_rows, *, tq, tk):
    b, tb, _ = q.shape
    assert (tb - CTX_LEN) % tk == 0

    def kv_specs(first_head):
        return [pl.BlockSpec((1, 1, tb, HEAD_DIM), lambda bi, h, i: (bi, first_head + h, 0, 0)),
                pl.BlockSpec((1, 1, HEAD_DIM + SUM_ROWS, tb), lambda bi, h, i: (bi, first_head + h, 0, 0))]

    out = jax.ShapeDtypeStruct((b, 8, HEAD_DIM, tb), BF16)
    return pl.pallas_call(
        functools.partial(_layer0_attention_kernel, tk=tk),
        grid=(b, 2, tb // tq),
        in_specs=[pl.BlockSpec((1, tq, 4 * HEAD_DIM), lambda bi, h, i: (bi, i, h)),
                  pl.BlockSpec((1, tq, 4 * HEAD_DIM), lambda bi, h, i: (bi, i, 2 + h))]
        + kv_specs(0) + kv_specs(2) + [pl.BlockSpec((1, 1, 4 * tq), lambda bi, h, i: (h, 0, 0))],
        out_specs=[pl.BlockSpec((1, 4, HEAD_DIM, tq), lambda bi, h, i: (bi, h, 0, i))] * 2,
        out_shape=[out, out],
        scratch_shapes=(_stream_scratch(tk, tq, 4, FLASH_BUFFERS)
                        + _stream_scratch(tq + 2 * WINDOW, tq, 4, WINDOW_BUFFERS)),
        compiler_params=_params(("arbitrary", "arbitrary", "arbitrary")),
        name="layer0_attention",
    )(q, q, k, vt, k, vt, sink_rows)


def _diff_kernel(qa_ref, qb_ref, k_ref, vt_ref, lam_ref, subw_ref, o_ref, *scratch, tk, lam_init):
    tb = k_ref.shape[2]
    d_v = 2 * HEAD_DIM
    chunks = _key_chunks(tb, tk)
    bufs = _split_scratch(scratch, 4, FLASH_BUFFERS)
    streams = []
    lane = lax.broadcasted_iota(jnp.int32, qa_ref.shape[1:], 1)
    for a in range(2):
        for q_ref in (qa_ref, qb_ref):
            qh = jnp.where(lane // HEAD_DIM == a, q_ref[0], jnp.zeros_like(q_ref[0]))
            streams.append(_Stream(chunks, lambda lo, size, tag, qh=qh: _nt_dot(k_ref[0, 0, lo:lo + size, :], qh),
                                   lambda lo, size: vt_ref[0, 0, :, lo:lo + size], *bufs[len(streams)]))
    _run_streams(streams)
    ots = [st.acc[:d_v, :] / st.acc[d_v:d_v + 1, :] for st in streams]
    ots = [jnp.concatenate(ots[0:2], axis=1), jnp.concatenate(ots[2:4], axis=1)]
    lam = (jnp.exp(jnp.sum(lam_ref[0:1, :] * lam_ref[1:2, :], axis=-1, keepdims=True))
           - jnp.exp(jnp.sum(lam_ref[2:3, :] * lam_ref[3:4, :], axis=-1, keepdims=True)) + lam_init)
    o = ots[0] - lam * ots[1]
    ms = jnp.mean(o * o, axis=0, keepdims=True)
    o = (o * lax.rsqrt(ms + EPS) * subw_ref[...]) * (1.0 - lam_init)
    o_ref[0, 0] = o.astype(BF16)


def _diff_attention(q, k, vt, lam_vecs, subw, *, tq, tk, lam_init):
    b, tb, _ = q.shape
    n_lat = tb - CTX_LEN
    n_heads = vt.shape[1]
    d_v = vt.shape[2] - SUM_ROWS
    hq = tq // 2
    assert CTX_LEN % hq == 0
    q_off = CTX_LEN // hq
    return pl.pallas_call(
        functools.partial(_diff_kernel, tk=tk, lam_init=lam_init),
        grid=(b, n_heads, n_lat // tq),
        in_specs=[
            pl.BlockSpec((1, hq, 2 * HEAD_DIM), lambda bi, h, i: (bi, 2 * i + q_off, h)),
            pl.BlockSpec((1, hq, 2 * HEAD_DIM), lambda bi, h, i: (bi, 2 * i + 1 + q_off, h)),
            pl.BlockSpec((1, 1, tb, 2 * HEAD_DIM), lambda bi, h, i: (bi, h, 0, 0)),
            pl.BlockSpec((1, 1, d_v + SUM_ROWS, tb), lambda bi, h, i: (bi, h, 0, 0)),
            pl.BlockSpec((4, HEAD_DIM), lambda bi, h, i: (0, 0)),
            pl.BlockSpec((d_v, 1), lambda bi, h, i: (0, 0)),
        ],
        out_specs=pl.BlockSpec((1, 1, d_v, tq), lambda bi, h, i: (bi, h, 0, i)),
        out_shape=jax.ShapeDtypeStruct((b, n_heads, d_v, n_lat), BF16),
        scratch_shapes=_stream_scratch(tk, tq // 2, 4, FLASH_BUFFERS),
        compiler_params=_params(("arbitrary", "arbitrary", "arbitrary")),
        name="diff_attention",
    )(q, q, k, vt, lam_vecs, subw)


def _post_kernel(*refs, n_attn):
    n_x = len(refs) - n_attn - 6
    x_refs, mod_ref = refs[:n_x], refs[n_x]
    o_refs = refs[n_x + 1:n_x + 1 + n_attn]
    wo_ref, nw_ref, wi_ref, wf_ref, out_ref = refs[n_x + 1 + n_attn:]
    mod = mod_ref[0, 0]
    kw = wo_ref.shape[0] // n_attn
    a = None
    for j, o_ref in enumerate(o_refs):
        ot = o_ref[0].reshape(kw, o_ref.shape[3])
        part = lax.dot_general(ot, wo_ref[kw * j:kw * (j + 1), :], (((0,), (0,)), ((), ())),
                               preferred_element_type=F32)
        a = part if a is None else a + part
    x1 = _stream_tile(x_refs) + mod[2:3, :] * a
    h = _modulated_norm(x1, nw_ref[...], mod[3:4, :], mod[4:5, :])
    u = _dot(h.astype(BF16), wi_ref[...])
    f = wf_ref.shape[0]
    gate = u[:, :f]
    act = (gate / (1.0 + jnp.exp(-gate))) * u[:, f:]
    y = _dot(act.astype(BF16), wf_ref[...])
    out_ref[0] = x1 + mod[5:6, :] * y


def _post(xs, mod_tab, attn_outs, wo, nw, wi, wf, *, latent_only):
    b, d = xs[0].shape[0], xs[0].shape[2]
    tb = sum(a.shape[1] for a in xs)
    tm = ROW_TILE
    off = CTX_LEN // tm if latent_only else 0
    n_rows = tb - CTX_LEN if latent_only else tb
    n_attn = len(attn_outs)

    def const(shape):
        return pl.BlockSpec(shape, lambda bi, i: (0,) * len(shape))

    in_specs = _stream_specs(xs, tm, off) + [
        pl.BlockSpec((1, 1, MOD_ROWS, d), lambda bi, i: (bi, jnp.minimum(i + off, 1), 0, 0)),
    ]
    for o in attn_outs:
        in_specs.append(pl.BlockSpec((1, o.shape[1], o.shape[2], tm), lambda bi, i: (bi, 0, 0, i)))
    in_specs += [const(wo.shape), const(nw.shape), const(wi.shape), const(wf.shape)]
    return pl.pallas_call(
        functools.partial(_post_kernel, n_attn=n_attn),
        grid=(b, n_rows // tm),
        in_specs=in_specs,
        out_specs=pl.BlockSpec((1, tm, d), lambda bi, i: (bi, i, 0)),
        out_shape=jax.ShapeDtypeStruct((b, n_rows, d), F32),
        compiler_params=_params(("arbitrary", "arbitrary")),
        name="post",
    )(*xs, mod_tab, *attn_outs, wo, nw, wi, wf)


def _deinterleave_perm(n_heads):
    one = jnp.concatenate([jnp.arange(0, HEAD_DIM, 2), jnp.arange(1, HEAD_DIM, 2)])
    return (jnp.arange(n_heads)[:, None] * HEAD_DIM + one[None, :]).reshape(-1)


def _rope_tables(n_lat):
    rows = n_lat // GRID_W
    row = jnp.repeat(jnp.arange(rows, dtype=F32), GRID_W)
    col = jnp.tile(jnp.arange(GRID_W, dtype=F32), rows)
    n_freq = HEAD_DIM // 4
    inv = ROPE_THETA ** (-jnp.arange(n_freq, dtype=F32) / n_freq)
    ang = jnp.concatenate([row[:, None] * inv, col[:, None] * inv], axis=-1)
    cos, sin = jnp.cos(ang), jnp.sin(ang)
    cos = jnp.concatenate([jnp.ones((CTX_LEN, HEAD_DIM // 2), F32), cos], axis=0)
    sin = jnp.concatenate([jnp.zeros((CTX_LEN, HEAD_DIM // 2), F32), sin], axis=0)
    cos_t = jnp.tile(jnp.concatenate([cos, cos], axis=-1), (1, 2))
    sin_t = jnp.tile(jnp.concatenate([-sin, sin], axis=-1), (1, 2))
    return cos_t, sin_t


def _gain_row(parts):
    one = _deinterleave_perm(1)
    return jnp.concatenate([jnp.tile(g[one], n) for g, n in parts])[None, :].astype(F32)


def kernel(x, c, ctx, c_ctx, mod_w, mod_b, norm_mix_w, norm_ffn_w, ev_w_in, ev_w_out, ev_qn_a, ev_kn_a,
           ev_qn_b, ev_kn_b, ev_sink_a, od_w_in, od_w_out, od_qn, od_kn, od_lq1, od_lk1, od_lq2, od_lk2,
           od_subln, ffn_w_in, ffn_w_out):
    b, n_lat, d = x.shape
    assert d == D_MODEL and ctx.shape[1] == CTX_LEN and b < MOD_ROWS
    assert n_lat % (DIFF_QUERY_TILES * QUERY_TILE) == 0 and n_lat % KEY_CHUNK == 0 and n_lat % GRID_W == 0
    hd = HEAD_DIM

    cc = jnp.zeros((MOD_ROWS, d), F32).at[:b].set(c).at[b].set(c_ctx)
    mod = _modulation(cc, mod_w, mod_b).reshape(DEPTH, MOD_ROWS, 6, d)
    mod_lat = mod[:, :b]
    mod_ctx = jnp.broadcast_to(mod[:, b:b + 1], mod_lat.shape)
    mod_tab = jnp.stack([mod_ctx, mod_lat], axis=2)
    mod_tab = jnp.pad(mod_tab, ((0, 0), (0, 0), (0, 0), (0, MOD_ROWS - 6), (0, 0)))

    cos_t, sin_t = _rope_tables(n_lat)
    bd = jnp.kron(jnp.eye(MXU_TILE // hd, dtype=F32), jnp.ones((hd, hd), F32)).astype(BF16)

    w = ev_w_in[0]
    widths = [8 * hd, 2 * hd, 2 * hd, 8 * hd, 2 * hd, 2 * hd]
    qa, ka, va, qb, kb, vb = jnp.split(w, [sum(widths[:n]) for n in range(1, 6)], axis=1)
    p8, p2 = _deinterleave_perm(8), _deinterleave_perm(2)
    w0 = jnp.concatenate([qa[:, p8], qb[:, p8], ka[:, p2], kb[:, p2]], axis=1).astype(BF16)
    wvt0 = jnp.concatenate([va, vb], axis=1).T.astype(BF16)
    g0 = _gain_row([(ev_qn_a[0], 8), (ev_qn_b[0], 8), (ev_kn_a[0], 2), (ev_kn_b[0], 2)])
    q0, k0, vt0 = _inproj((ctx, x), mod_tab[0], norm_mix_w[0][None, :], w0, wvt0, g0, cos_t, sin_t, bd,
                          n_k=4, n_v=4, d_v=hd)
    sink_rows = jnp.repeat(ev_sink_a[0].astype(F32) * LOG2E, QUERY_TILE).reshape(2, 1, 4 * QUERY_TILE)
    o_a, o_b = _layer0_attention(q0, k0, vt0, sink_rows, tq=QUERY_TILE, tk=KEY_CHUNK)
    xs = _post((ctx, x), mod_tab[0], [o_a, o_b], ev_w_out[0].astype(BF16), norm_ffn_w[0][None, :],
               ffn_w_in[0].astype(BF16), ffn_w_out[0].astype(BF16), latent_only=False)

    w = od_w_in[0]
    p16 = _deinterleave_perm(16)
    w1 = jnp.concatenate([w[:, :1024][:, p16], w[:, 1024:2048][:, p16]], axis=1).astype(BF16)
    wvt1 = w[:, 2048:].T.astype(BF16)
    g1 = _gain_row([(od_qn[0], 16), (od_kn[0], 16)])
    q1, k1, vt1 = _inproj((xs,), mod_tab[1], norm_mix_w[1][None, :], w1, wvt1, g1, cos_t, sin_t, bd,
                          n_k=16, n_v=8, d_v=2 * hd, k_pairs=True)
    lam_init = 0.8 - 0.6 * math.exp(-0.3 * 1)
    lam_vecs = jnp.stack([od_lq1[0], od_lk1[0], od_lq2[0], od_lk2[0]]).astype(F32)
    o_c = _diff_attention(q1, k1, vt1, lam_vecs, od_subln[0].astype(F32)[:, None],
                          tq=DIFF_QUERY_TILES * QUERY_TILE, tk=KEY_CHUNK, lam_init=lam_init)
    return _post((xs,), mod_tab[1], [o_c], od_w_out[0].astype(BF16), norm_ffn_w[1][None, :],
                 ffn_w_in[1].astype(BF16), ffn_w_out[1].astype(BF16), latent_only=True)
```

```python
import functools
import math

import jax
import jax.numpy as jnp
from jax import lax
from jax.experimental import pallas as pl
from jax.experimental.pallas import tpu as pltpu

LANES = 128
MXU_TILE = 256
D_MODEL = 1024
HEAD_DIM = 64
CTX_LEN = 256
GRID_W = 64
WINDOW = 128
DEPTH = 2
ROPE_THETA = 10000.0
EPS = 1e-6
NEG_INF = -1e30
LOG2E = 1.4426950408889634
Q_SCALE = HEAD_DIM ** -0.5 * LOG2E
MOD_ROWS = 8
ROW_TILE = 256
MOD_COL_TILE = 1536
KEY_CHUNK = 256
QUERY_TILE = 256
DIFF_QUERY_TILES = 2
FLASH_BUFFERS = 4
WINDOW_BUFFERS = 2
SOFTMAX_ROWS = 64
SUM_ROWS = 16
VMEM_LIMIT = 56 * 1024 * 1024

F32 = jnp.float32
BF16 = jnp.bfloat16


def _nt_dot(a, b):
    return lax.dot_general(a, b, (((1,), (1,)), ((), ())), preferred_element_type=F32)


def _dot(a, b):
    return jnp.dot(a, b, preferred_element_type=F32)


def _params(sem):
    return pltpu.CompilerParams(dimension_semantics=sem, vmem_limit_bytes=VMEM_LIMIT)


def _mod_kernel(cc_ref, w_ref, b_ref, o_ref):
    a = cc_ref[...]
    a = a / (1.0 + jnp.exp(-a))
    o_ref[0] = _dot(a.astype(BF16), w_ref[0].astype(BF16)) + b_ref[0]


def _modulation(cc, mod_w, mod_b):
    depth, d, n = mod_w.shape
    tn = MOD_COL_TILE
    return pl.pallas_call(
        _mod_kernel,
        grid=(depth, n // tn),
        in_specs=[
            pl.BlockSpec((MOD_ROWS, d), lambda l, j: (0, 0)),
            pl.BlockSpec((1, d, tn), lambda l, j: (l, 0, j)),
            pl.BlockSpec((1, 1, tn), lambda l, j: (l, 0, j)),
        ],
        out_specs=pl.BlockSpec((1, MOD_ROWS, tn), lambda l, j: (l, 0, j)),
        out_shape=jax.ShapeDtypeStruct((depth, MOD_ROWS, n), F32),
        compiler_params=_params(("arbitrary", "arbitrary")),
        name="modulation",
    )(cc, mod_w, mod_b.reshape(depth, 1, n))


def _modulated_norm(x, nw, shift, scale):
    ms = jnp.mean(x * x, axis=-1, keepdims=True)
    return (x * lax.rsqrt(ms + EPS) * nw) * (1.0 + scale) + shift


def _stream_specs(xs, tm, off=0):
    if len(xs) == 1:
        return [pl.BlockSpec((1, tm, xs[0].shape[2]), lambda bi, i: (bi, i + off, 0))]
    assert tm == CTX_LEN and off == 0
    d = xs[0].shape[2]
    return [pl.BlockSpec((1, tm, d), lambda bi, i: (bi, 0, 0)),
            pl.BlockSpec((1, tm, d), lambda bi, i: (bi, jnp.maximum(i - 1, 0), 0))]


def _stream_tile(x_refs):
    if len(x_refs) == 1:
        return x_refs[0][0]
    return jnp.where(pl.program_id(1) == 0, x_refs[0][0], x_refs[1][0])


def _inproj_kernel(*refs, n_q, n_k, n_v, d_v, k_pairs):
    x_refs = refs[:-11]
    mod_ref, nw_ref, w_ref, wvt_ref, g_ref, cos_ref, sin_ref, bd_ref, q_ref, k_ref, vt_ref = refs[-11:]
    tm = q_ref.shape[1]
    h = _modulated_norm(_stream_tile(x_refs), nw_ref[...], mod_ref[0, 0, 0:1, :], mod_ref[0, 0, 1:2, :])
    hb = h.astype(BF16)
    y = _dot(hb, w_ref[...])
    cos = cos_ref[...]
    sin = sin_ref[...]
    lane = lax.broadcasted_iota(jnp.int32, (tm, LANES), 1)
    first_half = (lane % HEAD_DIM) < (HEAD_DIM // 2)
    n_norm = n_q + n_k * HEAD_DIM
    for c in range(n_norm // MXU_TILE):
        yc = y[:, MXU_TILE * c:MXU_TILE * (c + 1)]
        ss = _dot((yc * yc).astype(BF16), bd_ref[...])
        z = yc * lax.rsqrt(ss * (1.0 / HEAD_DIM) + EPS) * g_ref[:, MXU_TILE * c:MXU_TILE * (c + 1)]
        for half in range(MXU_TILE // LANES):
            zc = z[:, LANES * half:LANES * (half + 1)]
            partner = jnp.where(first_half, pltpu.roll(zc, LANES - HEAD_DIM // 2, 1),
                                pltpu.roll(zc, HEAD_DIM // 2, 1))
            o = zc * cos + partner * sin
            col = MXU_TILE * c + LANES * half
            if col < n_q:
                q_ref[0, :, col:col + LANES] = (o * Q_SCALE).astype(BF16)
            else:
                kh = (col - n_q) // HEAD_DIM
                if k_pairs:
                    k_ref[0, kh // 2] = o.astype(BF16)
                else:
                    k_ref[0, kh] = o[:, :HEAD_DIM].astype(BF16)
                    k_ref[0, kh + 1] = o[:, HEAD_DIM:].astype(BF16)
    vt = _nt_dot(wvt_ref[...], hb).astype(BF16)
    ones_rows = (lax.broadcasted_iota(jnp.int32, (SUM_ROWS, tm), 0) == 0).astype(BF16)
    for hv in range(n_v):
        vt_ref[0, hv, 0:d_v, :] = vt[d_v * hv:d_v * (hv + 1), :]
        vt_ref[0, hv, d_v:d_v + SUM_ROWS, :] = ones_rows


def _inproj(xs, mod_tab, nw, w, wvt, gains, cos_t, sin_t, bd, *, n_k, n_v, d_v, k_pairs=False):
    b, d = xs[0].shape[0], xs[0].shape[2]
    tb = sum(a.shape[1] for a in xs)
    n_q = D_MODEL
    n_in = w.shape[1]
    tm = ROW_TILE
    kern = functools.partial(_inproj_kernel, n_q=n_q, n_k=n_k, n_v=n_v, d_v=d_v, k_pairs=k_pairs)
    k_blocks, k_width = (n_k // 2, 2 * HEAD_DIM) if k_pairs else (n_k, HEAD_DIM)
    return pl.pallas_call(
        kern,
        grid=(b, tb // tm),
        in_specs=_stream_specs(xs, tm) + [
            pl.BlockSpec((1, 1, MOD_ROWS, d), lambda bi, i: (bi, jnp.minimum(i, 1), 0, 0)),
            pl.BlockSpec((1, d), lambda bi, i: (0, 0)),
            pl.BlockSpec((d, n_in), lambda bi, i: (0, 0)),
            pl.BlockSpec((n_v * d_v, d), lambda bi, i: (0, 0)),
            pl.BlockSpec((1, gains.shape[1]), lambda bi, i: (0, 0)),
            pl.BlockSpec((tm, LANES), lambda bi, i: (i, 0)),
            pl.BlockSpec((tm, LANES), lambda bi, i: (i, 0)),
            pl.BlockSpec((MXU_TILE, MXU_TILE), lambda bi, i: (0, 0)),
        ],
        out_specs=[
            pl.BlockSpec((1, tm, n_q), lambda bi, i: (bi, i, 0)),
            pl.BlockSpec((1, k_blocks, tm, k_width), lambda bi, i: (bi, 0, i, 0)),
            pl.BlockSpec((1, n_v, d_v + SUM_ROWS, tm), lambda bi, i: (bi, 0, 0, i)),
        ],
        out_shape=[
            jax.ShapeDtypeStruct((b, tb, n_q), BF16),
            jax.ShapeDtypeStruct((b, k_blocks, tb, k_width), BF16),
            jax.ShapeDtypeStruct((b, n_v, d_v + SUM_ROWS, tb), BF16),
        ],
        compiler_params=_params(("arbitrary", "arbitrary")),
        name="inproj",
    )(*xs, mod_tab, nw, w, wvt, gains, cos_t, sin_t, bd)


def _softmax_step(s_ref, p_ref, m, chunk_max):
    m_new = chunk_max if m is None else jnp.maximum(m, chunk_max)
    for r in range(0, s_ref.shape[0], SOFTMAX_ROWS):
        p_ref[r:r + SOFTMAX_ROWS, :] = jnp.exp2(s_ref[r:r + SOFTMAX_ROWS, :] - m_new).astype(BF16)
    return m_new, (None if m is None else jnp.exp2(m - m_new))


class _Stream:
    def __init__(self, chunks, score, vt_slice, s_bufs, p_bufs):
        self.chunks, self.score, self.vt_slice = chunks, score, vt_slice
        self.s_bufs, self.p_bufs = s_bufs, p_bufs
        self.m = self.acc = None
        self.cmax = {}

    def buf(self, refs, c):
        return refs[c % len(refs)].at[0:self.chunks[c][1], :]

    def write_scores(self, c):
        if c < len(self.chunks):
            s = self.score(*self.chunks[c])
            self.buf(self.s_bufs, c)[...] = s
            self.cmax[c] = jnp.max(s, axis=0, keepdims=True)

    def step(self, c):
        if c >= len(self.chunks):
            return
        self.write_scores(c + 2)
        lo, size, _ = self.chunks[c]
        p_ref = self.buf(self.p_bufs, c)
        self.m, alpha = _softmax_step(self.buf(self.s_bufs, c), p_ref, self.m, self.cmax.pop(c))
        pv = _dot(self.vt_slice(lo, size), p_ref[...])
        self.acc = pv if alpha is None else alpha * self.acc + pv


def _run_streams(streams):
    for c in range(2):
        for st in streams:
            st.write_scores(c)
    for c in range(max(len(st.chunks) for st in streams)):
        for st in streams:
            st.step(c)


def _key_chunks(n_keys, tk):
    return [(0, CTX_LEN, "ctx")] + [(lo, tk, "lat") for lo in range(CTX_LEN, n_keys, tk)]


def _stream_scratch(n_keys, nq, n_streams, n_bufs):
    n = n_streams * n_bufs
    return [pltpu.VMEM((n_keys, nq), F32)] * n + [pltpu.VMEM((n_keys, nq), BF16)] * n


def _split_scratch(scratch, n_streams, n_bufs):
    n = n_streams * n_bufs
    return [(scratch[j * n_bufs:(j + 1) * n_bufs], scratch[n + j * n_bufs:n + (j + 1) * n_bufs])
            for j in range(n_streams)]


def _layer0_attention_kernel(qa_ref, qb_ref, ka_ref, vta_ref, kb_ref, vtb_ref, sink_ref, oa_ref, ob_ref,
                             *scratch, tk):
    tq = qa_ref.shape[1]
    h = pl.program_id(1)
    tb = ka_ref.shape[2]
    span = tq + 2 * WINDOW
    t = pl.program_id(2)
    n_dense = 2 * 4 * FLASH_BUFFERS
    dense_bufs = _split_scratch(scratch[:n_dense], 4, FLASH_BUFFERS)
    window_bufs = _split_scratch(scratch[n_dense:], 4, WINDOW_BUFFERS)

    def run(chunks):
        qa = qa_ref[0]
        ws = pl.multiple_of(jnp.clip(tq * t - WINDOW, 0, tb - span), LANES)
        kpos = ws - CTX_LEN + lax.broadcasted_iota(jnp.int32, (span, tq), 0)
        qpos = tq * t - CTX_LEN + lax.broadcasted_iota(jnp.int32, (span, tq), 1)
        ok = (jnp.abs(qpos - kpos) <= WINDOW) & (kpos >= 0) & (qpos >= 0)

        lane = lax.broadcasted_iota(jnp.int32, (tq, LANES), 1)

        def own_head(q_tile):
            return jnp.where(lane // HEAD_DIM == h, q_tile, jnp.zeros_like(q_tile))

        def window_score(g):
            qh = own_head(qa[:, LANES * g:LANES * (g + 1)])

            def score(lo, size, tag):
                s = _nt_dot(ka_ref[0, 0, pl.ds(lo, size), :], qh)
                return jnp.where(ok, s, NEG_INF) if tag == "win" else s
            return score

        win = [_Stream([(0, CTX_LEN, "ctx"), (ws, span, "win")], window_score(g),
                       lambda lo, size: vta_ref[0, 0, :, pl.ds(lo, size)], *window_bufs[g])
               for g in range(4)]

        qb = qb_ref[0]
        qts = [own_head(qb[:, LANES * g:LANES * (g + 1)]).astype(F32).T.astype(BF16) for g in range(4)]
        dense = [_Stream(chunks, lambda lo, size, tag, qt=qt: _dot(kb_ref[0, 0, lo:lo + size, :], qt),
                         lambda lo, size: vtb_ref[0, 0, :, lo:lo + size], *dense_bufs[g])
                 for g, qt in enumerate(qts)]

        _run_streams(win + dense)
        for g, st in enumerate(win):
            sink = sink_ref[0, :, tq * g:tq * (g + 1)]
            m_all = jnp.maximum(st.m, sink)
            scale = jnp.exp2(st.m - m_all)
            l = st.acc[HEAD_DIM:HEAD_DIM + 1, :] * scale + jnp.exp2(sink - m_all)
            oa_ref[0, g] = (st.acc[:HEAD_DIM, :] * scale / l).astype(BF16)
        for g, st in enumerate(dense):
            ob_ref[0, g] = (st.acc[:HEAD_DIM, :] / st.acc[HEAD_DIM:HEAD_DIM + 1, :]).astype(BF16)

    chunks = _key_chunks(tb, tk)
    is_latent = t * tq >= CTX_LEN
    pl.when(is_latent)(lambda: run(chunks))
    pl.when(jnp.logical_not(is_latent))(lambda: run(chunks[:1]))


def _layer0_attention(q, k, vt, sink_rows, *, tq, tk):
    b, tb, _ = q.shape
    assert (tb - CTX_LEN) % tk == 0

    def kv_specs(kind):
        return [pl.BlockSpec((1, 1, tb, 2 * HEAD_DIM), lambda bi, h, i: (bi, kind, 0, 0)),
                pl.BlockSpec((1, 1, HEAD_DIM + SUM_ROWS, tb), lambda bi, h, i: (bi, 2 * kind + h, 0, 0))]

    out = jax.ShapeDtypeStruct((b, 8, HEAD_DIM, tb), BF16)
    return pl.pallas_call(
        functools.partial(_layer0_attention_kernel, tk=tk),
        grid=(b, 2, tb // tq),
        in_specs=[pl.BlockSpec((1, tq, 8 * HEAD_DIM), lambda bi, h, i: (bi, i, 0)),
                  pl.BlockSpec((1, tq, 8 * HEAD_DIM), lambda bi, h, i: (bi, i, 1))]
        + kv_specs(0) + kv_specs(1) + [pl.BlockSpec((1, 1, 4 * tq), lambda bi, h, i: (h, 0, 0))],
        out_specs=[pl.BlockSpec((1, 4, HEAD_DIM, tq), lambda bi, h, i: (bi, h, 0, i))] * 2,
        out_shape=[out, out],
        scratch_shapes=(_stream_scratch(tk, tq, 4, FLASH_BUFFERS)
                        + _stream_scratch(tq + 2 * WINDOW, tq, 4, WINDOW_BUFFERS)),
        compiler_params=_params(("arbitrary", "arbitrary", "arbitrary")),
        name="layer0_attention",
    )(q, q, k, vt, k, vt, sink_rows)


def _diff_kernel(qa_ref, qb_ref, k_ref, vt_ref, lam_ref, subw_ref, o_ref, *scratch, tk, lam_init):
    tb = k_ref.shape[2]
    d_v = 2 * HEAD_DIM
    chunks = _key_chunks(tb, tk)
    bufs = _split_scratch(scratch, 4, FLASH_BUFFERS)
    streams = []
    lane = lax.broadcasted_iota(jnp.int32, qa_ref.shape[1:], 1)
    for a in range(2):
        for q_ref in (qa_ref, qb_ref):
            qh = jnp.where(lane // HEAD_DIM == a, q_ref[0], jnp.zeros_like(q_ref[0]))
            streams.append(_Stream(chunks, lambda lo, size, tag, qh=qh: _nt_dot(k_ref[0, 0, lo:lo + size, :], qh),
                                   lambda lo, size: vt_ref[0, 0, :, lo:lo + size], *bufs[len(streams)]))
    _run_streams(streams)
    ots = [st.acc[:d_v, :] / st.acc[d_v:d_v + 1, :] for st in streams]
    ots = [jnp.concatenate(ots[0:2], axis=1), jnp.concatenate(ots[2:4], axis=1)]
    lam = (jnp.exp(jnp.sum(lam_ref[0:1, :] * lam_ref[1:2, :], axis=-1, keepdims=True))
           - jnp.exp(jnp.sum(lam_ref[2:3, :] * lam_ref[3:4, :], axis=-1, keepdims=True)) + lam_init)
    o = ots[0] - lam * ots[1]
    ms = jnp.mean(o * o, axis=0, keepdims=True)
    o = (o * lax.rsqrt(ms + EPS) * subw_ref[...]) * (1.0 - lam_init)
    o_ref[0, 0] = o.astype(BF16)


def _diff_attention(q, k, vt, lam_vecs, subw, *, tq, tk, lam_init):
    b, tb, _ = q.shape
    n_lat = tb - CTX_LEN
    n_heads = vt.shape[1]
    d_v = vt.shape[2] - SUM_ROWS
    hq = tq // 2
    assert CTX_LEN % hq == 0
    q_off = CTX_LEN // hq
    return pl.pallas_call(
        functools.partial(_diff_kernel, tk=tk, lam_init=lam_init),
        grid=(b, n_heads, n_lat // tq),
        in_specs=[
            pl.BlockSpec((1, hq, 2 * HEAD_DIM), lambda bi, h, i: (bi, 2 * i + q_off, h)),
            pl.BlockSpec((1, hq, 2 * HEAD_DIM), lambda bi, h, i: (bi, 2 * i + 1 + q_off, h)),
            pl.BlockSpec((1, 1, tb, 2 * HEAD_DIM), lambda bi, h, i: (bi, h, 0, 0)),
            pl.BlockSpec((1, 1, d_v + SUM_ROWS, tb), lambda bi, h, i: (bi, h, 0, 0)),
            pl.BlockSpec((4, HEAD_DIM), lambda bi, h, i: (0, 0)),
            pl.BlockSpec((d_v, 1), lambda bi, h, i: (0, 0)),
        ],
        out_specs=pl.BlockSpec((1, 1, d_v, tq), lambda bi, h, i: (bi, h, 0, i)),
        out_shape=jax.ShapeDtypeStruct((b, n_heads, d_v, n_lat), BF16),
        scratch_shapes=_stream_scratch(tk, tq // 2, 4, FLASH_BUFFERS),
        compiler_params=_params(("arbitrary", "arbitrary", "arbitrary")),
        name="diff_attention",
    )(q, q, k, vt, lam_vecs, subw)


def _post_kernel(*refs, n_attn):
    n_x = len(refs) - n_attn - 6
    x_refs, mod_ref = refs[:n_x], refs[n_x]
    o_refs = refs[n_x + 1:n_x + 1 + n_attn]
    wo_ref, nw_ref, wi_ref, wf_ref, out_ref = refs[n_x + 1 + n_attn:]
    mod = mod_ref[0, 0]
    kw = wo_ref.shape[0] // n_attn
    a = None
    for j, o_ref in enumerate(o_refs):
        ot = o_ref[0].reshape(kw, o_ref.shape[3])
        part = lax.dot_general(ot, wo_ref[kw * j:kw * (j + 1), :], (((0,), (0,)), ((), ())),
                               preferred_element_type=F32)
        a = part if a is None else a + part
    x1 = _stream_tile(x_refs) + mod[2:3, :] * a
    h = _modulated_norm(x1, nw_ref[...], mod[3:4, :], mod[4:5, :])
    u = _dot(h.astype(BF16), wi_ref[...])
    f = wf_ref.shape[0]
    gate = u[:, :f]
    act = (gate / (1.0 + jnp.exp(-gate))) * u[:, f:]
    y = _dot(act.astype(BF16), wf_ref[...])
    out_ref[0] = x1 + mod[5:6, :] * y


def _post(xs, mod_tab, attn_outs, wo, nw, wi, wf, *, latent_only):
    b, d = xs[0].shape[0], xs[0].shape[2]
    tb = sum(a.shape[1] for a in xs)
    tm = ROW_TILE
    off = CTX_LEN // tm if latent_only else 0
    n_rows = tb - CTX_LEN if latent_only else tb
    n_attn = len(attn_outs)

    def const(shape):
        return pl.BlockSpec(shape, lambda bi, i: (0,) * len(shape))

    in_specs = _stream_specs(xs, tm, off) + [
        pl.BlockSpec((1, 1, MOD_ROWS, d), lambda bi, i: (bi, jnp.minimum(i + off, 1), 0, 0)),
    ]
    for o in attn_outs:
        in_specs.append(pl.BlockSpec((1, o.shape[1], o.shape[2], tm), lambda bi, i: (bi, 0, 0, i)))
    in_specs += [const(wo.shape), const(nw.shape), const(wi.shape), const(wf.shape)]
    return pl.pallas_call(
        functools.partial(_post_kernel, n_attn=n_attn),
        grid=(b, n_rows // tm),
        in_specs=in_specs,
        out_specs=pl.BlockSpec((1, tm, d), lambda bi, i: (bi, i, 0)),
        out_shape=jax.ShapeDtypeStruct((b, n_rows, d), F32),
        compiler_params=_params(("arbitrary", "arbitrary")),
        name="post",
    )(*xs, mod_tab, *attn_outs, wo, nw, wi, wf)


def _deinterleave_perm(n_heads):
    one = jnp.concatenate([jnp.arange(0, HEAD_DIM, 2), jnp.arange(1, HEAD_DIM, 2)])
    return (jnp.arange(n_heads)[:, None] * HEAD_DIM + one[None, :]).reshape(-1)


def _rope_tables(n_lat):
    rows = n_lat // GRID_W
    row = jnp.repeat(jnp.arange(rows, dtype=F32), GRID_W)
    col = jnp.tile(jnp.arange(GRID_W, dtype=F32), rows)
    n_freq = HEAD_DIM // 4
    inv = ROPE_THETA ** (-jnp.arange(n_freq, dtype=F32) / n_freq)
    ang = jnp.concatenate([row[:, None] * inv, col[:, None] * inv], axis=-1)
    cos, sin = jnp.cos(ang), jnp.sin(ang)
    cos = jnp.concatenate([jnp.ones((CTX_LEN, HEAD_DIM // 2), F32), cos], axis=0)
    sin = jnp.concatenate([jnp.zeros((CTX_LEN, HEAD_DIM // 2), F32), sin], axis=0)
    cos_t = jnp.tile(jnp.concatenate([cos, cos], axis=-1), (1, 2))
    sin_t = jnp.tile(jnp.concatenate([-sin, sin], axis=-1), (1, 2))
    return cos_t, sin_t


def _gain_row(parts):
    one = _deinterleave_perm(1)
    return jnp.concatenate([jnp.tile(g[one], n) for g, n in parts])[None, :].astype(F32)


def kernel(x, c, ctx, c_ctx, mod_w, mod_b, norm_mix_w, norm_ffn_w, ev_w_in, ev_w_out, ev_qn_a, ev_kn_a,
           ev_qn_b, ev_kn_b, ev_sink_a, od_w_in, od_w_out, od_qn, od_kn, od_lq1, od_lk1, od_lq2, od_lk2,
           od_subln, ffn_w_in, ffn_w_out):
    b, n_lat, d = x.shape
    assert d == D_MODEL and ctx.shape[1] == CTX_LEN and b < MOD_ROWS
    assert n_lat % (DIFF_QUERY_TILES * QUERY_TILE) == 0 and n_lat % KEY_CHUNK == 0 and n_lat % GRID_W == 0
    hd = HEAD_DIM

    cc = jnp.zeros((MOD_ROWS, d), F32).at[:b].set(c).at[b].set(c_ctx)
    mod = _modulation(cc, mod_w, mod_b).reshape(DEPTH, MOD_ROWS, 6, d)
    mod_lat = mod[:, :b]
    mod_ctx = jnp.broadcast_to(mod[:, b:b + 1], mod_lat.shape)
    mod_tab = jnp.stack([mod_ctx, mod_lat], axis=2)
    mod_tab = jnp.pad(mod_tab, ((0, 0), (0, 0), (0, 0), (0, MOD_ROWS - 6), (0, 0)))

    cos_t, sin_t = _rope_tables(n_lat)
    bd = jnp.kron(jnp.eye(MXU_TILE // hd, dtype=F32), jnp.ones((hd, hd), F32)).astype(BF16)

    w = ev_w_in[0]
    widths = [8 * hd, 2 * hd, 2 * hd, 8 * hd, 2 * hd, 2 * hd]
    qa, ka, va, qb, kb, vb = jnp.split(w, [sum(widths[:n]) for n in range(1, 6)], axis=1)
    pair_order = jnp.array([0, 4, 1, 5, 2, 6, 3, 7])
    p8 = _deinterleave_perm(8).reshape(8, hd)[pair_order].reshape(-1)
    p2 = _deinterleave_perm(2)
    w0 = jnp.concatenate([qa[:, p8], qb[:, p8], ka[:, p2], kb[:, p2]], axis=1).astype(BF16)
    wvt0 = jnp.concatenate([va, vb], axis=1).T.astype(BF16)
    g0 = _gain_row([(ev_qn_a[0], 8), (ev_qn_b[0], 8), (ev_kn_a[0], 2), (ev_kn_b[0], 2)])
    q0, k0, vt0 = _inproj((ctx, x), mod_tab[0], norm_mix_w[0][None, :], w0, wvt0, g0, cos_t, sin_t, bd,
                          n_k=4, n_v=4, d_v=hd, k_pairs=True)
    sink_rows = jnp.repeat(ev_sink_a[0].astype(F32) * LOG2E, QUERY_TILE).reshape(2, 1, 4 * QUERY_TILE)
    o_a, o_b = _layer0_attention(q0, k0, vt0, sink_rows, tq=QUERY_TILE, tk=KEY_CHUNK)
    xs = _post((ctx, x), mod_tab[0], [o_a, o_b], ev_w_out[0].astype(BF16), norm_ffn_w[0][None, :],
               ffn_w_in[0].astype(BF16), ffn_w_out[0].astype(BF16), latent_only=False)

    w = od_w_in[0]
    p16 = _deinterleave_perm(16)
    w1 = jnp.concatenate([w[:, :1024][:, p16], w[:, 1024:2048][:, p16]], axis=1).astype(BF16)
    wvt1 = w[:, 2048:].T.astype(BF16)
    g1 = _gain_row([(od_qn[0], 16), (od_kn[0], 16)])
    q1, k1, vt1 = _inproj((xs,), mod_tab[1], norm_mix_w[1][None, :], w1, wvt1, g1, cos_t, sin_t, bd,
                          n_k=16, n_v=8, d_v=2 * hd, k_pairs=True)
    lam_init = 0.8 - 0.6 * math.exp(-0.3 * 1)
    lam_vecs = jnp.stack([od_lq1[0], od_lk1[0], od_lq2[0], od_lk2[0]]).astype(F32)
    o_c = _diff_attention(q1, k1, vt1, lam_vecs, od_subln[0].astype(F32)[:, None],
                          tq=DIFF_QUERY_TILES * QUERY_TILE, tk=KEY_CHUNK, lam_init=lam_init)
    return _post((xs,), mod_tab[1], [o_c], od_w_out[0].astype(BF16), norm_ffn_w[1][None, :],
                 ffn_w_in[1].astype(BF16), ffn_w_out[1].astype(BF16), latent_only=True)
```

```python
import functools
import math

import jax
import jax.numpy as jnp
from jax import lax
from jax.experimental import pallas as pl
from jax.experimental.pallas import tpu as pltpu

LANES = 128
MXU_TILE = 256
D_MODEL = 1024
HEAD_DIM = 64
CTX_LEN = 256
GRID_W = 64
WINDOW = 128
DEPTH = 2
ROPE_THETA = 10000.0
EPS = 1e-6
NEG_INF = -1e30
LOG2E = 1.4426950408889634
Q_SCALE = HEAD_DIM ** -0.5 * LOG2E
MOD_ROWS = 8
ROW_TILE = 256
MOD_COL_TILE = 1536
KEY_CHUNK = 256
QUERY_TILE = 256
DIFF_QUERY_TILES = 2
FLASH_BUFFERS = 4
WINDOW_BUFFERS = 2
SOFTMAX_ROWS = 64
SUM_ROWS = 16
VMEM_LIMIT = 56 * 1024 * 1024

F32 = jnp.float32
BF16 = jnp.bfloat16


def _nt_dot(a, b):
    return lax.dot_general(a, b, (((1,), (1,)), ((), ())), preferred_element_type=F32)


def _dot(a, b):
    return jnp.dot(a, b, preferred_element_type=F32)


def _params(sem):
    return pltpu.CompilerParams(dimension_semantics=sem, vmem_limit_bytes=VMEM_LIMIT)


def _mod_kernel(cc_ref, w_ref, b_ref, o_ref):
    a = cc_ref[...]
    a = a / (1.0 + jnp.exp(-a))
    o_ref[0] = _dot(a.astype(BF16), w_ref[0].astype(BF16)) + b_ref[0]


def _modulation(cc, mod_w, mod_b):
    depth, d, n = mod_w.shape
    tn = MOD_COL_TILE
    return pl.pallas_call(
        _mod_kernel,
        grid=(depth, n // tn),
        in_specs=[
            pl.BlockSpec((MOD_ROWS, d), lambda l, j: (0, 0)),
            pl.BlockSpec((1, d, tn), lambda l, j: (l, 0, j)),
            pl.BlockSpec((1, 1, tn), lambda l, j: (l, 0, j)),
        ],
        out_specs=pl.BlockSpec((1, MOD_ROWS, tn), lambda l, j: (l, 0, j)),
        out_shape=jax.ShapeDtypeStruct((depth, MOD_ROWS, n), F32),
        compiler_params=_params(("arbitrary", "arbitrary")),
        name="modulation",
    )(cc, mod_w, mod_b.reshape(depth, 1, n))


def _modulated_norm(x, nw, shift, scale):
    ms = jnp.mean(x * x, axis=-1, keepdims=True)
    return (x * lax.rsqrt(ms + EPS) * nw) * (1.0 + scale) + shift


def _stream_specs(xs, tm, off=0):
    if len(xs) == 1:
        return [pl.BlockSpec((1, tm, xs[0].shape[2]), lambda bi, i: (bi, i + off, 0))]
    assert tm == CTX_LEN and off == 0
    d = xs[0].shape[2]
    return [pl.BlockSpec((1, tm, d), lambda bi, i: (bi, 0, 0)),
            pl.BlockSpec((1, tm, d), lambda bi, i: (bi, jnp.maximum(i - 1, 0), 0))]


def _stream_tile(x_refs):
    if len(x_refs) == 1:
        return x_refs[0][0]
    return jnp.where(pl.program_id(1) == 0, x_refs[0][0], x_refs[1][0])


def _inproj_kernel(*refs, n_q, n_k, n_v, d_v, k_pairs):
    x_refs = refs[:-11]
    mod_ref, nw_ref, w_ref, wvt_ref, g_ref, cos_ref, sin_ref, bd_ref, q_ref, k_ref, vt_ref = refs[-11:]
    tm = q_ref.shape[1]
    h = _modulated_norm(_stream_tile(x_refs), nw_ref[...], mod_ref[0, 0, 0:1, :], mod_ref[0, 0, 1:2, :])
    hb = h.astype(BF16)
    y = _dot(hb, w_ref[...])
    cos = cos_ref[...]
    sin = sin_ref[...]
    lane = lax.broadcasted_iota(jnp.int32, (tm, LANES), 1)
    first_half = (lane % HEAD_DIM) < (HEAD_DIM // 2)
    n_norm = n_q + n_k * HEAD_DIM
    for c in range(n_norm // MXU_TILE):
        yc = y[:, MXU_TILE * c:MXU_TILE * (c + 1)]
        ss = _dot((yc * yc).astype(BF16), bd_ref[...])
        z = yc * lax.rsqrt(ss * (1.0 / HEAD_DIM) + EPS) * g_ref[:, MXU_TILE * c:MXU_TILE * (c + 1)]
        for half in range(MXU_TILE // LANES):
            zc = z[:, LANES * half:LANES * (half + 1)]
            partner = jnp.where(first_half, pltpu.roll(zc, LANES - HEAD_DIM // 2, 1),
                                pltpu.roll(zc, HEAD_DIM // 2, 1))
            o = zc * cos + partner * sin
            col = MXU_TILE * c + LANES * half
            if col < n_q:
                q_ref[0, :, col:col + LANES] = (o * Q_SCALE).astype(BF16)
            else:
                kh = (col - n_q) // HEAD_DIM
                if k_pairs:
                    k_ref[0, kh // 2] = o.astype(BF16)
                else:
                    k_ref[0, kh] = o[:, :HEAD_DIM].astype(BF16)
                    k_ref[0, kh + 1] = o[:, HEAD_DIM:].astype(BF16)
    vt = _nt_dot(wvt_ref[...], hb).astype(BF16)
    ones_rows = (lax.broadcasted_iota(jnp.int32, (SUM_ROWS, tm), 0) == 0).astype(BF16)
    for hv in range(n_v):
        vt_ref[0, hv, 0:d_v, :] = vt[d_v * hv:d_v * (hv + 1), :]
        vt_ref[0, hv, d_v:d_v + SUM_ROWS, :] = ones_rows


def _inproj(xs, mod_tab, nw, w, wvt, gains, cos_t, sin_t, bd, *, n_k, n_v, d_v, k_pairs=False):
    b, d = xs[0].shape[0], xs[0].shape[2]
    tb = sum(a.shape[1] for a in xs)
    n_q = D_MODEL
    n_in = w.shape[1]
    tm = ROW_TILE
    kern = functools.partial(_inproj_kernel, n_q=n_q, n_k=n_k, n_v=n_v, d_v=d_v, k_pairs=k_pairs)
    k_blocks, k_width = (n_k // 2, 2 * HEAD_DIM) if k_pairs else (n_k, HEAD_DIM)
    return pl.pallas_call(
        kern,
        grid=(b, tb // tm),
        in_specs=_stream_specs(xs, tm) + [
            pl.BlockSpec((1, 1, MOD_ROWS, d), lambda bi, i: (bi, jnp.minimum(i, 1), 0, 0)),
            pl.BlockSpec((1, d), lambda bi, i: (0, 0)),
            pl.BlockSpec((d, n_in), lambda bi, i: (0, 0)),
            pl.BlockSpec((n_v * d_v, d), lambda bi, i: (0, 0)),
            pl.BlockSpec((1, gains.shape[1]), lambda bi, i: (0, 0)),
            pl.BlockSpec((tm, LANES), lambda bi, i: (i, 0)),
            pl.BlockSpec((tm, LANES), lambda bi, i: (i, 0)),
            pl.BlockSpec((MXU_TILE, MXU_TILE), lambda bi, i: (0, 0)),
        ],
        out_specs=[
            pl.BlockSpec((1, tm, n_q), lambda bi, i: (bi, i, 0)),
            pl.BlockSpec((1, k_blocks, tm, k_width), lambda bi, i: (bi, 0, i, 0)),
            pl.BlockSpec((1, n_v, d_v + SUM_ROWS, tm), lambda bi, i: (bi, 0, 0, i)),
        ],
        out_shape=[
            jax.ShapeDtypeStruct((b, tb, n_q), BF16),
            jax.ShapeDtypeStruct((b, k_blocks, tb, k_width), BF16),
            jax.ShapeDtypeStruct((b, n_v, d_v + SUM_ROWS, tb), BF16),
        ],
        compiler_params=_params(("arbitrary", "arbitrary")),
        name="inproj",
    )(*xs, mod_tab, nw, w, wvt, gains, cos_t, sin_t, bd)


def _softmax_step(s_ref, p_ref, m, chunk_max):
    m_new = chunk_max if m is None else jnp.maximum(m, chunk_max)
    for r in range(0, s_ref.shape[0], SOFTMAX_ROWS):
        p_ref[r:r + SOFTMAX_ROWS, :] = jnp.exp2(s_ref[r:r + SOFTMAX_ROWS, :] - m_new).astype(BF16)
    return m_new, (None if m is None else jnp.exp2(m - m_new))


class _Stream:
    def __init__(self, chunks, score, vt_slice, s_bufs, p_bufs):
        self.chunks, self.score, self.vt_slice = chunks, score, vt_slice
        self.s_bufs, self.p_bufs = s_bufs, p_bufs
        self.m = self.acc = None
        self.cmax = {}

    def buf(self, refs, c):
        return refs[c % len(refs)].at[0:self.chunks[c][1], :]

    def write_scores(self, c):
        if c < len(self.chunks):
            s = self.score(*self.chunks[c])
            self.buf(self.s_bufs, c)[...] = s
            self.cmax[c] = jnp.max(s, axis=0, keepdims=True)

    def step(self, c):
        if c >= len(self.chunks):
            return
        self.write_scores(c + 2)
        lo, size, _ = self.chunks[c]
        p_ref = self.buf(self.p_bufs, c)
        self.m, alpha = _softmax_step(self.buf(self.s_bufs, c), p_ref, self.m, self.cmax.pop(c))
        pv = _dot(self.vt_slice(lo, size), p_ref[...])
        self.acc = pv if alpha is None else alpha * self.acc + pv


def _run_streams(streams):
    for c in range(2):
        for st in streams:
            st.write_scores(c)
    for c in range(max(len(st.chunks) for st in streams)):
        for st in streams:
            st.step(c)


def _key_chunks(n_keys, tk):
    return [(0, CTX_LEN, "ctx")] + [(lo, tk, "lat") for lo in range(CTX_LEN, n_keys, tk)]


def _stream_scratch(n_keys, nq, n_streams, n_bufs):
    n = n_streams * n_bufs
    return [pltpu.VMEM((n_keys, nq), F32)] * n + [pltpu.VMEM((n_keys, nq), BF16)] * n


def _split_scratch(scratch, n_streams, n_bufs):
    n = n_streams * n_bufs
    return [(scratch[j * n_bufs:(j + 1) * n_bufs], scratch[n + j * n_bufs:n + (j + 1) * n_bufs])
            for j in range(n_streams)]


def _layer0_attention_kernel(qa_ref, qb_ref, ka_ref, vta_ref, kb_ref, vtb_ref, sink_ref, oa_ref, ob_ref,
                             *scratch, tk):
    tq = qa_ref.shape[1]
    tb = ka_ref.shape[2]
    span = tq + 2 * WINDOW
    t = pl.program_id(2)
    n_dense = 2 * 4 * FLASH_BUFFERS
    dense_bufs = _split_scratch(scratch[:n_dense], 4, FLASH_BUFFERS)
    window_bufs = _split_scratch(scratch[n_dense:], 4, WINDOW_BUFFERS)

    def run(chunks):
        qa = qa_ref[0]
        ws = pl.multiple_of(jnp.clip(tq * t - WINDOW, 0, tb - span), LANES)
        kpos = ws - CTX_LEN + lax.broadcasted_iota(jnp.int32, (span, tq), 0)
        qpos = tq * t - CTX_LEN + lax.broadcasted_iota(jnp.int32, (span, tq), 1)
        ok = (jnp.abs(qpos - kpos) <= WINDOW) & (kpos >= 0) & (qpos >= 0)

        def window_score(g):
            qh = qa[:, HEAD_DIM * g:HEAD_DIM * (g + 1)]

            def score(lo, size, tag):
                s = _nt_dot(ka_ref[0, 0, pl.ds(lo, size), :], qh)
                return jnp.where(ok, s, NEG_INF) if tag == "win" else s
            return score

        win = [_Stream([(0, CTX_LEN, "ctx"), (ws, span, "win")], window_score(g),
                       lambda lo, size: vta_ref[0, 0, :, pl.ds(lo, size)], *window_bufs[g])
               for g in range(4)]

        qb = qb_ref[0]
        qts = [qb[:, HEAD_DIM * g:HEAD_DIM * (g + 1)].astype(F32).T.astype(BF16) for g in range(4)]
        dense = [_Stream(chunks, lambda lo, size, tag, qt=qt: _dot(kb_ref[0, 0, lo:lo + size, :], qt),
                         lambda lo, size: vtb_ref[0, 0, :, lo:lo + size], *dense_bufs[g])
                 for g, qt in enumerate(qts)]

        _run_streams(win + dense)
        for g, st in enumerate(win):
            sink = sink_ref[0, :, tq * g:tq * (g + 1)]
            m_all = jnp.maximum(st.m, sink)
            scale = jnp.exp2(st.m - m_all)
            l = st.acc[HEAD_DIM:HEAD_DIM + 1, :] * scale + jnp.exp2(sink - m_all)
            oa_ref[0, g] = (st.acc[:HEAD_DIM, :] * scale / l).astype(BF16)
        for g, st in enumerate(dense):
            ob_ref[0, g] = (st.acc[:HEAD_DIM, :] / st.acc[HEAD_DIM:HEAD_DIM + 1, :]).astype(BF16)

    chunks = _key_chunks(tb, tk)
    is_latent = t * tq >= CTX_LEN
    pl.when(is_latent)(lambda: run(chunks))
    pl.when(jnp.logical_not(is_latent))(lambda: run(chunks[:1]))


def _layer0_attention(q, k, vt, sink_rows, *, tq, tk):
    b, tb, _ = q.shape
    assert (tb - CTX_LEN) % tk == 0

    def kv_specs(first_head):
        return [pl.BlockSpec((1, 1, tb, HEAD_DIM), lambda bi, h, i: (bi, first_head + h, 0, 0)),
                pl.BlockSpec((1, 1, HEAD_DIM + SUM_ROWS, tb), lambda bi, h, i: (bi, first_head + h, 0, 0))]

    out = jax.ShapeDtypeStruct((b, 8, HEAD_DIM, tb), BF16)
    return pl.pallas_call(
        functools.partial(_layer0_attention_kernel, tk=tk),
        grid=(b, 2, tb // tq),
        in_specs=[pl.BlockSpec((1, tq, 4 * HEAD_DIM), lambda bi, h, i: (bi, i, h)),
                  pl.BlockSpec((1, tq, 4 * HEAD_DIM), lambda bi, h, i: (bi, i, 2 + h))]
        + kv_specs(0) + kv_specs(2) + [pl.BlockSpec((1, 1, 4 * tq), lambda bi, h, i: (h, 0, 0))],
        out_specs=[pl.BlockSpec((1, 4, HEAD_DIM, tq), lambda bi, h, i: (bi, h, 0, i))] * 2,
        out_shape=[out, out],
        scratch_shapes=(_stream_scratch(tk, tq, 4, FLASH_BUFFERS)
                        + _stream_scratch(tq + 2 * WINDOW, tq, 4, WINDOW_BUFFERS)),
        compiler_params=_params(("arbitrary", "arbitrary", "arbitrary")),
        name="layer0_attention",
    )(q, q, k, vt, k, vt, sink_rows)


def _diff_kernel(qa_ref, qb_ref, k_ref, vt_ref, lam_ref, subw_ref, o_ref, *scratch, tk, lam_init):
    tb = k_ref.shape[2]
    d_v = 2 * HEAD_DIM
    chunks = _key_chunks(tb, tk)
    bufs = _split_scratch(scratch, 4, FLASH_BUFFERS)
    streams = []
    lane = lax.broadcasted_iota(jnp.int32, qa_ref.shape[1:], 1)
    for a in range(2):
        for q_ref in (qa_ref, qb_ref):
            qh = jnp.where(lane // HEAD_DIM == a, q_ref[0], jnp.zeros_like(q_ref[0]))
            streams.append(_Stream(chunks, lambda lo, size, tag, qh=qh: _nt_dot(k_ref[0, 0, lo:lo + size, :], qh),
                                   lambda lo, size: vt_ref[0, 0, :, lo:lo + size], *bufs[len(streams)]))
    _run_streams(streams)
    ots = [st.acc[:d_v, :] / st.acc[d_v:d_v + 1, :] for st in streams]
    ots = [jnp.concatenate(ots[0:2], axis=1), jnp.concatenate(ots[2:4], axis=1)]
    lam = (jnp.exp(jnp.sum(lam_ref[0:1, :] * lam_ref[1:2, :], axis=-1, keepdims=True))
           - jnp.exp(jnp.sum(lam_ref[2:3, :] * lam_ref[3:4, :], axis=-1, keepdims=True)) + lam_init)
    o = ots[0] - lam * ots[1]
    ms = jnp.mean(o * o, axis=0, keepdims=True)
    o = (o * lax.rsqrt(ms + EPS) * subw_ref[...]) * (1.0 - lam_init)
    o_ref[0, 0] = o.astype(BF16)


def _diff_attention(q, k, vt, lam_vecs, subw, *, tq, tk, lam_init):
    b, tb, _ = q.shape
    n_lat = tb - CTX_LEN
    n_heads = vt.shape[1]
    d_v = vt.shape[2] - SUM_ROWS
    hq = tq // 2
    assert CTX_LEN % hq == 0
    q_off = CTX_LEN // hq
    return pl.pallas_call(
        functools.partial(_diff_kernel, tk=tk, lam_init=lam_init),
        grid=(b, n_heads, n_lat // tq),
        in_specs=[
            pl.BlockSpec((1, hq, 2 * HEAD_DIM), lambda bi, h, i: (bi, 2 * i + q_off, h)),
            pl.BlockSpec((1, hq, 2 * HEAD_DIM), lambda bi, h, i: (bi, 2 * i + 1 + q_off, h)),
            pl.BlockSpec((1, 1, tb, 2 * HEAD_DIM), lambda bi, h, i: (bi, h, 0, 0)),
            pl.BlockSpec((1, 1, d_v + SUM_ROWS, tb), lambda bi, h, i: (bi, h, 0, 0)),
            pl.BlockSpec((4, HEAD_DIM), lambda bi, h, i: (0, 0)),
            pl.BlockSpec((d_v, 1), lambda bi, h, i: (0, 0)),
        ],
        out_specs=pl.BlockSpec((1, 1, d_v, tq), lambda bi, h, i: (bi, h, 0, i)),
        out_shape=jax.ShapeDtypeStruct((b, n_heads, d_v, n_lat), BF16),
        scratch_shapes=_stream_scratch(tk, tq // 2, 4, FLASH_BUFFERS),
        compiler_params=_params(("arbitrary", "arbitrary", "arbitrary")),
        name="diff_attention",
    )(q, q, k, vt, lam_vecs, subw)


def _post_kernel(*refs, n_attn):
    n_x = len(refs) - n_attn - 6
    x_refs, mod_ref = refs[:n_x], refs[n_x]
    o_refs = refs[n_x + 1:n_x + 1 + n_attn]
    wo_ref, nw_ref, wi_ref, wf_ref, out_ref = refs[n_x + 1 + n_attn:]
    mod = mod_ref[0, 0]
    kw = wo_ref.shape[0] // n_attn
    a = None
    for j, o_ref in enumerate(o_refs):
        ot = o_ref[0].reshape(kw, o_ref.shape[3])
        part = lax.dot_general(ot, wo_ref[kw * j:kw * (j + 1), :], (((0,), (0,)), ((), ())),
                               preferred_element_type=F32)
        a = part if a is None else a + part
    x1 = _stream_tile(x_refs) + mod[2:3, :] * a
    h = _modulated_norm(x1, nw_ref[...], mod[3:4, :], mod[4:5, :])
    u = _dot(h.astype(BF16), wi_ref[...])
    f = wf_ref.shape[0]
    gate = u[:, :f]
    act = (gate / (1.0 + jnp.exp(-gate))) * u[:, f:]
    y = _dot(act.astype(BF16), wf_ref[...])
    out_ref[0] = x1 + mod[5:6, :] * y


def _post(xs, mod_tab, attn_outs, wo, nw, wi, wf, *, latent_only):
    b, d = xs[0].shape[0], xs[0].shape[2]
    tb = sum(a.shape[1] for a in xs)
    tm = ROW_TILE
    off = CTX_LEN // tm if latent_only else 0
    n_rows = tb - CTX_LEN if latent_only else tb
    n_attn = len(attn_outs)

    def const(shape):
        return pl.BlockSpec(shape, lambda bi, i: (0,) * len(shape))

    in_specs = _stream_specs(xs, tm, off) + [
        pl.BlockSpec((1, 1, MOD_ROWS, d), lambda bi, i: (bi, jnp.minimum(i + off, 1), 0, 0)),
    ]
    for o in attn_outs:
        in_specs.append(pl.BlockSpec((1, o.shape[1], o.shape[2], tm), lambda bi, i: (bi, 0, 0, i)))
    in_specs += [const(wo.shape), const(nw.shape), const(wi.shape), const(wf.shape)]
    return pl.pallas_call(
        functools.partial(_post_kernel, n_attn=n_attn),
        grid=(b, n_rows // tm),
        in_specs=in_specs,
        out_specs=pl.BlockSpec((1, tm, d), lambda bi, i: (bi, i, 0)),
        out_shape=jax.ShapeDtypeStruct((b, n_rows, d), F32),
        compiler_params=_params(("arbitrary", "arbitrary")),
        name="post",
    )(*xs, mod_tab, *attn_outs, wo, nw, wi, wf)


def _deinterleave_perm(n_heads):
    one = jnp.concatenate([jnp.arange(0, HEAD_DIM, 2), jnp.arange(1, HEAD_DIM, 2)])
    return (jnp.arange(n_heads)[:, None] * HEAD_DIM + one[None, :]).reshape(-1)


def _rope_tables(n_lat):
    rows = n_lat // GRID_W
    row = jnp.repeat(jnp.arange(rows, dtype=F32), GRID_W)
    col = jnp.tile(jnp.arange(GRID_W, dtype=F32), rows)
    n_freq = HEAD_DIM // 4
    inv = ROPE_THETA ** (-jnp.arange(n_freq, dtype=F32) / n_freq)
    ang = jnp.concatenate([row[:, None] * inv, col[:, None] * inv], axis=-1)
    cos, sin = jnp.cos(ang), jnp.sin(ang)
    cos = jnp.concatenate([jnp.ones((CTX_LEN, HEAD_DIM // 2), F32), cos], axis=0)
    sin = jnp.concatenate([jnp.zeros((CTX_LEN, HEAD_DIM // 2), F32), sin], axis=0)
    cos_t = jnp.tile(jnp.concatenate([cos, cos], axis=-1), (1, 2))
    sin_t = jnp.tile(jnp.concatenate([-sin, sin], axis=-1), (1, 2))
    return cos_t, sin_t


def _gain_row(parts):
    one = _deinterleave_perm(1)
    return jnp.concatenate([jnp.tile(g[one], n) for g, n in parts])[None, :].astype(F32)


def kernel(x, c, ctx, c_ctx, mod_w, mod_b, norm_mix_w, norm_ffn_w, ev_w_in, ev_w_out, ev_qn_a, ev_kn_a,
           ev_qn_b, ev_kn_b, ev_sink_a, od_w_in, od_w_out, od_qn, od_kn, od_lq1, od_lk1, od_lq2, od_lk2,
           od_subln, ffn_w_in, ffn_w_out):
    b, n_lat, d = x.shape
    assert d == D_MODEL and ctx.shape[1] == CTX_LEN and b < MOD_ROWS
    assert n_lat % (DIFF_QUERY_TILES * QUERY_TILE) == 0 and n_lat % KEY_CHUNK == 0 and n_lat % GRID_W == 0
    hd = HEAD_DIM

    cc = jnp.zeros((MOD_ROWS, d), F32).at[:b].set(c).at[b].set(c_ctx)
    mod = _modulation(cc, mod_w, mod_b).reshape(DEPTH, MOD_ROWS, 6, d)
    mod_lat = mod[:, :b]
    mod_ctx = jnp.broadcast_to(mod[:, b:b + 1], mod_lat.shape)
    mod_tab = jnp.stack([mod_ctx, mod_lat], axis=2)
    mod_tab = jnp.pad(mod_tab, ((0, 0), (0, 0), (0, 0), (0, MOD_ROWS - 6), (0, 0)))

    cos_t, sin_t = _rope_tables(n_lat)
    bd = jnp.kron(jnp.eye(MXU_TILE // hd, dtype=F32), jnp.ones((hd, hd), F32)).astype(BF16)

    w = ev_w_in[0]
    widths = [8 * hd, 2 * hd, 2 * hd, 8 * hd, 2 * hd, 2 * hd]
    qa, ka, va, qb, kb, vb = jnp.split(w, [sum(widths[:n]) for n in range(1, 6)], axis=1)
    p8, p2 = _deinterleave_perm(8), _deinterleave_perm(2)
    w0 = jnp.concatenate([qa[:, p8], qb[:, p8], ka[:, p2], kb[:, p2]], axis=1).astype(BF16)
    wvt0 = jnp.concatenate([va, vb], axis=1).T.astype(BF16)
    g0 = _gain_row([(ev_qn_a[0], 8), (ev_qn_b[0], 8), (ev_kn_a[0], 2), (ev_kn_b[0], 2)])
    q0, k0, vt0 = _inproj((ctx, x), mod_tab[0], norm_mix_w[0][None, :], w0, wvt0, g0, cos_t, sin_t, bd,
                          n_k=4, n_v=4, d_v=hd)
    sink_rows = jnp.repeat(ev_sink_a[0].astype(F32) * LOG2E, QUERY_TILE).reshape(2, 1, 4 * QUERY_TILE)
    o_a, o_b = _layer0_attention(q0, k0, vt0, sink_rows, tq=QUERY_TILE, tk=KEY_CHUNK)
    xs = _post((ctx, x), mod_tab[0], [o_a, o_b], ev_w_out[0].astype(BF16), norm_ffn_w[0][None, :],
               ffn_w_in[0].astype(BF16), ffn_w_out[0].astype(BF16), latent_only=False)

    w = od_w_in[0]
    p16 = _deinterleave_perm(16)
    n_qk = 16 * hd
    w1 = jnp.concatenate([w[:, :n_qk][:, p16], w[:, n_qk:2 * n_qk][:, p16]], axis=1).astype(BF16)
    wvt1 = w[:, 2 * n_qk:].T.astype(BF16)
    g1 = _gain_row([(od_qn[0], 16), (od_kn[0], 16)])
    q1, k1, vt1 = _inproj((xs,), mod_tab[1], norm_mix_w[1][None, :], w1, wvt1, g1, cos_t, sin_t, bd,
                          n_k=16, n_v=8, d_v=2 * hd, k_pairs=True)
    lam_init = 0.8 - 0.6 * math.exp(-0.3 * 1)
    lam_vecs = jnp.stack([od_lq1[0], od_lk1[0], od_lq2[0], od_lk2[0]]).astype(F32)
    o_c = _diff_attention(q1, k1, vt1, lam_vecs, od_subln[0].astype(F32)[:, None],
                          tq=DIFF_QUERY_TILES * QUERY_TILE, tk=KEY_CHUNK, lam_init=lam_init)
    return _post((xs,), mod_tab[1], [o_c], od_w_out[0].astype(BF16), norm_ffn_w[1][None, :],
                 ffn_w_in[1].astype(BF16), ffn_w_out[1].astype(BF16), latent_only=True)
```

```python
import functools
import math

import jax
import jax.numpy as jnp
from jax import lax
from jax.experimental import pallas as pl
from jax.experimental.pallas import tpu as pltpu

LANES = 128
MXU_TILE = 256
D_MODEL = 1024
HEAD_DIM = 64
CTX_LEN = 256
GRID_W = 64
WINDOW = 128
DEPTH = 2
ROPE_THETA = 10000.0
EPS = 1e-6
NEG_INF = -1e30
LOG2E = 1.4426950408889634
Q_SCALE = HEAD_DIM ** -0.5 * LOG2E
MOD_ROWS = 8
ROW_TILE = 256
MOD_COL_TILE = 1536
KEY_CHUNK = 256
QUERY_TILE = 256
DIFF_QUERY_TILES = 2
FLASH_BUFFERS = 4
WINDOW_BUFFERS = 2
SOFTMAX_ROWS = 64
SUM_ROWS = 16
VMEM_LIMIT = 56 * 1024 * 1024

F32 = jnp.float32
BF16 = jnp.bfloat16


def _nt_dot(a, b):
    return lax.dot_general(a, b, (((1,), (1,)), ((), ())), preferred_element_type=F32)


def _dot(a, b):
    return jnp.dot(a, b, preferred_element_type=F32)


def _params(sem):
    return pltpu.CompilerParams(dimension_semantics=sem, vmem_limit_bytes=VMEM_LIMIT)


def _mod_kernel(cc_ref, w_ref, b_ref, o_ref):
    a = cc_ref[...]
    a = a / (1.0 + jnp.exp(-a))
    o_ref[0] = _dot(a.astype(BF16), w_ref[0].astype(BF16)) + b_ref[0]


def _modulation(cc, mod_w, mod_b):
    depth, d, n = mod_w.shape
    tn = MOD_COL_TILE
    return pl.pallas_call(
        _mod_kernel,
        grid=(depth, n // tn),
        in_specs=[
            pl.BlockSpec((MOD_ROWS, d), lambda l, j: (0, 0)),
            pl.BlockSpec((1, d, tn), lambda l, j: (l, 0, j)),
            pl.BlockSpec((1, 1, tn), lambda l, j: (l, 0, j)),
        ],
        out_specs=pl.BlockSpec((1, MOD_ROWS, tn), lambda l, j: (l, 0, j)),
        out_shape=jax.ShapeDtypeStruct((depth, MOD_ROWS, n), F32),
        compiler_params=_params(("arbitrary", "arbitrary")),
        name="modulation",
    )(cc, mod_w, mod_b.reshape(depth, 1, n))


def _modulated_norm(x, nw, shift, scale):
    ms = jnp.mean(x * x, axis=-1, keepdims=True)
    return (x * lax.rsqrt(ms + EPS) * nw) * (1.0 + scale) + shift


def _stream_specs(xs, tm, off=0):
    if len(xs) == 1:
        return [pl.BlockSpec((1, tm, xs[0].shape[2]), lambda bi, i: (bi, i + off, 0))]
    assert tm == CTX_LEN and off == 0
    d = xs[0].shape[2]
    return [pl.BlockSpec((1, tm, d), lambda bi, i: (bi, 0, 0)),
            pl.BlockSpec((1, tm, d), lambda bi, i: (bi, jnp.maximum(i - 1, 0), 0))]


def _stream_tile(x_refs):
    if len(x_refs) == 1:
        return x_refs[0][0]
    return jnp.where(pl.program_id(1) == 0, x_refs[0][0], x_refs[1][0])


def _inproj_kernel(*refs, n_q, n_k, n_v, d_v, k_pairs):
    _inproj_body(_stream_tile(refs[:-11]), *refs[-11:], n_q=n_q, n_k=n_k, n_v=n_v, d_v=d_v, k_pairs=k_pairs)


def _inproj_body(x, mod_ref, nw_ref, w_ref, wvt_ref, g_ref, cos_ref, sin_ref, bd_ref, q_ref, k_ref, vt_ref,
                 *, n_q, n_k, n_v, d_v, k_pairs):
    tm = q_ref.shape[1]
    h = _modulated_norm(x, nw_ref[...], mod_ref[0, 0, 0:1, :], mod_ref[0, 0, 1:2, :])
    hb = h.astype(BF16)
    y = _dot(hb, w_ref[...])
    cos = cos_ref[...]
    sin = sin_ref[...]
    lane = lax.broadcasted_iota(jnp.int32, (tm, LANES), 1)
    first_half = (lane % HEAD_DIM) < (HEAD_DIM // 2)
    n_norm = n_q + n_k * HEAD_DIM
    for c in range(n_norm // MXU_TILE):
        yc = y[:, MXU_TILE * c:MXU_TILE * (c + 1)]
        ss = _dot((yc * yc).astype(BF16), bd_ref[...])
        z = yc * lax.rsqrt(ss * (1.0 / HEAD_DIM) + EPS) * g_ref[:, MXU_TILE * c:MXU_TILE * (c + 1)]
        for half in range(MXU_TILE // LANES):
            zc = z[:, LANES * half:LANES * (half + 1)]
            partner = jnp.where(first_half, pltpu.roll(zc, LANES - HEAD_DIM // 2, 1),
                                pltpu.roll(zc, HEAD_DIM // 2, 1))
            o = zc * cos + partner * sin
            col = MXU_TILE * c + LANES * half
            if col < n_q:
                q_ref[0, :, col:col + LANES] = (o * Q_SCALE).astype(BF16)
            else:
                kh = (col - n_q) // HEAD_DIM
                if k_pairs:
                    k_ref[0, kh // 2] = o.astype(BF16)
                else:
                    k_ref[0, kh] = o[:, :HEAD_DIM].astype(BF16)
                    k_ref[0, kh + 1] = o[:, HEAD_DIM:].astype(BF16)
    vt = _nt_dot(wvt_ref[...], hb).astype(BF16)
    ones_rows = (lax.broadcasted_iota(jnp.int32, (SUM_ROWS, tm), 0) == 0).astype(BF16)
    for hv in range(n_v):
        vt_ref[0, hv, 0:d_v, :] = vt[d_v * hv:d_v * (hv + 1), :]
        vt_ref[0, hv, d_v:d_v + SUM_ROWS, :] = ones_rows


def _inproj(xs, mod_tab, nw, w, wvt, gains, cos_t, sin_t, bd, *, n_k, n_v, d_v, k_pairs=False):
    b, d = xs[0].shape[0], xs[0].shape[2]
    tb = sum(a.shape[1] for a in xs)
    n_q = D_MODEL
    n_in = w.shape[1]
    tm = ROW_TILE
    kern = functools.partial(_inproj_kernel, n_q=n_q, n_k=n_k, n_v=n_v, d_v=d_v, k_pairs=k_pairs)
    k_blocks, k_width = (n_k // 2, 2 * HEAD_DIM) if k_pairs else (n_k, HEAD_DIM)
    in_specs, out_specs, out_shape = _inproj_specs(b, tb, d, w, wvt, gains, k_blocks, k_width, n_v, d_v)
    return pl.pallas_call(
        kern,
        grid=(b, tb // tm),
        in_specs=_stream_specs(xs, tm) + in_specs,
        out_specs=out_specs,
        out_shape=out_shape,
        compiler_params=_params(("arbitrary", "arbitrary")),
        name="inproj",
    )(*xs, mod_tab, nw, w, wvt, gains, cos_t, sin_t, bd)


def _resident(shape):
    return pl.BlockSpec(shape, lambda bi, i: (0,) * len(shape), pipeline_mode=pl.Buffered(1))


def _inproj_specs(b, tb, d, w, wvt, gains, k_blocks, k_width, n_v, d_v):
    tm = ROW_TILE
    in_specs = [
        pl.BlockSpec((1, 1, MOD_ROWS, d), lambda bi, i: (bi, jnp.minimum(i, 1), 0, 0)),
        _resident((1, d)), _resident(w.shape), _resident(wvt.shape), _resident(gains.shape),
        pl.BlockSpec((tm, LANES), lambda bi, i: (i, 0)),
        pl.BlockSpec((tm, LANES), lambda bi, i: (i, 0)),
        _resident((MXU_TILE, MXU_TILE)),
    ]
    out_specs = [
        pl.BlockSpec((1, tm, D_MODEL), lambda bi, i: (bi, i, 0)),
        pl.BlockSpec((1, k_blocks, tm, k_width), lambda bi, i: (bi, 0, i, 0)),
        pl.BlockSpec((1, n_v, d_v + SUM_ROWS, tm), lambda bi, i: (bi, 0, 0, i)),
    ]
    out_shape = [
        jax.ShapeDtypeStruct((b, tb, D_MODEL), BF16),
        jax.ShapeDtypeStruct((b, k_blocks, tb, k_width), BF16),
        jax.ShapeDtypeStruct((b, n_v, d_v + SUM_ROWS, tb), BF16),
    ]
    return in_specs, out_specs, out_shape


def _softmax_step(s_ref, p_ref, m, chunk_max):
    m_new = chunk_max if m is None else jnp.maximum(m, chunk_max)
    for r in range(0, s_ref.shape[0], SOFTMAX_ROWS):
        p_ref[r:r + SOFTMAX_ROWS, :] = jnp.exp2(s_ref[r:r + SOFTMAX_ROWS, :] - m_new).astype(BF16)
    return m_new, (None if m is None else jnp.exp2(m - m_new))


class _Stream:
    def __init__(self, chunks, score, vt_slice, s_bufs, p_bufs):
        self.chunks, self.score, self.vt_slice = chunks, score, vt_slice
        self.s_bufs, self.p_bufs = s_bufs, p_bufs
        self.m = self.acc = None
        self.cmax = {}

    def buf(self, refs, c):
        return refs[c % len(refs)].at[0:self.chunks[c][1], :]

    def write_scores(self, c):
        if c < len(self.chunks):
            s = self.score(*self.chunks[c])
            self.buf(self.s_bufs, c)[...] = s
            self.cmax[c] = jnp.max(s, axis=0, keepdims=True)

    def step(self, c):
        if c >= len(self.chunks):
            return
        self.write_scores(c + 2)
        lo, size, _ = self.chunks[c]
        p_ref = self.buf(self.p_bufs, c)
        self.m, alpha = _softmax_step(self.buf(self.s_bufs, c), p_ref, self.m, self.cmax.pop(c))
        pv = _dot(self.vt_slice(lo, size), p_ref[...])
        self.acc = pv if alpha is None else alpha * self.acc + pv


def _run_streams(streams):
    for c in range(2):
        for st in streams:
            st.write_scores(c)
    for c in range(max(len(st.chunks) for st in streams)):
        for st in streams:
            st.step(c)


def _key_chunks(n_keys, tk):
    return [(0, CTX_LEN, "ctx")] + [(lo, tk, "lat") for lo in range(CTX_LEN, n_keys, tk)]


def _stream_scratch(n_keys, nq, n_streams, n_bufs):
    n = n_streams * n_bufs
    return [pltpu.VMEM((n_keys, nq), F32)] * n + [pltpu.VMEM((n_keys, nq), BF16)] * n


def _split_scratch(scratch, n_streams, n_bufs):
    n = n_streams * n_bufs
    return [(scratch[j * n_bufs:(j + 1) * n_bufs], scratch[n + j * n_bufs:n + (j + 1) * n_bufs])
            for j in range(n_streams)]


def _layer0_attention_kernel(qa_ref, qb_ref, ka_ref, vta_ref, kb_ref, vtb_ref, sink_ref, oa_ref, ob_ref,
                             *scratch, tk):
    tq = qa_ref.shape[1]
    tb = ka_ref.shape[2]
    span = tq + 2 * WINDOW
    t = pl.program_id(2)
    n_dense = 2 * 4 * FLASH_BUFFERS
    dense_bufs = _split_scratch(scratch[:n_dense], 4, FLASH_BUFFERS)
    window_bufs = _split_scratch(scratch[n_dense:], 4, WINDOW_BUFFERS)

    def run(chunks):
        qa = qa_ref[0]
        ws = pl.multiple_of(jnp.clip(tq * t - WINDOW, 0, tb - span), LANES)
        kpos = ws - CTX_LEN + lax.broadcasted_iota(jnp.int32, (span, tq), 0)
        qpos = tq * t - CTX_LEN + lax.broadcasted_iota(jnp.int32, (span, tq), 1)
        ok = (jnp.abs(qpos - kpos) <= WINDOW) & (kpos >= 0) & (qpos >= 0)

        def window_score(g):
            qh = qa[:, HEAD_DIM * g:HEAD_DIM * (g + 1)]

            def score(lo, size, tag):
                s = _nt_dot(ka_ref[0, 0, pl.ds(lo, size), :], qh)
                return jnp.where(ok, s, NEG_INF) if tag == "win" else s
            return score

        win = [_Stream([(0, CTX_LEN, "ctx"), (ws, span, "win")], window_score(g),
                       lambda lo, size: vta_ref[0, 0, :, pl.ds(lo, size)], *window_bufs[g])
               for g in range(4)]

        qb = qb_ref[0]
        qts = [qb[:, HEAD_DIM * g:HEAD_DIM * (g + 1)].astype(F32).T.astype(BF16) for g in range(4)]
        dense = [_Stream(chunks, lambda lo, size, tag, qt=qt: _dot(kb_ref[0, 0, lo:lo + size, :], qt),
                         lambda lo, size: vtb_ref[0, 0, :, lo:lo + size], *dense_bufs[g])
                 for g, qt in enumerate(qts)]

        _run_streams(win + dense)
        for g, st in enumerate(win):
            sink = sink_ref[0, :, tq * g:tq * (g + 1)]
            m_all = jnp.maximum(st.m, sink)
            scale = jnp.exp2(st.m - m_all)
            l = st.acc[HEAD_DIM:HEAD_DIM + 1, :] * scale + jnp.exp2(sink - m_all)
            oa_ref[0, g] = (st.acc[:HEAD_DIM, :] * scale / l).astype(BF16)
        for g, st in enumerate(dense):
            ob_ref[0, g] = (st.acc[:HEAD_DIM, :] / st.acc[HEAD_DIM:HEAD_DIM + 1, :]).astype(BF16)

    chunks = _key_chunks(tb, tk)
    is_latent = t * tq >= CTX_LEN
    pl.when(is_latent)(lambda: run(chunks))
    pl.when(jnp.logical_not(is_latent))(lambda: run(chunks[:1]))


def _layer0_attention(q, k, vt, sink_rows, *, tq, tk):
    b, tb, _ = q.shape
    assert (tb - CTX_LEN) % tk == 0

    def kv_specs(first_head):
        return [pl.BlockSpec((1, 1, tb, HEAD_DIM), lambda bi, h, i: (bi, first_head + h, 0, 0)),
                pl.BlockSpec((1, 1, HEAD_DIM + SUM_ROWS, tb), lambda bi, h, i: (bi, first_head + h, 0, 0))]

    out = jax.ShapeDtypeStruct((b, 8, HEAD_DIM, tb), BF16)
    return pl.pallas_call(
        functools.partial(_layer0_attention_kernel, tk=tk),
        grid=(b, 2, tb // tq),
        in_specs=[pl.BlockSpec((1, tq, 4 * HEAD_DIM), lambda bi, h, i: (bi, i, h)),
                  pl.BlockSpec((1, tq, 4 * HEAD_DIM), lambda bi, h, i: (bi, i, 2 + h))]
        + kv_specs(0) + kv_specs(2) + [pl.BlockSpec((1, 1, 4 * tq), lambda bi, h, i: (h, 0, 0))],
        out_specs=[pl.BlockSpec((1, 4, HEAD_DIM, tq), lambda bi, h, i: (bi, h, 0, i))] * 2,
        out_shape=[out, out],
        scratch_shapes=(_stream_scratch(tk, tq, 4, FLASH_BUFFERS)
                        + _stream_scratch(tq + 2 * WINDOW, tq, 4, WINDOW_BUFFERS)),
        compiler_params=_params(("arbitrary", "arbitrary", "arbitrary")),
        name="layer0_attention",
    )(q, q, k, vt, k, vt, sink_rows)


def _diff_kernel(qa_ref, qb_ref, k_ref, vt_ref, lam_ref, subw_ref, o_ref, *scratch, tk, lam_init):
    tb = k_ref.shape[2]
    d_v = 2 * HEAD_DIM
    chunks = _key_chunks(tb, tk)
    bufs = _split_scratch(scratch, 4, FLASH_BUFFERS)
    streams = []
    lane = lax.broadcasted_iota(jnp.int32, qa_ref.shape[1:], 1)
    for a in range(2):
        for q_ref in (qa_ref, qb_ref):
            qh = jnp.where(lane // HEAD_DIM == a, q_ref[0], jnp.zeros_like(q_ref[0]))
            streams.append(_Stream(chunks, lambda lo, size, tag, qh=qh: _nt_dot(k_ref[0, 0, lo:lo + size, :], qh),
                                   lambda lo, size: vt_ref[0, 0, :, lo:lo + size], *bufs[len(streams)]))
    _run_streams(streams)
    ots = [st.acc[:d_v, :] / st.acc[d_v:d_v + 1, :] for st in streams]
    ots = [jnp.concatenate(ots[0:2], axis=1), jnp.concatenate(ots[2:4], axis=1)]
    lam = (jnp.exp(jnp.sum(lam_ref[0:1, :] * lam_ref[1:2, :], axis=-1, keepdims=True))
           - jnp.exp(jnp.sum(lam_ref[2:3, :] * lam_ref[3:4, :], axis=-1, keepdims=True)) + lam_init)
    o = ots[0] - lam * ots[1]
    ms = jnp.mean(o * o, axis=0, keepdims=True)
    o = (o * lax.rsqrt(ms + EPS) * subw_ref[...]) * (1.0 - lam_init)
    o_ref[0, 0] = o.astype(BF16)


def _diff_attention(q, k, vt, lam_vecs, subw, *, tq, tk, lam_init):
    b, tb, _ = q.shape
    n_lat = tb - CTX_LEN
    n_heads = vt.shape[1]
    d_v = vt.shape[2] - SUM_ROWS
    hq = tq // 2
    assert CTX_LEN % hq == 0
    q_off = CTX_LEN // hq
    return pl.pallas_call(
        functools.partial(_diff_kernel, tk=tk, lam_init=lam_init),
        grid=(b, n_heads, n_lat // tq),
        in_specs=[
            pl.BlockSpec((1, hq, 2 * HEAD_DIM), lambda bi, h, i: (bi, 2 * i + q_off, h)),
            pl.BlockSpec((1, hq, 2 * HEAD_DIM), lambda bi, h, i: (bi, 2 * i + 1 + q_off, h)),
            pl.BlockSpec((1, 1, tb, 2 * HEAD_DIM), lambda bi, h, i: (bi, h, 0, 0)),
            pl.BlockSpec((1, 1, d_v + SUM_ROWS, tb), lambda bi, h, i: (bi, h, 0, 0)),
            pl.BlockSpec((4, HEAD_DIM), lambda bi, h, i: (0, 0)),
            pl.BlockSpec((d_v, 1), lambda bi, h, i: (0, 0)),
        ],
        out_specs=pl.BlockSpec((1, 1, d_v, tq), lambda bi, h, i: (bi, h, 0, i)),
        out_shape=jax.ShapeDtypeStruct((b, n_heads, d_v, n_lat), BF16),
        scratch_shapes=_stream_scratch(tk, tq // 2, 4, FLASH_BUFFERS),
        compiler_params=_params(("arbitrary", "arbitrary", "arbitrary")),
        name="diff_attention",
    )(q, q, k, vt, lam_vecs, subw)


def _post_kernel(*refs, n_attn):
    n_x = len(refs) - n_attn - 6
    out_ref = refs[-1]
    out_ref[0] = _post_body(_stream_tile(refs[:n_x]), refs[n_x], refs[n_x + 1:n_x + 1 + n_attn], *refs[-5:-1])


def _post_inproj_kernel(*refs, n_attn, n_x, inproj_cfg):
    n_post = n_x + 1 + n_attn + 4
    x2 = _post_body(_stream_tile(refs[:n_x]), refs[n_x], refs[n_x + 1:n_x + 1 + n_attn], *refs[n_post - 4:n_post])
    out_ref = refs[n_post + 8]
    out_ref[0] = x2
    _inproj_body(x2, *refs[n_post:n_post + 8], *refs[n_post + 9:], **inproj_cfg)


def _post_body(x, mod_ref, o_refs, wo_ref, nw_ref, wi_ref, wf_ref):
    n_attn = len(o_refs)
    mod = mod_ref[0, 0]
    kw = wo_ref.shape[0] // n_attn
    a = None
    for j, o_ref in enumerate(o_refs):
        ot = o_ref[0].reshape(kw, o_ref.shape[3])
        part = lax.dot_general(ot, wo_ref[kw * j:kw * (j + 1), :], (((0,), (0,)), ((), ())),
                               preferred_element_type=F32)
        a = part if a is None else a + part
    x1 = x + mod[2:3, :] * a
    h = _modulated_norm(x1, nw_ref[...], mod[3:4, :], mod[4:5, :])
    u = _dot(h.astype(BF16), wi_ref[...])
    f = wf_ref.shape[0]
    gate = u[:, :f]
    act = (gate / (1.0 + jnp.exp(-gate))) * u[:, f:]
    y = _dot(act.astype(BF16), wf_ref[...])
    return x1 + mod[5:6, :] * y


def _post(xs, mod_tab, attn_outs, wo, nw, wi, wf, *, latent_only, next_inproj=None):
    b, d = xs[0].shape[0], xs[0].shape[2]
    tb = sum(a.shape[1] for a in xs)
    tm = ROW_TILE
    off = CTX_LEN // tm if latent_only else 0
    n_rows = tb - CTX_LEN if latent_only else tb
    n_attn = len(attn_outs)

    in_specs = _stream_specs(xs, tm, off) + [
        pl.BlockSpec((1, 1, MOD_ROWS, d), lambda bi, i: (bi, jnp.minimum(i + off, 1), 0, 0)),
    ]
    for o in attn_outs:
        in_specs.append(pl.BlockSpec((1, o.shape[1], o.shape[2], tm), lambda bi, i: (bi, 0, 0, i)))
    in_specs += [_resident(wo.shape), _resident(nw.shape), _resident(wi.shape), _resident(wf.shape)]
    x_spec = pl.BlockSpec((1, tm, d), lambda bi, i: (bi, i, 0))
    x_shape = jax.ShapeDtypeStruct((b, n_rows, d), F32)
    if next_inproj is not None:
        assert not latent_only
        *operands, cfg = next_inproj
        k_blocks, k_width = (cfg["n_k"] // 2, 2 * HEAD_DIM) if cfg["k_pairs"] else (cfg["n_k"], HEAD_DIM)
        p_in, p_out, p_shape = _inproj_specs(b, tb, d, operands[2], operands[3], operands[4], k_blocks, k_width,
                                             cfg["n_v"], cfg["d_v"])
        return pl.pallas_call(
            functools.partial(_post_inproj_kernel, n_attn=n_attn, n_x=len(xs), inproj_cfg=dict(cfg, n_q=D_MODEL)),
            grid=(b, n_rows // tm),
            in_specs=in_specs + p_in,
            out_specs=[x_spec] + p_out,
            out_shape=[x_shape] + p_shape,
            compiler_params=_params(("arbitrary", "arbitrary")),
            name="post_inproj",
        )(*xs, mod_tab, *attn_outs, wo, nw, wi, wf, *operands)
    return pl.pallas_call(
        functools.partial(_post_kernel, n_attn=n_attn),
        grid=(b, n_rows // tm),
        in_specs=in_specs,
        out_specs=x_spec,
        out_shape=x_shape,
        compiler_params=_params(("arbitrary", "arbitrary")),
        name="post",
    )(*xs, mod_tab, *attn_outs, wo, nw, wi, wf)


def _deinterleave_perm(n_heads):
    one = jnp.concatenate([jnp.arange(0, HEAD_DIM, 2), jnp.arange(1, HEAD_DIM, 2)])
    return (jnp.arange(n_heads)[:, None] * HEAD_DIM + one[None, :]).reshape(-1)


def _rope_tables(n_lat):
    rows = n_lat // GRID_W
    row = jnp.repeat(jnp.arange(rows, dtype=F32), GRID_W)
    col = jnp.tile(jnp.arange(GRID_W, dtype=F32), rows)
    n_freq = HEAD_DIM // 4
    inv = ROPE_THETA ** (-jnp.arange(n_freq, dtype=F32) / n_freq)
    ang = jnp.concatenate([row[:, None] * inv, col[:, None] * inv], axis=-1)
    cos, sin = jnp.cos(ang), jnp.sin(ang)
    cos = jnp.concatenate([jnp.ones((CTX_LEN, HEAD_DIM // 2), F32), cos], axis=0)
    sin = jnp.concatenate([jnp.zeros((CTX_LEN, HEAD_DIM // 2), F32), sin], axis=0)
    cos_t = jnp.tile(jnp.concatenate([cos, cos], axis=-1), (1, 2))
    sin_t = jnp.tile(jnp.concatenate([-sin, sin], axis=-1), (1, 2))
    return cos_t, sin_t


def _gain_row(parts):
    one = _deinterleave_perm(1)
    return jnp.concatenate([jnp.tile(g[one], n) for g, n in parts])[None, :].astype(F32)


def kernel(x, c, ctx, c_ctx, mod_w, mod_b, norm_mix_w, norm_ffn_w, ev_w_in, ev_w_out, ev_qn_a, ev_kn_a,
           ev_qn_b, ev_kn_b, ev_sink_a, od_w_in, od_w_out, od_qn, od_kn, od_lq1, od_lk1, od_lq2, od_lk2,
           od_subln, ffn_w_in, ffn_w_out):
    b, n_lat, d = x.shape
    assert d == D_MODEL and ctx.shape[1] == CTX_LEN and b < MOD_ROWS
    assert n_lat % (DIFF_QUERY_TILES * QUERY_TILE) == 0 and n_lat % KEY_CHUNK == 0 and n_lat % GRID_W == 0
    hd = HEAD_DIM

    cc = jnp.zeros((MOD_ROWS, d), F32).at[:b].set(c).at[b].set(c_ctx)
    mod = _modulation(cc, mod_w, mod_b).reshape(DEPTH, MOD_ROWS, 6, d)
    mod_lat = mod[:, :b]
    mod_ctx = jnp.broadcast_to(mod[:, b:b + 1], mod_lat.shape)
    mod_tab = jnp.stack([mod_ctx, mod_lat], axis=2)
    mod_tab = jnp.pad(mod_tab, ((0, 0), (0, 0), (0, 0), (0, MOD_ROWS - 6), (0, 0)))

    cos_t, sin_t = _rope_tables(n_lat)
    bd = jnp.kron(jnp.eye(MXU_TILE // hd, dtype=F32), jnp.ones((hd, hd), F32)).astype(BF16)

    w = ev_w_in[0]
    widths = [8 * hd, 2 * hd, 2 * hd, 8 * hd, 2 * hd, 2 * hd]
    qa, ka, va, qb, kb, vb = jnp.split(w, [sum(widths[:n]) for n in range(1, 6)], axis=1)
    p8, p2 = _deinterleave_perm(8), _deinterleave_perm(2)
    w0 = jnp.concatenate([qa[:, p8], qb[:, p8], ka[:, p2], kb[:, p2]], axis=1).astype(BF16)
    wvt0 = jnp.concatenate([va, vb], axis=1).T.astype(BF16)
    g0 = _gain_row([(ev_qn_a[0], 8), (ev_qn_b[0], 8), (ev_kn_a[0], 2), (ev_kn_b[0], 2)])
    q0, k0, vt0 = _inproj((ctx, x), mod_tab[0], norm_mix_w[0][None, :], w0, wvt0, g0, cos_t, sin_t, bd,
                          n_k=4, n_v=4, d_v=hd)
    sink_rows = jnp.repeat(ev_sink_a[0].astype(F32) * LOG2E, QUERY_TILE).reshape(2, 1, 4 * QUERY_TILE)
    o_a, o_b = _layer0_attention(q0, k0, vt0, sink_rows, tq=QUERY_TILE, tk=KEY_CHUNK)
    w = od_w_in[0]
    p16 = _deinterleave_perm(16)
    n_qk = 16 * hd
    w1 = jnp.concatenate([w[:, :n_qk][:, p16], w[:, n_qk:2 * n_qk][:, p16]], axis=1).astype(BF16)
    wvt1 = w[:, 2 * n_qk:].T.astype(BF16)
    g1 = _gain_row([(od_qn[0], 16), (od_kn[0], 16)])
    xs, q1, k1, vt1 = _post((ctx, x), mod_tab[0], [o_a, o_b], ev_w_out[0].astype(BF16), norm_ffn_w[0][None, :],
                            ffn_w_in[0].astype(BF16), ffn_w_out[0].astype(BF16), latent_only=False,
                            next_inproj=(mod_tab[1], norm_mix_w[1][None, :], w1, wvt1, g1, cos_t, sin_t, bd,
                                         dict(n_k=16, n_v=8, d_v=2 * hd, k_pairs=True)))
    lam_init = 0.8 - 0.6 * math.exp(-0.3 * 1)
    lam_vecs = jnp.stack([od_lq1[0], od_lk1[0], od_lq2[0], od_lk2[0]]).astype(F32)
    o_c = _diff_attention(q1, k1, vt1, lam_vecs, od_subln[0].astype(F32)[:, None],
                          tq=DIFF_QUERY_TILES * QUERY_TILE, tk=KEY_CHUNK, lam_init=lam_init)
    return _post((xs,), mod_tab[1], [o_c], od_w_out[0].astype(BF16), norm_ffn_w[1][None, :],
                 ffn_w_in[1].astype(BF16), ffn_w_out[1].astype(BF16), latent_only=True)
```

```python
import functools
import math

import jax
import jax.numpy as jnp
from jax import lax
from jax.experimental import pallas as pl
from jax.experimental.pallas import tpu as pltpu

LANES = 128
MXU_TILE = 256
D_MODEL = 1024
HEAD_DIM = 64
CTX_LEN = 256
GRID_W = 64
WINDOW = 128
DEPTH = 2
ROPE_THETA = 10000.0
EPS = 1e-6
NEG_INF = -1e30
LOG2E = 1.4426950408889634
Q_SCALE = HEAD_DIM ** -0.5 * LOG2E
MOD_ROWS = 8
ROW_TILE = 256
MOD_COL_TILE = 1536
KEY_CHUNK = 256
QUERY_TILE = 256
DIFF_QUERY_TILES = 2
FLASH_BUFFERS = 4
WINDOW_BUFFERS = 2
SOFTMAX_ROWS = 64
SUM_ROWS = 16
VMEM_LIMIT = 56 * 1024 * 1024

F32 = jnp.float32
BF16 = jnp.bfloat16


def _nt_dot(a, b):
    return lax.dot_general(a, b, (((1,), (1,)), ((), ())), preferred_element_type=F32)


def _dot(a, b):
    return jnp.dot(a, b, preferred_element_type=F32)


def _params(sem):
    sem = ("parallel",) * len(sem)
    return pltpu.CompilerParams(dimension_semantics=sem, vmem_limit_bytes=VMEM_LIMIT)


def _mod_kernel(cc_ref, w_ref, b_ref, o_ref):
    a = cc_ref[...]
    a = a / (1.0 + jnp.exp(-a))
    o_ref[0] = _dot(a.astype(BF16), w_ref[0].astype(BF16)) + b_ref[0]


def _modulation(cc, mod_w, mod_b):
    depth, d, n = mod_w.shape
    tn = MOD_COL_TILE
    return pl.pallas_call(
        _mod_kernel,
        grid=(depth, n // tn),
        in_specs=[
            pl.BlockSpec((MOD_ROWS, d), lambda l, j: (0, 0)),
            pl.BlockSpec((1, d, tn), lambda l, j: (l, 0, j)),
            pl.BlockSpec((1, 1, tn), lambda l, j: (l, 0, j)),
        ],
        out_specs=pl.BlockSpec((1, MOD_ROWS, tn), lambda l, j: (l, 0, j)),
        out_shape=jax.ShapeDtypeStruct((depth, MOD_ROWS, n), F32),
        compiler_params=_params(("arbitrary", "arbitrary")),
        name="modulation",
    )(cc, mod_w, mod_b.reshape(depth, 1, n))


def _modulated_norm(x, nw, shift, scale):
    ms = jnp.mean(x * x, axis=-1, keepdims=True)
    return (x * lax.rsqrt(ms + EPS) * nw) * (1.0 + scale) + shift


def _stream_specs(xs, tm, off=0):
    if len(xs) == 1:
        return [pl.BlockSpec((1, tm, xs[0].shape[2]), lambda bi, i: (bi, i + off, 0))]
    assert tm == CTX_LEN and off == 0
    d = xs[0].shape[2]
    return [pl.BlockSpec((1, tm, d), lambda bi, i: (bi, 0, 0)),
            pl.BlockSpec((1, tm, d), lambda bi, i: (bi, jnp.maximum(i - 1, 0), 0))]


def _stream_tile(x_refs):
    if len(x_refs) == 1:
        return x_refs[0][0]
    return jnp.where(pl.program_id(1) == 0, x_refs[0][0], x_refs[1][0])


def _inproj_kernel(*refs, n_q, n_k, n_v, d_v, k_pairs):
    _inproj_body(_stream_tile(refs[:-11]), *refs[-11:], n_q=n_q, n_k=n_k, n_v=n_v, d_v=d_v, k_pairs=k_pairs)


def _inproj_body(x, mod_ref, nw_ref, w_ref, wvt_ref, g_ref, cos_ref, sin_ref, bd_ref, q_ref, k_ref, vt_ref,
                 *, n_q, n_k, n_v, d_v, k_pairs):
    tm = q_ref.shape[1]
    h = _modulated_norm(x, nw_ref[...], mod_ref[0, 0, 0:1, :], mod_ref[0, 0, 1:2, :])
    hb = h.astype(BF16)
    y = _dot(hb, w_ref[...])
    cos = cos_ref[...]
    sin = sin_ref[...]
    lane = lax.broadcasted_iota(jnp.int32, (tm, LANES), 1)
    first_half = (lane % HEAD_DIM) < (HEAD_DIM // 2)
    n_norm = n_q + n_k * HEAD_DIM
    for c in range(n_norm // MXU_TILE):
        yc = y[:, MXU_TILE * c:MXU_TILE * (c + 1)]
        ss = _dot((yc * yc).astype(BF16), bd_ref[...])
        z = yc * lax.rsqrt(ss * (1.0 / HEAD_DIM) + EPS) * g_ref[:, MXU_TILE * c:MXU_TILE * (c + 1)]
        for half in range(MXU_TILE // LANES):
            zc = z[:, LANES * half:LANES * (half + 1)]
            partner = jnp.where(first_half, pltpu.roll(zc, LANES - HEAD_DIM // 2, 1),
                                pltpu.roll(zc, HEAD_DIM // 2, 1))
            o = zc * cos + partner * sin
            col = MXU_TILE * c + LANES * half
            if col < n_q:
                q_ref[0, :, col:col + LANES] = (o * Q_SCALE).astype(BF16)
            else:
                kh = (col - n_q) // HEAD_DIM
                if k_pairs:
                    k_ref[0, kh // 2] = o.astype(BF16)
                else:
                    k_ref[0, kh] = o[:, :HEAD_DIM].astype(BF16)
                    k_ref[0, kh + 1] = o[:, HEAD_DIM:].astype(BF16)
    vt = _nt_dot(wvt_ref[...], hb).astype(BF16)
    ones_rows = (lax.broadcasted_iota(jnp.int32, (SUM_ROWS, tm), 0) == 0).astype(BF16)
    for hv in range(n_v):
        vt_ref[0, hv, 0:d_v, :] = vt[d_v * hv:d_v * (hv + 1), :]
        vt_ref[0, hv, d_v:d_v + SUM_ROWS, :] = ones_rows


def _inproj(xs, mod_tab, nw, w, wvt, gains, cos_t, sin_t, bd, *, n_k, n_v, d_v, k_pairs=False):
    b, d = xs[0].shape[0], xs[0].shape[2]
    tb = sum(a.shape[1] for a in xs)
    n_q = D_MODEL
    n_in = w.shape[1]
    tm = ROW_TILE
    kern = functools.partial(_inproj_kernel, n_q=n_q, n_k=n_k, n_v=n_v, d_v=d_v, k_pairs=k_pairs)
    k_blocks, k_width = (n_k // 2, 2 * HEAD_DIM) if k_pairs else (n_k, HEAD_DIM)
    in_specs, out_specs, out_shape = _inproj_specs(b, tb, d, w, wvt, gains, k_blocks, k_width, n_v, d_v)
    return pl.pallas_call(
        kern,
        grid=(b, tb // tm),
        in_specs=_stream_specs(xs, tm) + in_specs,
        out_specs=out_specs,
        out_shape=out_shape,
        compiler_params=_params(("arbitrary", "arbitrary")),
        name="inproj",
    )(*xs, mod_tab, nw, w, wvt, gains, cos_t, sin_t, bd)


def _resident(shape):
    return pl.BlockSpec(shape, lambda bi, i: (0,) * len(shape), pipeline_mode=pl.Buffered(1))


def _inproj_specs(b, tb, d, w, wvt, gains, k_blocks, k_width, n_v, d_v):
    tm = ROW_TILE
    in_specs = [
        pl.BlockSpec((1, 1, MOD_ROWS, d), lambda bi, i: (bi, jnp.minimum(i, 1), 0, 0)),
        _resident((1, d)), _resident(w.shape), _resident(wvt.shape), _resident(gains.shape),
        pl.BlockSpec((tm, LANES), lambda bi, i: (i, 0)),
        pl.BlockSpec((tm, LANES), lambda bi, i: (i, 0)),
        _resident((MXU_TILE, MXU_TILE)),
    ]
    out_specs = [
        pl.BlockSpec((1, tm, D_MODEL), lambda bi, i: (bi, i, 0)),
        pl.BlockSpec((1, k_blocks, tm, k_width), lambda bi, i: (bi, 0, i, 0)),
        pl.BlockSpec((1, n_v, d_v + SUM_ROWS, tm), lambda bi, i: (bi, 0, 0, i)),
    ]
    out_shape = [
        jax.ShapeDtypeStruct((b, tb, D_MODEL), BF16),
        jax.ShapeDtypeStruct((b, k_blocks, tb, k_width), BF16),
        jax.ShapeDtypeStruct((b, n_v, d_v + SUM_ROWS, tb), BF16),
    ]
    return in_specs, out_specs, out_shape


def _softmax_step(s_ref, p_ref, m, chunk_max):
    m_new = chunk_max if m is None else jnp.maximum(m, chunk_max)
    for r in range(0, s_ref.shape[0], SOFTMAX_ROWS):
        p_ref[r:r + SOFTMAX_ROWS, :] = jnp.exp2(s_ref[r:r + SOFTMAX_ROWS, :] - m_new).astype(BF16)
    return m_new, (None if m is None else jnp.exp2(m - m_new))


class _Stream:
    def __init__(self, chunks, score, vt_slice, s_bufs, p_bufs):
        self.chunks, self.score, self.vt_slice = chunks, score, vt_slice
        self.s_bufs, self.p_bufs = s_bufs, p_bufs
        self.m = self.acc = None
        self.cmax = {}

    def buf(self, refs, c):
        return refs[c % len(refs)].at[0:self.chunks[c][1], :]

    def write_scores(self, c):
        if c < len(self.chunks):
            s = self.score(*self.chunks[c])
            self.buf(self.s_bufs, c)[...] = s
            self.cmax[c] = jnp.max(s, axis=0, keepdims=True)

    def step(self, c):
        if c >= len(self.chunks):
            return
        self.write_scores(c + 2)
        lo, size, _ = self.chunks[c]
        p_ref = self.buf(self.p_bufs, c)
        self.m, alpha = _softmax_step(self.buf(self.s_bufs, c), p_ref, self.m, self.cmax.pop(c))
        pv = _dot(self.vt_slice(lo, size), p_ref[...])
        self.acc = pv if alpha is None else alpha * self.acc + pv


def _run_streams(streams):
    for c in range(2):
        for st in streams:
            st.write_scores(c)
    for c in range(max(len(st.chunks) for st in streams)):
        for st in streams:
            st.step(c)


def _key_chunks(n_keys, tk):
    return [(0, CTX_LEN, "ctx")] + [(lo, tk, "lat") for lo in range(CTX_LEN, n_keys, tk)]


def _stream_scratch(n_keys, nq, n_streams, n_bufs):
    n = n_streams * n_bufs
    return [pltpu.VMEM((n_keys, nq), F32)] * n + [pltpu.VMEM((n_keys, nq), BF16)] * n


def _split_scratch(scratch, n_streams, n_bufs):
    n = n_streams * n_bufs
    return [(scratch[j * n_bufs:(j + 1) * n_bufs], scratch[n + j * n_bufs:n + (j + 1) * n_bufs])
            for j in range(n_streams)]


def _layer0_attention_kernel(qa_ref, qb_ref, ka_ref, vta_ref, kb_ref, vtb_ref, sink_ref, oa_ref, ob_ref,
                             *scratch, tk):
    tq = qa_ref.shape[1]
    tb = ka_ref.shape[2]
    span = tq + 2 * WINDOW
    t = pl.program_id(2)
    n_dense = 2 * 4 * FLASH_BUFFERS
    dense_bufs = _split_scratch(scratch[:n_dense], 4, FLASH_BUFFERS)
    window_bufs = _split_scratch(scratch[n_dense:], 4, WINDOW_BUFFERS)

    def run(chunks):
        qa = qa_ref[0]
        ws = pl.multiple_of(jnp.clip(tq * t - WINDOW, 0, tb - span), LANES)
        kpos = ws - CTX_LEN + lax.broadcasted_iota(jnp.int32, (span, tq), 0)
        qpos = tq * t - CTX_LEN + lax.broadcasted_iota(jnp.int32, (span, tq), 1)
        ok = (jnp.abs(qpos - kpos) <= WINDOW) & (kpos >= 0) & (qpos >= 0)

        def window_score(g):
            qh = qa[:, HEAD_DIM * g:HEAD_DIM * (g + 1)]

            def score(lo, size, tag):
                s = _nt_dot(ka_ref[0, 0, pl.ds(lo, size), :], qh)
                return jnp.where(ok, s, NEG_INF) if tag == "win" else s
            return score

        win = [_Stream([(0, CTX_LEN, "ctx"), (ws, span, "win")], window_score(g),
                       lambda lo, size: vta_ref[0, 0, :, pl.ds(lo, size)], *window_bufs[g])
               for g in range(4)]

        qb = qb_ref[0]
        qts = [qb[:, HEAD_DIM * g:HEAD_DIM * (g + 1)].astype(F32).T.astype(BF16) for g in range(4)]
        dense = [_Stream(chunks, lambda lo, size, tag, qt=qt: _dot(kb_ref[0, 0, lo:lo + size, :], qt),
                         lambda lo, size: vtb_ref[0, 0, :, lo:lo + size], *dense_bufs[g])
                 for g, qt in enumerate(qts)]

        _run_streams(win + dense)
        for g, st in enumerate(win):
            sink = sink_ref[0, :, tq * g:tq * (g + 1)]
            m_all = jnp.maximum(st.m, sink)
            scale = jnp.exp2(st.m - m_all)
            l = st.acc[HEAD_DIM:HEAD_DIM + 1, :] * scale + jnp.exp2(sink - m_all)
            oa_ref[0, g] = (st.acc[:HEAD_DIM, :] * scale / l).astype(BF16)
        for g, st in enumerate(dense):
            ob_ref[0, g] = (st.acc[:HEAD_DIM, :] / st.acc[HEAD_DIM:HEAD_DIM + 1, :]).astype(BF16)

    chunks = _key_chunks(tb, tk)
    is_latent = t * tq >= CTX_LEN
    pl.when(is_latent)(lambda: run(chunks))
    pl.when(jnp.logical_not(is_latent))(lambda: run(chunks[:1]))


def _layer0_attention(q, k, vt, sink_rows, *, tq, tk):
    b, tb, _ = q.shape
    assert (tb - CTX_LEN) % tk == 0

    def kv_specs(first_head):
        return [pl.BlockSpec((1, 1, tb, HEAD_DIM), lambda bi, h, i: (bi, first_head + h, 0, 0)),
                pl.BlockSpec((1, 1, HEAD_DIM + SUM_ROWS, tb), lambda bi, h, i: (bi, first_head + h, 0, 0))]

    out = jax.ShapeDtypeStruct((b, 8, HEAD_DIM, tb), BF16)
    return pl.pallas_call(
        functools.partial(_layer0_attention_kernel, tk=tk),
        grid=(b, 2, tb // tq),
        in_specs=[pl.BlockSpec((1, tq, 4 * HEAD_DIM), lambda bi, h, i: (bi, i, h)),
                  pl.BlockSpec((1, tq, 4 * HEAD_DIM), lambda bi, h, i: (bi, i, 2 + h))]
        + kv_specs(0) + kv_specs(2) + [pl.BlockSpec((1, 1, 4 * tq), lambda bi, h, i: (h, 0, 0))],
        out_specs=[pl.BlockSpec((1, 4, HEAD_DIM, tq), lambda bi, h, i: (bi, h, 0, i))] * 2,
        out_shape=[out, out],
        scratch_shapes=(_stream_scratch(tk, tq, 4, FLASH_BUFFERS)
                        + _stream_scratch(tq + 2 * WINDOW, tq, 4, WINDOW_BUFFERS)),
        compiler_params=_params(("arbitrary", "arbitrary", "arbitrary")),
        name="layer0_attention",
    )(q, q, k, vt, k, vt, sink_rows)


def _diff_kernel(qa_ref, qb_ref, k_ref, vt_ref, lam_ref, subw_ref, o_ref, *scratch, tk, lam_init):
    tb = k_ref.shape[2]
    d_v = 2 * HEAD_DIM
    chunks = _key_chunks(tb, tk)
    bufs = _split_scratch(scratch, 4, FLASH_BUFFERS)
    streams = []
    lane = lax.broadcasted_iota(jnp.int32, qa_ref.shape[1:], 1)
    for a in range(2):
        for q_ref in (qa_ref, qb_ref):
            qh = jnp.where(lane // HEAD_DIM == a, q_ref[0], jnp.zeros_like(q_ref[0]))
            streams.append(_Stream(chunks, lambda lo, size, tag, qh=qh: _nt_dot(k_ref[0, 0, lo:lo + size, :], qh),
                                   lambda lo, size: vt_ref[0, 0, :, lo:lo + size], *bufs[len(streams)]))
    _run_streams(streams)
    ots = [st.acc[:d_v, :] / st.acc[d_v:d_v + 1, :] for st in streams]
    ots = [jnp.concatenate(ots[0:2], axis=1), jnp.concatenate(ots[2:4], axis=1)]
    lam = (jnp.exp(jnp.sum(lam_ref[0:1, :] * lam_ref[1:2, :], axis=-1, keepdims=True))
           - jnp.exp(jnp.sum(lam_ref[2:3, :] * lam_ref[3:4, :], axis=-1, keepdims=True)) + lam_init)
    o = ots[0] - lam * ots[1]
    ms = jnp.mean(o * o, axis=0, keepdims=True)
    o = (o * lax.rsqrt(ms + EPS) * subw_ref[...]) * (1.0 - lam_init)
    o_ref[0, 0] = o.astype(BF16)


def _diff_attention(q, k, vt, lam_vecs, subw, *, tq, tk, lam_init):
    b, tb, _ = q.shape
    n_lat = tb - CTX_LEN
    n_heads = vt.shape[1]
    d_v = vt.shape[2] - SUM_ROWS
    hq = tq // 2
    assert CTX_LEN % hq == 0
    q_off = CTX_LEN // hq
    return pl.pallas_call(
        functools.partial(_diff_kernel, tk=tk, lam_init=lam_init),
        grid=(b, n_heads, n_lat // tq),
        in_specs=[
            pl.BlockSpec((1, hq, 2 * HEAD_DIM), lambda bi, h, i: (bi, 2 * i + q_off, h)),
            pl.BlockSpec((1, hq, 2 * HEAD_DIM), lambda bi, h, i: (bi, 2 * i + 1 + q_off, h)),
            pl.BlockSpec((1, 1, tb, 2 * HEAD_DIM), lambda bi, h, i: (bi, h, 0, 0)),
            pl.BlockSpec((1, 1, d_v + SUM_ROWS, tb), lambda bi, h, i: (bi, h, 0, 0)),
            pl.BlockSpec((4, HEAD_DIM), lambda bi, h, i: (0, 0)),
            pl.BlockSpec((d_v, 1), lambda bi, h, i: (0, 0)),
        ],
        out_specs=pl.BlockSpec((1, 1, d_v, tq), lambda bi, h, i: (bi, h, 0, i)),
        out_shape=jax.ShapeDtypeStruct((b, n_heads, d_v, n_lat), BF16),
        scratch_shapes=_stream_scratch(tk, tq // 2, 4, FLASH_BUFFERS),
        compiler_params=_params(("arbitrary", "arbitrary", "arbitrary")),
        name="diff_attention",
    )(q, q, k, vt, lam_vecs, subw)


def _post_kernel(*refs, n_attn):
    n_x = len(refs) - n_attn - 6
    out_ref = refs[-1]
    out_ref[0] = _post_body(_stream_tile(refs[:n_x]), refs[n_x], refs[n_x + 1:n_x + 1 + n_attn], *refs[-5:-1])


def _post_inproj_kernel(*refs, n_attn, n_x, inproj_cfg):
    n_post = n_x + 1 + n_attn + 4
    x2 = _post_body(_stream_tile(refs[:n_x]), refs[n_x], refs[n_x + 1:n_x + 1 + n_attn], *refs[n_post - 4:n_post])
    out_ref = refs[n_post + 8]
    out_ref[0] = x2
    _inproj_body(x2, *refs[n_post:n_post + 8], *refs[n_post + 9:], **inproj_cfg)


def _post_body(x, mod_ref, o_refs, wo_ref, nw_ref, wi_ref, wf_ref):
    n_attn = len(o_refs)
    mod = mod_ref[0, 0]
    kw = wo_ref.shape[0] // n_attn
    a = None
    for j, o_ref in enumerate(o_refs):
        ot = o_ref[0].reshape(kw, o_ref.shape[3])
        part = lax.dot_general(ot, wo_ref[kw * j:kw * (j + 1), :], (((0,), (0,)), ((), ())),
                               preferred_element_type=F32)
        a = part if a is None else a + part
    x1 = x + mod[2:3, :] * a
    h = _modulated_norm(x1, nw_ref[...], mod[3:4, :], mod[4:5, :])
    u = _dot(h.astype(BF16), wi_ref[...])
    f = wf_ref.shape[0]
    gate = u[:, :f]
    act = (gate / (1.0 + jnp.exp(-gate))) * u[:, f:]
    y = _dot(act.astype(BF16), wf_ref[...])
    return x1 + mod[5:6, :] * y


def _post(xs, mod_tab, attn_outs, wo, nw, wi, wf, *, latent_only, next_inproj=None):
    b, d = xs[0].shape[0], xs[0].shape[2]
    tb = sum(a.shape[1] for a in xs)
    tm = ROW_TILE
    off = CTX_LEN // tm if latent_only else 0
    n_rows = tb - CTX_LEN if latent_only else tb
    n_attn = len(attn_outs)

    in_specs = _stream_specs(xs, tm, off) + [
        pl.BlockSpec((1, 1, MOD_ROWS, d), lambda bi, i: (bi, jnp.minimum(i + off, 1), 0, 0)),
    ]
    for o in attn_outs:
        in_specs.append(pl.BlockSpec((1, o.shape[1], o.shape[2], tm), lambda bi, i: (bi, 0, 0, i)))
    in_specs += [_resident(wo.shape), _resident(nw.shape), _resident(wi.shape), _resident(wf.shape)]
    x_spec = pl.BlockSpec((1, tm, d), lambda bi, i: (bi, i, 0))
    x_shape = jax.ShapeDtypeStruct((b, n_rows, d), F32)
    if next_inproj is not None:
        assert not latent_only
        *operands, cfg = next_inproj
        k_blocks, k_width = (cfg["n_k"] // 2, 2 * HEAD_DIM) if cfg["k_pairs"] else (cfg["n_k"], HEAD_DIM)
        p_in, p_out, p_shape = _inproj_specs(b, tb, d, operands[2], operands[3], operands[4], k_blocks, k_width,
                                             cfg["n_v"], cfg["d_v"])
        return pl.pallas_call(
            functools.partial(_post_inproj_kernel, n_attn=n_attn, n_x=len(xs), inproj_cfg=dict(cfg, n_q=D_MODEL)),
            grid=(b, n_rows // tm),
            in_specs=in_specs + p_in,
            out_specs=[x_spec] + p_out,
            out_shape=[x_shape] + p_shape,
            compiler_params=_params(("arbitrary", "arbitrary")),
            name="post_inproj",
        )(*xs, mod_tab, *attn_outs, wo, nw, wi, wf, *operands)
    return pl.pallas_call(
        functools.partial(_post_kernel, n_attn=n_attn),
        grid=(b, n_rows // tm),
        in_specs=in_specs,
        out_specs=x_spec,
        out_shape=x_shape,
        compiler_params=_params(("arbitrary", "arbitrary")),
        name="post",
    )(*xs, mod_tab, *attn_outs, wo, nw, wi, wf)


def _deinterleave_perm(n_heads):
    one = jnp.concatenate([jnp.arange(0, HEAD_DIM, 2), jnp.arange(1, HEAD_DIM, 2)])
    return (jnp.arange(n_heads)[:, None] * HEAD_DIM + one[None, :]).reshape(-1)


def _rope_tables(n_lat):
    rows = n_lat // GRID_W
    row = jnp.repeat(jnp.arange(rows, dtype=F32), GRID_W)
    col = jnp.tile(jnp.arange(GRID_W, dtype=F32), rows)
    n_freq = HEAD_DIM // 4
    inv = ROPE_THETA ** (-jnp.arange(n_freq, dtype=F32) / n_freq)
    ang = jnp.concatenate([row[:, None] * inv, col[:, None] * inv], axis=-1)
    cos, sin = jnp.cos(ang), jnp.sin(ang)
    cos = jnp.concatenate([jnp.ones((CTX_LEN, HEAD_DIM // 2), F32), cos], axis=0)
    sin = jnp.concatenate([jnp.zeros((CTX_LEN, HEAD_DIM // 2), F32), sin], axis=0)
    cos_t = jnp.tile(jnp.concatenate([cos, cos], axis=-1), (1, 2))
    sin_t = jnp.tile(jnp.concatenate([-sin, sin], axis=-1), (1, 2))
    return cos_t, sin_t


def _gain_row(parts):
    one = _deinterleave_perm(1)
    return jnp.concatenate([jnp.tile(g[one], n) for g, n in parts])[None, :].astype(F32)


def kernel(x, c, ctx, c_ctx, mod_w, mod_b, norm_mix_w, norm_ffn_w, ev_w_in, ev_w_out, ev_qn_a, ev_kn_a,
           ev_qn_b, ev_kn_b, ev_sink_a, od_w_in, od_w_out, od_qn, od_kn, od_lq1, od_lk1, od_lq2, od_lk2,
           od_subln, ffn_w_in, ffn_w_out):
    b, n_lat, d = x.shape
    assert d == D_MODEL and ctx.shape[1] == CTX_LEN and b < MOD_ROWS
    assert n_lat % (DIFF_QUERY_TILES * QUERY_TILE) == 0 and n_lat % KEY_CHUNK == 0 and n_lat % GRID_W == 0
    hd = HEAD_DIM

    cc = jnp.zeros((MOD_ROWS, d), F32).at[:b].set(c).at[b].set(c_ctx)
    mod = _modulation(cc, mod_w, mod_b).reshape(DEPTH, MOD_ROWS, 6, d)
    mod_lat = mod[:, :b]
    mod_ctx = jnp.broadcast_to(mod[:, b:b + 1], mod_lat.shape)
    mod_tab = jnp.stack([mod_ctx, mod_lat], axis=2)
    mod_tab = jnp.pad(mod_tab, ((0, 0), (0, 0), (0, 0), (0, MOD_ROWS - 6), (0, 0)))

    cos_t, sin_t = _rope_tables(n_lat)
    bd = jnp.kron(jnp.eye(MXU_TILE // hd, dtype=F32), jnp.ones((hd, hd), F32)).astype(BF16)

    w = ev_w_in[0]
    widths = [8 * hd, 2 * hd, 2 * hd, 8 * hd, 2 * hd, 2 * hd]
    qa, ka, va, qb, kb, vb = jnp.split(w, [sum(widths[:n]) for n in range(1, 6)], axis=1)
    p8, p2 = _deinterleave_perm(8), _deinterleave_perm(2)
    w0 = jnp.concatenate([qa[:, p8], qb[:, p8], ka[:, p2], kb[:, p2]], axis=1).astype(BF16)
    wvt0 = jnp.concatenate([va, vb], axis=1).T.astype(BF16)
    g0 = _gain_row([(ev_qn_a[0], 8), (ev_qn_b[0], 8), (ev_kn_a[0], 2), (ev_kn_b[0], 2)])
    q0, k0, vt0 = _inproj((ctx, x), mod_tab[0], norm_mix_w[0][None, :], w0, wvt0, g0, cos_t, sin_t, bd,
                          n_k=4, n_v=4, d_v=hd)
    sink_rows = jnp.repeat(ev_sink_a[0].astype(F32) * LOG2E, QUERY_TILE).reshape(2, 1, 4 * QUERY_TILE)
    o_a, o_b = _layer0_attention(q0, k0, vt0, sink_rows, tq=QUERY_TILE, tk=KEY_CHUNK)
    w = od_w_in[0]
    p16 = _deinterleave_perm(16)
    n_qk = 16 * hd
    w1 = jnp.concatenate([w[:, :n_qk][:, p16], w[:, n_qk:2 * n_qk][:, p16]], axis=1).astype(BF16)
    wvt1 = w[:, 2 * n_qk:].T.astype(BF16)
    g1 = _gain_row([(od_qn[0], 16), (od_kn[0], 16)])
    xs, q1, k1, vt1 = _post((ctx, x), mod_tab[0], [o_a, o_b], ev_w_out[0].astype(BF16), norm_ffn_w[0][None, :],
                            ffn_w_in[0].astype(BF16), ffn_w_out[0].astype(BF16), latent_only=False,
                            next_inproj=(mod_tab[1], norm_mix_w[1][None, :], w1, wvt1, g1, cos_t, sin_t, bd,
                                         dict(n_k=16, n_v=8, d_v=2 * hd, k_pairs=True)))
    lam_init = 0.8 - 0.6 * math.exp(-0.3 * 1)
    lam_vecs = jnp.stack([od_lq1[0], od_lk1[0], od_lq2[0], od_lk2[0]]).astype(F32)
    o_c = _diff_attention(q1, k1, vt1, lam_vecs, od_subln[0].astype(F32)[:, None],
                          tq=DIFF_QUERY_TILES * QUERY_TILE, tk=KEY_CHUNK, lam_init=lam_init)
    return _post((xs,), mod_tab[1], [o_c], od_w_out[0].astype(BF16), norm_ffn_w[1][None, :],
                 ffn_w_in[1].astype(BF16), ffn_w_out[1].astype(BF16), latent_only=True)
```

```python
import functools
import math

import jax
import jax.numpy as jnp
from jax import lax
from jax.experimental import pallas as pl
from jax.experimental.pallas import tpu as pltpu

LANES = 128
MXU_TILE = 256
D_MODEL = 1024
HEAD_DIM = 64
CTX_LEN = 256
GRID_W = 64
WINDOW = 128
DEPTH = 2
ROPE_THETA = 10000.0
EPS = 1e-6
NEG_INF = -1e30
LOG2E = 1.4426950408889634
Q_SCALE = HEAD_DIM ** -0.5 * LOG2E
MOD_ROWS = 8
ROW_TILE = 256
MOD_COL_TILE = 1536
KEY_CHUNK = 256
QUERY_TILE = 256
DIFF_QUERY_TILES = 2
DIFF_HEADS = 2
FLASH_BUFFERS = 4
WINDOW_BUFFERS = 2
SOFTMAX_ROWS = 64
SUM_ROWS = 16
VMEM_LIMIT = 56 * 1024 * 1024

F32 = jnp.float32
BF16 = jnp.bfloat16


def _nt_dot(a, b):
    return lax.dot_general(a, b, (((1,), (1,)), ((), ())), preferred_element_type=F32)


def _dot(a, b):
    return jnp.dot(a, b, preferred_element_type=F32)


def _params(sem):
    sem = ("parallel",) * len(sem)
    return pltpu.CompilerParams(dimension_semantics=sem, vmem_limit_bytes=VMEM_LIMIT)


def _mod_kernel(cc_ref, w_ref, b_ref, o_ref):
    a = cc_ref[...]
    a = a / (1.0 + jnp.exp(-a))
    o_ref[0] = _dot(a.astype(BF16), w_ref[0].astype(BF16)) + b_ref[0]


def _modulation(cc, mod_w, mod_b):
    depth, d, n = mod_w.shape
    tn = MOD_COL_TILE
    return pl.pallas_call(
        _mod_kernel,
        grid=(depth, n // tn),
        in_specs=[
            pl.BlockSpec((MOD_ROWS, d), lambda l, j: (0, 0)),
            pl.BlockSpec((1, d, tn), lambda l, j: (l, 0, j)),
            pl.BlockSpec((1, 1, tn), lambda l, j: (l, 0, j)),
        ],
        out_specs=pl.BlockSpec((1, MOD_ROWS, tn), lambda l, j: (l, 0, j)),
        out_shape=jax.ShapeDtypeStruct((depth, MOD_ROWS, n), F32),
        compiler_params=_params(("arbitrary", "arbitrary")),
        name="modulation",
    )(cc, mod_w, mod_b.reshape(depth, 1, n))


def _modulated_norm(x, nw, shift, scale):
    ms = jnp.mean(x * x, axis=-1, keepdims=True)
    return (x * lax.rsqrt(ms + EPS) * nw) * (1.0 + scale) + shift


def _stream_specs(xs, tm, off=0):
    if len(xs) == 1:
        return [pl.BlockSpec((1, tm, xs[0].shape[2]), lambda bi, i: (bi, i + off, 0))]
    assert tm == CTX_LEN and off == 0
    d = xs[0].shape[2]
    return [pl.BlockSpec((1, tm, d), lambda bi, i: (bi, 0, 0)),
            pl.BlockSpec((1, tm, d), lambda bi, i: (bi, jnp.maximum(i - 1, 0), 0))]


def _stream_tile(x_refs):
    if len(x_refs) == 1:
        return x_refs[0][0]
    return jnp.where(pl.program_id(1) == 0, x_refs[0][0], x_refs[1][0])


def _inproj_kernel(*refs, n_q, n_k, n_v, d_v, k_pairs):
    _inproj_body(_stream_tile(refs[:-11]), *refs[-11:], n_q=n_q, n_k=n_k, n_v=n_v, d_v=d_v, k_pairs=k_pairs)


def _inproj_body(x, mod_ref, nw_ref, w_ref, wvt_ref, g_ref, cos_ref, sin_ref, bd_ref, q_ref, k_ref, vt_ref,
                 *, n_q, n_k, n_v, d_v, k_pairs):
    tm = q_ref.shape[1]
    h = _modulated_norm(x, nw_ref[...], mod_ref[0, 0, 0:1, :], mod_ref[0, 0, 1:2, :])
    hb = h.astype(BF16)
    y = _dot(hb, w_ref[...])
    cos = cos_ref[...]
    sin = sin_ref[...]
    lane = lax.broadcasted_iota(jnp.int32, (tm, LANES), 1)
    first_half = (lane % HEAD_DIM) < (HEAD_DIM // 2)
    n_norm = n_q + n_k * HEAD_DIM
    for c in range(n_norm // MXU_TILE):
        yc = y[:, MXU_TILE * c:MXU_TILE * (c + 1)]
        ss = _dot((yc * yc).astype(BF16), bd_ref[...])
        z = yc * lax.rsqrt(ss * (1.0 / HEAD_DIM) + EPS) * g_ref[:, MXU_TILE * c:MXU_TILE * (c + 1)]
        for half in range(MXU_TILE // LANES):
            zc = z[:, LANES * half:LANES * (half + 1)]
            partner = jnp.where(first_half, pltpu.roll(zc, LANES - HEAD_DIM // 2, 1),
                                pltpu.roll(zc, HEAD_DIM // 2, 1))
            o = zc * cos + partner * sin
            col = MXU_TILE * c + LANES * half
            if col < n_q:
                q_ref[0, :, col:col + LANES] = (o * Q_SCALE).astype(BF16)
            else:
                kh = (col - n_q) // HEAD_DIM
                if k_pairs:
                    k_ref[0, kh // 2] = o.astype(BF16)
                else:
                    k_ref[0, kh] = o[:, :HEAD_DIM].astype(BF16)
                    k_ref[0, kh + 1] = o[:, HEAD_DIM:].astype(BF16)
    vt = _nt_dot(wvt_ref[...], hb).astype(BF16)
    ones_rows = (lax.broadcasted_iota(jnp.int32, (SUM_ROWS, tm), 0) == 0).astype(BF16)
    for hv in range(n_v):
        vt_ref[0, hv, 0:d_v, :] = vt[d_v * hv:d_v * (hv + 1), :]
        vt_ref[0, hv, d_v:d_v + SUM_ROWS, :] = ones_rows


def _inproj(xs, mod_tab, nw, w, wvt, gains, cos_t, sin_t, bd, *, n_k, n_v, d_v, k_pairs=False):
    b, d = xs[0].shape[0], xs[0].shape[2]
    tb = sum(a.shape[1] for a in xs)
    n_q = D_MODEL
    n_in = w.shape[1]
    tm = ROW_TILE
    kern = functools.partial(_inproj_kernel, n_q=n_q, n_k=n_k, n_v=n_v, d_v=d_v, k_pairs=k_pairs)
    k_blocks, k_width = (n_k // 2, 2 * HEAD_DIM) if k_pairs else (n_k, HEAD_DIM)
    in_specs, out_specs, out_shape = _inproj_specs(b, tb, d, w, wvt, gains, k_blocks, k_width, n_v, d_v)
    return pl.pallas_call(
        kern,
        grid=(b, tb // tm),
        in_specs=_stream_specs(xs, tm) + in_specs,
        out_specs=out_specs,
        out_shape=out_shape,
        compiler_params=_params(("arbitrary", "arbitrary")),
        name="inproj",
    )(*xs, mod_tab, nw, w, wvt, gains, cos_t, sin_t, bd)


def _resident(shape):
    return pl.BlockSpec(shape, lambda bi, i: (0,) * len(shape), pipeline_mode=pl.Buffered(1))


def _inproj_specs(b, tb, d, w, wvt, gains, k_blocks, k_width, n_v, d_v):
    tm = ROW_TILE
    in_specs = [
        pl.BlockSpec((1, 1, MOD_ROWS, d), lambda bi, i: (bi, jnp.minimum(i, 1), 0, 0)),
        _resident((1, d)), _resident(w.shape), _resident(wvt.shape), _resident(gains.shape),
        pl.BlockSpec((tm, LANES), lambda bi, i: (i, 0)),
        pl.BlockSpec((tm, LANES), lambda bi, i: (i, 0)),
        _resident((MXU_TILE, MXU_TILE)),
    ]
    out_specs = [
        pl.BlockSpec((1, tm, D_MODEL), lambda bi, i: (bi, i, 0)),
        pl.BlockSpec((1, k_blocks, tm, k_width), lambda bi, i: (bi, 0, i, 0)),
        pl.BlockSpec((1, n_v, d_v + SUM_ROWS, tm), lambda bi, i: (bi, 0, 0, i)),
    ]
    out_shape = [
        jax.ShapeDtypeStruct((b, tb, D_MODEL), BF16),
        jax.ShapeDtypeStruct((b, k_blocks, tb, k_width), BF16),
        jax.ShapeDtypeStruct((b, n_v, d_v + SUM_ROWS, tb), BF16),
    ]
    return in_specs, out_specs, out_shape


def _softmax_step(s_ref, p_ref, m, chunk_max):
    m_new = chunk_max if m is None else jnp.maximum(m, chunk_max)
    for r in range(0, s_ref.shape[0], SOFTMAX_ROWS):
        p_ref[r:r + SOFTMAX_ROWS, :] = jnp.exp2(s_ref[r:r + SOFTMAX_ROWS, :] - m_new).astype(BF16)
    return m_new, (None if m is None else jnp.exp2(m - m_new))


class _Stream:
    def __init__(self, chunks, score, vt_slice, s_bufs, p_bufs):
        self.chunks, self.score, self.vt_slice = chunks, score, vt_slice
        self.s_bufs, self.p_bufs = s_bufs, p_bufs
        self.m = self.acc = None
        self.cmax = {}

    def buf(self, refs, c):
        return refs[c % len(refs)].at[0:self.chunks[c][1], :]

    def write_scores(self, c):
        if c < len(self.chunks):
            s = self.score(*self.chunks[c])
            self.buf(self.s_bufs, c)[...] = s
            self.cmax[c] = jnp.max(s, axis=0, keepdims=True)

    def step(self, c):
        if c >= len(self.chunks):
            return
        self.write_scores(c + 2)
        lo, size, _ = self.chunks[c]
        p_ref = self.buf(self.p_bufs, c)
        self.m, alpha = _softmax_step(self.buf(self.s_bufs, c), p_ref, self.m, self.cmax.pop(c))
        pv = _dot(self.vt_slice(lo, size), p_ref[...])
        self.acc = pv if alpha is None else alpha * self.acc + pv


def _run_streams(streams):
    for c in range(2):
        for st in streams:
            st.write_scores(c)
    for c in range(max(len(st.chunks) for st in streams)):
        for st in streams:
            st.step(c)


def _key_chunks(n_keys, tk):
    return [(0, CTX_LEN, "ctx")] + [(lo, tk, "lat") for lo in range(CTX_LEN, n_keys, tk)]


def _stream_scratch(n_keys, nq, n_streams, n_bufs):
    n = n_streams * n_bufs
    return [pltpu.VMEM((n_keys, nq), F32)] * n + [pltpu.VMEM((n_keys, nq), BF16)] * n


def _split_scratch(scratch, n_streams, n_bufs):
    n = n_streams * n_bufs
    return [(scratch[j * n_bufs:(j + 1) * n_bufs], scratch[n + j * n_bufs:n + (j + 1) * n_bufs])
            for j in range(n_streams)]


def _layer0_attention_kernel(qa_ref, qb_ref, ka_ref, vta_ref, kb_ref, vtb_ref, sink_ref, oa_ref, ob_ref,
                             *scratch, tk):
    tq = qa_ref.shape[1]
    tb = ka_ref.shape[2]
    span = tq + 2 * WINDOW
    t = pl.program_id(2)
    n_dense = 2 * 4 * FLASH_BUFFERS
    dense_bufs = _split_scratch(scratch[:n_dense], 4, FLASH_BUFFERS)
    window_bufs = _split_scratch(scratch[n_dense:], 4, WINDOW_BUFFERS)

    def run(chunks):
        qa = qa_ref[0]
        ws = pl.multiple_of(jnp.clip(tq * t - WINDOW, 0, tb - span), LANES)
        kpos = ws - CTX_LEN + lax.broadcasted_iota(jnp.int32, (span, tq), 0)
        qpos = tq * t - CTX_LEN + lax.broadcasted_iota(jnp.int32, (span, tq), 1)
        ok = (jnp.abs(qpos - kpos) <= WINDOW) & (kpos >= 0) & (qpos >= 0)

        def window_score(g):
            qh = qa[:, HEAD_DIM * g:HEAD_DIM * (g + 1)]

            def score(lo, size, tag):
                s = _nt_dot(ka_ref[0, 0, pl.ds(lo, size), :], qh)
                return jnp.where(ok, s, NEG_INF) if tag == "win" else s
            return score

        win = [_Stream([(0, CTX_LEN, "ctx"), (ws, span, "win")], window_score(g),
                       lambda lo, size: vta_ref[0, 0, :, pl.ds(lo, size)], *window_bufs[g])
               for g in range(4)]

        qb = qb_ref[0]
        qts = [qb[:, HEAD_DIM * g:HEAD_DIM * (g + 1)].astype(F32).T.astype(BF16) for g in range(4)]
        dense = [_Stream(chunks, lambda lo, size, tag, qt=qt: _dot(kb_ref[0, 0, lo:lo + size, :], qt),
                         lambda lo, size: vtb_ref[0, 0, :, lo:lo + size], *dense_bufs[g])
                 for g, qt in enumerate(qts)]

        _run_streams(win + dense)
        for g, st in enumerate(win):
            sink = sink_ref[0, :, tq * g:tq * (g + 1)]
            m_all = jnp.maximum(st.m, sink)
            scale = jnp.exp2(st.m - m_all)
            l = st.acc[HEAD_DIM:HEAD_DIM + 1, :] * scale + jnp.exp2(sink - m_all)
            oa_ref[0, g] = (st.acc[:HEAD_DIM, :] * scale / l).astype(BF16)
        for g, st in enumerate(dense):
            ob_ref[0, g] = (st.acc[:HEAD_DIM, :] / st.acc[HEAD_DIM:HEAD_DIM + 1, :]).astype(BF16)

    chunks = _key_chunks(tb, tk)
    is_latent = t * tq >= CTX_LEN
    pl.when(is_latent)(lambda: run(chunks))
    pl.when(jnp.logical_not(is_latent))(lambda: run(chunks[:1]))


def _layer0_attention(q, k, vt, sink_rows, *, tq, tk):
    b, tb, _ = q.shape
    assert (tb - CTX_LEN) % tk == 0

    def kv_specs(first_head):
        return [pl.BlockSpec((1, 1, tb, HEAD_DIM), lambda bi, h, i: (bi, first_head + h, 0, 0)),
                pl.BlockSpec((1, 1, HEAD_DIM + SUM_ROWS, tb), lambda bi, h, i: (bi, first_head + h, 0, 0))]

    out = jax.ShapeDtypeStruct((b, 8, HEAD_DIM, tb), BF16)
    return pl.pallas_call(
        functools.partial(_layer0_attention_kernel, tk=tk),
        grid=(b, 2, tb // tq),
        in_specs=[pl.BlockSpec((1, tq, 4 * HEAD_DIM), lambda bi, h, i: (bi, i, h)),
                  pl.BlockSpec((1, tq, 4 * HEAD_DIM), lambda bi, h, i: (bi, i, 2 + h))]
        + kv_specs(0) + kv_specs(2) + [pl.BlockSpec((1, 1, 4 * tq), lambda bi, h, i: (h, 0, 0))],
        out_specs=[pl.BlockSpec((1, 4, HEAD_DIM, tq), lambda bi, h, i: (bi, h, 0, i))] * 2,
        out_shape=[out, out],
        scratch_shapes=(_stream_scratch(tk, tq, 4, FLASH_BUFFERS)
                        + _stream_scratch(tq + 2 * WINDOW, tq, 4, WINDOW_BUFFERS)),
        compiler_params=_params(("arbitrary", "arbitrary", "arbitrary")),
        name="layer0_attention",
    )(q, q, k, vt, k, vt, sink_rows)


def _diff_kernel(qa_ref, qb_ref, k_ref, vt_ref, lam_ref, subw_ref, o_ref, *scratch, tk, lam_init):
    tb = k_ref.shape[2]
    d_v = 2 * HEAD_DIM
    chunks = _key_chunks(tb, tk)
    bufs = _split_scratch(scratch, 4 * DIFF_HEADS, FLASH_BUFFERS)
    lane = lax.broadcasted_iota(jnp.int32, (qa_ref.shape[1], 2 * HEAD_DIM), 1)
    lam = (jnp.exp(jnp.sum(lam_ref[0:1, :] * lam_ref[1:2, :], axis=-1, keepdims=True))
           - jnp.exp(jnp.sum(lam_ref[2:3, :] * lam_ref[3:4, :], axis=-1, keepdims=True)) + lam_init)
    for hh in range(DIFF_HEADS):
        streams = []
        for a in range(2):
            for q_ref in (qa_ref, qb_ref):
                q = q_ref[0, :, 2 * HEAD_DIM * hh:2 * HEAD_DIM * (hh + 1)]
                qh = jnp.where(lane // HEAD_DIM == a, q, jnp.zeros_like(q))
                streams.append(_Stream(
                    chunks, lambda lo, size, tag, qh=qh, hh=hh: _nt_dot(k_ref[0, hh, lo:lo + size, :], qh),
                    lambda lo, size, hh=hh: vt_ref[0, hh, :, lo:lo + size], *bufs[4 * hh + len(streams)]))
        _run_streams(streams)
        ots = [st.acc[:d_v, :] / st.acc[d_v:d_v + 1, :] for st in streams]
        ots = [jnp.concatenate(ots[0:2], axis=1), jnp.concatenate(ots[2:4], axis=1)]
        o = ots[0] - lam * ots[1]
        ms = jnp.mean(o * o, axis=0, keepdims=True)
        o = (o * lax.rsqrt(ms + EPS) * subw_ref[...]) * (1.0 - lam_init)
        o_ref[0, hh] = o.astype(BF16)


def _diff_attention(q, k, vt, lam_vecs, subw, *, tq, tk, lam_init):
    b, tb, _ = q.shape
    n_lat = tb - CTX_LEN
    n_heads = vt.shape[1]
    d_v = vt.shape[2] - SUM_ROWS
    hq = tq // 2
    assert CTX_LEN % hq == 0
    q_off = CTX_LEN // hq
    return pl.pallas_call(
        functools.partial(_diff_kernel, tk=tk, lam_init=lam_init),
        grid=(b, n_heads // DIFF_HEADS, n_lat // tq),
        in_specs=[
            pl.BlockSpec((1, hq, 2 * HEAD_DIM * DIFF_HEADS), lambda bi, h, i: (bi, 2 * i + q_off, h)),
            pl.BlockSpec((1, hq, 2 * HEAD_DIM * DIFF_HEADS), lambda bi, h, i: (bi, 2 * i + 1 + q_off, h)),
            pl.BlockSpec((1, DIFF_HEADS, tb, 2 * HEAD_DIM), lambda bi, h, i: (bi, h, 0, 0)),
            pl.BlockSpec((1, DIFF_HEADS, d_v + SUM_ROWS, tb), lambda bi, h, i: (bi, h, 0, 0)),
            pl.BlockSpec((4, HEAD_DIM), lambda bi, h, i: (0, 0)),
            pl.BlockSpec((d_v, 1), lambda bi, h, i: (0, 0)),
        ],
        out_specs=pl.BlockSpec((1, DIFF_HEADS, d_v, tq), lambda bi, h, i: (bi, h, 0, i)),
        out_shape=jax.ShapeDtypeStruct((b, n_heads, d_v, n_lat), BF16),
        scratch_shapes=_stream_scratch(tk, tq // 2, 4 * DIFF_HEADS, FLASH_BUFFERS),
        compiler_params=_params(("arbitrary", "arbitrary", "arbitrary")),
        name="diff_attention",
    )(q, q, k, vt, lam_vecs, subw)


def _post_kernel(*refs, n_attn):
    n_x = len(refs) - n_attn - 6
    out_ref = refs[-1]
    out_ref[0] = _post_body(_stream_tile(refs[:n_x]), refs[n_x], refs[n_x + 1:n_x + 1 + n_attn], *refs[-5:-1])


def _post_inproj_kernel(*refs, n_attn, n_x, inproj_cfg):
    n_post = n_x + 1 + n_attn + 4
    x2 = _post_body(_stream_tile(refs[:n_x]), refs[n_x], refs[n_x + 1:n_x + 1 + n_attn], *refs[n_post - 4:n_post])
    out_ref = refs[n_post + 8]
    out_ref[0] = x2
    _inproj_body(x2, *refs[n_post:n_post + 8], *refs[n_post + 9:], **inproj_cfg)


def _post_body(x, mod_ref, o_refs, wo_ref, nw_ref, wi_ref, wf_ref):
    n_attn = len(o_refs)
    mod = mod_ref[0, 0]
    kw = wo_ref.shape[0] // n_attn
    a = None
    for j, o_ref in enumerate(o_refs):
        ot = o_ref[0].reshape(kw, o_ref.shape[3])
        part = lax.dot_general(ot, wo_ref[kw * j:kw * (j + 1), :], (((0,), (0,)), ((), ())),
                               preferred_element_type=F32)
        a = part if a is None else a + part
    x1 = x + mod[2:3, :] * a
    h = _modulated_norm(x1, nw_ref[...], mod[3:4, :], mod[4:5, :])
    u = _dot(h.astype(BF16), wi_ref[...])
    f = wf_ref.shape[0]
    gate = u[:, :f]
    act = (gate / (1.0 + jnp.exp(-gate))) * u[:, f:]
    y = _dot(act.astype(BF16), wf_ref[...])
    return x1 + mod[5:6, :] * y


def _post(xs, mod_tab, attn_outs, wo, nw, wi, wf, *, latent_only, next_inproj=None):
    b, d = xs[0].shape[0], xs[0].shape[2]
    tb = sum(a.shape[1] for a in xs)
    tm = ROW_TILE
    off = CTX_LEN // tm if latent_only else 0
    n_rows = tb - CTX_LEN if latent_only else tb
    n_attn = len(attn_outs)

    in_specs = _stream_specs(xs, tm, off) + [
        pl.BlockSpec((1, 1, MOD_ROWS, d), lambda bi, i: (bi, jnp.minimum(i + off, 1), 0, 0)),
    ]
    for o in attn_outs:
        in_specs.append(pl.BlockSpec((1, o.shape[1], o.shape[2], tm), lambda bi, i: (bi, 0, 0, i)))
    in_specs += [_resident(wo.shape), _resident(nw.shape), _resident(wi.shape), _resident(wf.shape)]
    x_spec = pl.BlockSpec((1, tm, d), lambda bi, i: (bi, i, 0))
    x_shape = jax.ShapeDtypeStruct((b, n_rows, d), F32)
    if next_inproj is not None:
        assert not latent_only
        *operands, cfg = next_inproj
        k_blocks, k_width = (cfg["n_k"] // 2, 2 * HEAD_DIM) if cfg["k_pairs"] else (cfg["n_k"], HEAD_DIM)
        p_in, p_out, p_shape = _inproj_specs(b, tb, d, operands[2], operands[3], operands[4], k_blocks, k_width,
                                             cfg["n_v"], cfg["d_v"])
        return pl.pallas_call(
            functools.partial(_post_inproj_kernel, n_attn=n_attn, n_x=len(xs), inproj_cfg=dict(cfg, n_q=D_MODEL)),
            grid=(b, n_rows // tm),
            in_specs=in_specs + p_in,
            out_specs=[x_spec] + p_out,
            out_shape=[x_shape] + p_shape,
            compiler_params=_params(("arbitrary", "arbitrary")),
            name="post_inproj",
        )(*xs, mod_tab, *attn_outs, wo, nw, wi, wf, *operands)
    return pl.pallas_call(
        functools.partial(_post_kernel, n_attn=n_attn),
        grid=(b, n_rows // tm),
        in_specs=in_specs,
        out_specs=x_spec,
        out_shape=x_shape,
        compiler_params=_params(("arbitrary", "arbitrary")),
        name="post",
    )(*xs, mod_tab, *attn_outs, wo, nw, wi, wf)


def _deinterleave_perm(n_heads):
    one = jnp.concatenate([jnp.arange(0, HEAD_DIM, 2), jnp.arange(1, HEAD_DIM, 2)])
    return (jnp.arange(n_heads)[:, None] * HEAD_DIM + one[None, :]).reshape(-1)


def _rope_tables(n_lat):
    rows = n_lat // GRID_W
    row = jnp.repeat(jnp.arange(rows, dtype=F32), GRID_W)
    col = jnp.tile(jnp.arange(GRID_W, dtype=F32), rows)
    n_freq = HEAD_DIM // 4
    inv = ROPE_THETA ** (-jnp.arange(n_freq, dtype=F32) / n_freq)
    ang = jnp.concatenate([row[:, None] * inv, col[:, None] * inv], axis=-1)
    cos, sin = jnp.cos(ang), jnp.sin(ang)
    cos = jnp.concatenate([jnp.ones((CTX_LEN, HEAD_DIM // 2), F32), cos], axis=0)
    sin = jnp.concatenate([jnp.zeros((CTX_LEN, HEAD_DIM // 2), F32), sin], axis=0)
    cos_t = jnp.tile(jnp.concatenate([cos, cos], axis=-1), (1, 2))
    sin_t = jnp.tile(jnp.concatenate([-sin, sin], axis=-1), (1, 2))
    return cos_t, sin_t


def _gain_row(parts):
    one = _deinterleave_perm(1)
    return jnp.concatenate([jnp.tile(g[one], n) for g, n in parts])[None, :].astype(F32)


def kernel(x, c, ctx, c_ctx, mod_w, mod_b, norm_mix_w, norm_ffn_w, ev_w_in, ev_w_out, ev_qn_a, ev_kn_a,
           ev_qn_b, ev_kn_b, ev_sink_a, od_w_in, od_w_out, od_qn, od_kn, od_lq1, od_lk1, od_lq2, od_lk2,
           od_subln, ffn_w_in, ffn_w_out):
    b, n_lat, d = x.shape
    assert d == D_MODEL and ctx.shape[1] == CTX_LEN and b < MOD_ROWS
    assert n_lat % (DIFF_QUERY_TILES * QUERY_TILE) == 0 and n_lat % KEY_CHUNK == 0 and n_lat % GRID_W == 0
    hd = HEAD_DIM

    cc = jnp.zeros((MOD_ROWS, d), F32).at[:b].set(c).at[b].set(c_ctx)
    mod = _modulation(cc, mod_w, mod_b).reshape(DEPTH, MOD_ROWS, 6, d)
    mod_lat = mod[:, :b]
    mod_ctx = jnp.broadcast_to(mod[:, b:b + 1], mod_lat.shape)
    mod_tab = jnp.stack([mod_ctx, mod_lat], axis=2)
    mod_tab = jnp.pad(mod_tab, ((0, 0), (0, 0), (0, 0), (0, MOD_ROWS - 6), (0, 0)))

    cos_t, sin_t = _rope_tables(n_lat)
    bd = jnp.kron(jnp.eye(MXU_TILE // hd, dtype=F32), jnp.ones((hd, hd), F32)).astype(BF16)

    w = ev_w_in[0]
    widths = [8 * hd, 2 * hd, 2 * hd, 8 * hd, 2 * hd, 2 * hd]
    qa, ka, va, qb, kb, vb = jnp.split(w, [sum(widths[:n]) for n in range(1, 6)], axis=1)
    p8, p2 = _deinterleave_perm(8), _deinterleave_perm(2)
    w0 = jnp.concatenate([qa[:, p8], qb[:, p8], ka[:, p2], kb[:, p2]], axis=1).astype(BF16)
    wvt0 = jnp.concatenate([va, vb], axis=1).T.astype(BF16)
    g0 = _gain_row([(ev_qn_a[0], 8), (ev_qn_b[0], 8), (ev_kn_a[0], 2), (ev_kn_b[0], 2)])
    q0, k0, vt0 = _inproj((ctx, x), mod_tab[0], norm_mix_w[0][None, :], w0, wvt0, g0, cos_t, sin_t, bd,
                          n_k=4, n_v=4, d_v=hd)
    sink_rows = jnp.repeat(ev_sink_a[0].astype(F32) * LOG2E, QUERY_TILE).reshape(2, 1, 4 * QUERY_TILE)
    o_a, o_b = _layer0_attention(q0, k0, vt0, sink_rows, tq=QUERY_TILE, tk=KEY_CHUNK)
    w = od_w_in[0]
    p16 = _deinterleave_perm(16)
    n_qk = 16 * hd
    w1 = jnp.concatenate([w[:, :n_qk][:, p16], w[:, n_qk:2 * n_qk][:, p16]], axis=1).astype(BF16)
    wvt1 = w[:, 2 * n_qk:].T.astype(BF16)
    g1 = _gain_row([(od_qn[0], 16), (od_kn[0], 16)])
    xs, q1, k1, vt1 = _post((ctx, x), mod_tab[0], [o_a, o_b], ev_w_out[0].astype(BF16), norm_ffn_w[0][None, :],
                            ffn_w_in[0].astype(BF16), ffn_w_out[0].astype(BF16), latent_only=False,
                            next_inproj=(mod_tab[1], norm_mix_w[1][None, :], w1, wvt1, g1, cos_t, sin_t, bd,
                                         dict(n_k=16, n_v=8, d_v=2 * hd, k_pairs=True)))
    lam_init = 0.8 - 0.6 * math.exp(-0.3 * 1)
    lam_vecs = jnp.stack([od_lq1[0], od_lk1[0], od_lq2[0], od_lk2[0]]).astype(F32)
    o_c = _diff_attention(q1, k1, vt1, lam_vecs, od_subln[0].astype(F32)[:, None],
                          tq=DIFF_QUERY_TILES * QUERY_TILE, tk=KEY_CHUNK, lam_init=lam_init)
    return _post((xs,), mod_tab[1], [o_c], od_w_out[0].astype(BF16), norm_ffn_w[1][None, :],
                 ffn_w_in[1].astype(BF16), ffn_w_out[1].astype(BF16), latent_only=True)
```

```python
import functools
import math

import jax
import jax.numpy as jnp
from jax import lax
from jax.experimental import pallas as pl
from jax.experimental.pallas import tpu as pltpu

LANES = 128
MXU_TILE = 256
D_MODEL = 1024
HEAD_DIM = 64
CTX_LEN = 256
GRID_W = 64
WINDOW = 128
DEPTH = 2
ROPE_THETA = 10000.0
EPS = 1e-6
NEG_INF = -1e30
LOG2E = 1.4426950408889634
Q_SCALE = HEAD_DIM ** -0.5 * LOG2E
MOD_ROWS = 8
ROW_TILE = 256
MOD_COL_TILE = 1536
KEY_CHUNK = 256
QUERY_TILE = 256
DIFF_QUERY_TILES = 2
DIFF_HEADS = 4
DIFF_SCRATCH_SETS = 2
FLASH_BUFFERS = 4
WINDOW_BUFFERS = 2
SOFTMAX_ROWS = 64
SUM_ROWS = 16
VMEM_LIMIT = 56 * 1024 * 1024

F32 = jnp.float32
BF16 = jnp.bfloat16


def _nt_dot(a, b):
    return lax.dot_general(a, b, (((1,), (1,)), ((), ())), preferred_element_type=F32)


def _dot(a, b):
    return jnp.dot(a, b, preferred_element_type=F32)


def _params(sem):
    sem = ("parallel",) * len(sem)
    return pltpu.CompilerParams(dimension_semantics=sem, vmem_limit_bytes=VMEM_LIMIT)


def _mod_kernel(cc_ref, w_ref, b_ref, o_ref):
    a = cc_ref[...]
    a = a / (1.0 + jnp.exp(-a))
    o_ref[0] = _dot(a.astype(BF16), w_ref[0].astype(BF16)) + b_ref[0]


def _modulation(cc, mod_w, mod_b):
    depth, d, n = mod_w.shape
    tn = MOD_COL_TILE
    return pl.pallas_call(
        _mod_kernel,
        grid=(depth, n // tn),
        in_specs=[
            pl.BlockSpec((MOD_ROWS, d), lambda l, j: (0, 0)),
            pl.BlockSpec((1, d, tn), lambda l, j: (l, 0, j)),
            pl.BlockSpec((1, 1, tn), lambda l, j: (l, 0, j)),
        ],
        out_specs=pl.BlockSpec((1, MOD_ROWS, tn), lambda l, j: (l, 0, j)),
        out_shape=jax.ShapeDtypeStruct((depth, MOD_ROWS, n), F32),
        compiler_params=_params(("arbitrary", "arbitrary")),
        name="modulation",
    )(cc, mod_w, mod_b.reshape(depth, 1, n))


def _modulated_norm(x, nw, shift, scale):
    ms = jnp.mean(x * x, axis=-1, keepdims=True)
    return (x * lax.rsqrt(ms + EPS) * nw) * (1.0 + scale) + shift


def _stream_specs(xs, tm, off=0):
    if len(xs) == 1:
        return [pl.BlockSpec((1, tm, xs[0].shape[2]), lambda bi, i: (bi, i + off, 0))]
    assert tm == CTX_LEN and off == 0
    d = xs[0].shape[2]
    return [pl.BlockSpec((1, tm, d), lambda bi, i: (bi, 0, 0)),
            pl.BlockSpec((1, tm, d), lambda bi, i: (bi, jnp.maximum(i - 1, 0), 0))]


def _stream_tile(x_refs):
    if len(x_refs) == 1:
        return x_refs[0][0]
    return jnp.where(pl.program_id(1) == 0, x_refs[0][0], x_refs[1][0])


def _inproj_kernel(*refs, n_q, n_k, n_v, d_v, k_pairs):
    _inproj_body(_stream_tile(refs[:-11]), *refs[-11:], n_q=n_q, n_k=n_k, n_v=n_v, d_v=d_v, k_pairs=k_pairs)


def _inproj_body(x, mod_ref, nw_ref, w_ref, wvt_ref, g_ref, cos_ref, sin_ref, bd_ref, q_ref, k_ref, vt_ref,
                 *, n_q, n_k, n_v, d_v, k_pairs):
    tm = q_ref.shape[1]
    h = _modulated_norm(x, nw_ref[...], mod_ref[0, 0, 0:1, :], mod_ref[0, 0, 1:2, :])
    hb = h.astype(BF16)
    y = _dot(hb, w_ref[...])
    cos = cos_ref[...]
    sin = sin_ref[...]
    lane = lax.broadcasted_iota(jnp.int32, (tm, LANES), 1)
    first_half = (lane % HEAD_DIM) < (HEAD_DIM // 2)
    n_norm = n_q + n_k * HEAD_DIM
    for c in range(n_norm // MXU_TILE):
        yc = y[:, MXU_TILE * c:MXU_TILE * (c + 1)]
        ss = _dot((yc * yc).astype(BF16), bd_ref[...])
        z = yc * lax.rsqrt(ss * (1.0 / HEAD_DIM) + EPS) * g_ref[:, MXU_TILE * c:MXU_TILE * (c + 1)]
        for half in range(MXU_TILE // LANES):
            zc = z[:, LANES * half:LANES * (half + 1)]
            partner = jnp.where(first_half, pltpu.roll(zc, LANES - HEAD_DIM // 2, 1),
                                pltpu.roll(zc, HEAD_DIM // 2, 1))
            o = zc * cos + partner * sin
            col = MXU_TILE * c + LANES * half
            if col < n_q:
                q_ref[0, :, col:col + LANES] = (o * Q_SCALE).astype(BF16)
            else:
                kh = (col - n_q) // HEAD_DIM
                if k_pairs:
                    k_ref[0, kh // 2] = o.astype(BF16)
                else:
                    k_ref[0, kh] = o[:, :HEAD_DIM].astype(BF16)
                    k_ref[0, kh + 1] = o[:, HEAD_DIM:].astype(BF16)
    vt = _nt_dot(wvt_ref[...], hb).astype(BF16)
    ones_rows = (lax.broadcasted_iota(jnp.int32, (SUM_ROWS, tm), 0) == 0).astype(BF16)
    for hv in range(n_v):
        vt_ref[0, hv, 0:d_v, :] = vt[d_v * hv:d_v * (hv + 1), :]
        vt_ref[0, hv, d_v:d_v + SUM_ROWS, :] = ones_rows


def _inproj(xs, mod_tab, nw, w, wvt, gains, cos_t, sin_t, bd, *, n_k, n_v, d_v, k_pairs=False):
    b, d = xs[0].shape[0], xs[0].shape[2]
    tb = sum(a.shape[1] for a in xs)
    n_q = D_MODEL
    n_in = w.shape[1]
    tm = ROW_TILE
    kern = functools.partial(_inproj_kernel, n_q=n_q, n_k=n_k, n_v=n_v, d_v=d_v, k_pairs=k_pairs)
    k_blocks, k_width = (n_k // 2, 2 * HEAD_DIM) if k_pairs else (n_k, HEAD_DIM)
    in_specs, out_specs, out_shape = _inproj_specs(b, tb, d, w, wvt, gains, k_blocks, k_width, n_v, d_v)
    return pl.pallas_call(
        kern,
        grid=(b, tb // tm),
        in_specs=_stream_specs(xs, tm) + in_specs,
        out_specs=out_specs,
        out_shape=out_shape,
        compiler_params=_params(("arbitrary", "arbitrary")),
        name="inproj",
    )(*xs, mod_tab, nw, w, wvt, gains, cos_t, sin_t, bd)


def _resident(shape):
    return pl.BlockSpec(shape, lambda bi, i: (0,) * len(shape), pipeline_mode=pl.Buffered(1))


def _inproj_specs(b, tb, d, w, wvt, gains, k_blocks, k_width, n_v, d_v):
    tm = ROW_TILE
    in_specs = [
        pl.BlockSpec((1, 1, MOD_ROWS, d), lambda bi, i: (bi, jnp.minimum(i, 1), 0, 0)),
        _resident((1, d)), _resident(w.shape), _resident(wvt.shape), _resident(gains.shape),
        pl.BlockSpec((tm, LANES), lambda bi, i: (i, 0)),
        pl.BlockSpec((tm, LANES), lambda bi, i: (i, 0)),
        _resident((MXU_TILE, MXU_TILE)),
    ]
    out_specs = [
        pl.BlockSpec((1, tm, D_MODEL), lambda bi, i: (bi, i, 0)),
        pl.BlockSpec((1, k_blocks, tm, k_width), lambda bi, i: (bi, 0, i, 0)),
        pl.BlockSpec((1, n_v, d_v + SUM_ROWS, tm), lambda bi, i: (bi, 0, 0, i)),
    ]
    out_shape = [
        jax.ShapeDtypeStruct((b, tb, D_MODEL), BF16),
        jax.ShapeDtypeStruct((b, k_blocks, tb, k_width), BF16),
        jax.ShapeDtypeStruct((b, n_v, d_v + SUM_ROWS, tb), BF16),
    ]
    return in_specs, out_specs, out_shape


def _softmax_step(s_ref, p_ref, m, chunk_max):
    m_new = chunk_max if m is None else jnp.maximum(m, chunk_max)
    for r in range(0, s_ref.shape[0], SOFTMAX_ROWS):
        p_ref[r:r + SOFTMAX_ROWS, :] = jnp.exp2(s_ref[r:r + SOFTMAX_ROWS, :] - m_new).astype(BF16)
    return m_new, (None if m is None else jnp.exp2(m - m_new))


class _Stream:
    def __init__(self, chunks, score, vt_slice, s_bufs, p_bufs):
        self.chunks, self.score, self.vt_slice = chunks, score, vt_slice
        self.s_bufs, self.p_bufs = s_bufs, p_bufs
        self.m = self.acc = None
        self.cmax = {}

    def buf(self, refs, c):
        return refs[c % len(refs)].at[0:self.chunks[c][1], :]

    def write_scores(self, c):
        if c < len(self.chunks):
            s = self.score(*self.chunks[c])
            self.buf(self.s_bufs, c)[...] = s
            self.cmax[c] = jnp.max(s, axis=0, keepdims=True)

    def step(self, c):
        if c >= len(self.chunks):
            return
        self.write_scores(c + 2)
        lo, size, _ = self.chunks[c]
        p_ref = self.buf(self.p_bufs, c)
        self.m, alpha = _softmax_step(self.buf(self.s_bufs, c), p_ref, self.m, self.cmax.pop(c))
        pv = _dot(self.vt_slice(lo, size), p_ref[...])
        self.acc = pv if alpha is None else alpha * self.acc + pv


def _run_streams(streams):
    for c in range(2):
        for st in streams:
            st.write_scores(c)
    for c in range(max(len(st.chunks) for st in streams)):
        for st in streams:
            st.step(c)


def _key_chunks(n_keys, tk):
    return [(0, CTX_LEN, "ctx")] + [(lo, tk, "lat") for lo in range(CTX_LEN, n_keys, tk)]


def _stream_scratch(n_keys, nq, n_streams, n_bufs):
    n = n_streams * n_bufs
    return [pltpu.VMEM((n_keys, nq), F32)] * n + [pltpu.VMEM((n_keys, nq), BF16)] * n


def _split_scratch(scratch, n_streams, n_bufs):
    n = n_streams * n_bufs
    return [(scratch[j * n_bufs:(j + 1) * n_bufs], scratch[n + j * n_bufs:n + (j + 1) * n_bufs])
            for j in range(n_streams)]


def _layer0_attention_kernel(qa_ref, qb_ref, ka_ref, vta_ref, kb_ref, vtb_ref, sink_ref, oa_ref, ob_ref,
                             *scratch, tk):
    tq = qa_ref.shape[1]
    tb = ka_ref.shape[2]
    span = tq + 2 * WINDOW
    t = pl.program_id(2)
    n_dense = 2 * 4 * FLASH_BUFFERS
    dense_bufs = _split_scratch(scratch[:n_dense], 4, FLASH_BUFFERS)
    window_bufs = _split_scratch(scratch[n_dense:], 4, WINDOW_BUFFERS)

    def run(chunks):
        qa = qa_ref[0]
        ws = pl.multiple_of(jnp.clip(tq * t - WINDOW, 0, tb - span), LANES)
        kpos = ws - CTX_LEN + lax.broadcasted_iota(jnp.int32, (span, tq), 0)
        qpos = tq * t - CTX_LEN + lax.broadcasted_iota(jnp.int32, (span, tq), 1)
        ok = (jnp.abs(qpos - kpos) <= WINDOW) & (kpos >= 0) & (qpos >= 0)

        def window_score(g):
            qh = qa[:, HEAD_DIM * g:HEAD_DIM * (g + 1)]

            def score(lo, size, tag):
                s = _nt_dot(ka_ref[0, 0, pl.ds(lo, size), :], qh)
                return jnp.where(ok, s, NEG_INF) if tag == "win" else s
            return score

        win = [_Stream([(0, CTX_LEN, "ctx"), (ws, span, "win")], window_score(g),
                       lambda lo, size: vta_ref[0, 0, :, pl.ds(lo, size)], *window_bufs[g])
               for g in range(4)]

        qb = qb_ref[0]
        qts = [qb[:, HEAD_DIM * g:HEAD_DIM * (g + 1)].astype(F32).T.astype(BF16) for g in range(4)]
        dense = [_Stream(chunks, lambda lo, size, tag, qt=qt: _dot(kb_ref[0, 0, lo:lo + size, :], qt),
                         lambda lo, size: vtb_ref[0, 0, :, lo:lo + size], *dense_bufs[g])
                 for g, qt in enumerate(qts)]

        _run_streams(win + dense)
        for g, st in enumerate(win):
            sink = sink_ref[0, :, tq * g:tq * (g + 1)]
            m_all = jnp.maximum(st.m, sink)
            scale = jnp.exp2(st.m - m_all)
            l = st.acc[HEAD_DIM:HEAD_DIM + 1, :] * scale + jnp.exp2(sink - m_all)
            oa_ref[0, g] = (st.acc[:HEAD_DIM, :] * scale / l).astype(BF16)
        for g, st in enumerate(dense):
            ob_ref[0, g] = (st.acc[:HEAD_DIM, :] / st.acc[HEAD_DIM:HEAD_DIM + 1, :]).astype(BF16)

    chunks = _key_chunks(tb, tk)
    is_latent = t * tq >= CTX_LEN
    pl.when(is_latent)(lambda: run(chunks))
    pl.when(jnp.logical_not(is_latent))(lambda: run(chunks[:1]))


def _layer0_attention(q, k, vt, sink_rows, *, tq, tk):
    b, tb, _ = q.shape
    assert (tb - CTX_LEN) % tk == 0

    def kv_specs(first_head):
        return [pl.BlockSpec((1, 1, tb, HEAD_DIM), lambda bi, h, i: (bi, first_head + h, 0, 0)),
                pl.BlockSpec((1, 1, HEAD_DIM + SUM_ROWS, tb), lambda bi, h, i: (bi, first_head + h, 0, 0))]

    out = jax.ShapeDtypeStruct((b, 8, HEAD_DIM, tb), BF16)
    return pl.pallas_call(
        functools.partial(_layer0_attention_kernel, tk=tk),
        grid=(b, 2, tb // tq),
        in_specs=[pl.BlockSpec((1, tq, 4 * HEAD_DIM), lambda bi, h, i: (bi, i, h)),
                  pl.BlockSpec((1, tq, 4 * HEAD_DIM), lambda bi, h, i: (bi, i, 2 + h))]
        + kv_specs(0) + kv_specs(2) + [pl.BlockSpec((1, 1, 4 * tq), lambda bi, h, i: (h, 0, 0))],
        out_specs=[pl.BlockSpec((1, 4, HEAD_DIM, tq), lambda bi, h, i: (bi, h, 0, i))] * 2,
        out_shape=[out, out],
        scratch_shapes=(_stream_scratch(tk, tq, 4, FLASH_BUFFERS)
                        + _stream_scratch(tq + 2 * WINDOW, tq, 4, WINDOW_BUFFERS)),
        compiler_params=_params(("arbitrary", "arbitrary", "arbitrary")),
        name="layer0_attention",
    )(q, q, k, vt, k, vt, sink_rows)


def _diff_kernel(qa_ref, qb_ref, k_ref, vt_ref, lam_ref, subw_ref, o_ref, *scratch, tk, lam_init):
    tb = k_ref.shape[2]
    d_v = 2 * HEAD_DIM
    chunks = _key_chunks(tb, tk)
    bufs = _split_scratch(scratch, 4 * DIFF_SCRATCH_SETS, FLASH_BUFFERS)
    lane = lax.broadcasted_iota(jnp.int32, (qa_ref.shape[1], 2 * HEAD_DIM), 1)
    lam = (jnp.exp(jnp.sum(lam_ref[0:1, :] * lam_ref[1:2, :], axis=-1, keepdims=True))
           - jnp.exp(jnp.sum(lam_ref[2:3, :] * lam_ref[3:4, :], axis=-1, keepdims=True)) + lam_init)
    for hh in range(DIFF_HEADS):
        streams = []
        for a in range(2):
            for q_ref in (qa_ref, qb_ref):
                q = q_ref[0, :, 2 * HEAD_DIM * hh:2 * HEAD_DIM * (hh + 1)]
                qh = jnp.where(lane // HEAD_DIM == a, q, jnp.zeros_like(q))
                streams.append(_Stream(
                    chunks, lambda lo, size, tag, qh=qh, hh=hh: _nt_dot(k_ref[0, hh, lo:lo + size, :], qh),
                    lambda lo, size, hh=hh: vt_ref[0, hh, :, lo:lo + size], *bufs[4 * (hh % DIFF_SCRATCH_SETS) + len(streams)]))
        _run_streams(streams)
        ots = [st.acc[:d_v, :] / st.acc[d_v:d_v + 1, :] for st in streams]
        ots = [jnp.concatenate(ots[0:2], axis=1), jnp.concatenate(ots[2:4], axis=1)]
        o = ots[0] - lam * ots[1]
        ms = jnp.mean(o * o, axis=0, keepdims=True)
        o = (o * lax.rsqrt(ms + EPS) * subw_ref[...]) * (1.0 - lam_init)
        o_ref[0, hh] = o.astype(BF16)


def _diff_attention(q, k, vt, lam_vecs, subw, *, tq, tk, lam_init):
    b, tb, _ = q.shape
    n_lat = tb - CTX_LEN
    n_heads = vt.shape[1]
    d_v = vt.shape[2] - SUM_ROWS
    hq = tq // 2
    assert CTX_LEN % hq == 0
    q_off = CTX_LEN // hq
    return pl.pallas_call(
        functools.partial(_diff_kernel, tk=tk, lam_init=lam_init),
        grid=(b, n_heads // DIFF_HEADS, n_lat // tq),
        in_specs=[
            pl.BlockSpec((1, hq, 2 * HEAD_DIM * DIFF_HEADS), lambda bi, h, i: (bi, 2 * i + q_off, h)),
            pl.BlockSpec((1, hq, 2 * HEAD_DIM * DIFF_HEADS), lambda bi, h, i: (bi, 2 * i + 1 + q_off, h)),
            pl.BlockSpec((1, DIFF_HEADS, tb, 2 * HEAD_DIM), lambda bi, h, i: (bi, h, 0, 0)),
            pl.BlockSpec((1, DIFF_HEADS, d_v + SUM_ROWS, tb), lambda bi, h, i: (bi, h, 0, 0)),
            pl.BlockSpec((4, HEAD_DIM), lambda bi, h, i: (0, 0)),
            pl.BlockSpec((d_v, 1), lambda bi, h, i: (0, 0)),
        ],
        out_specs=pl.BlockSpec((1, DIFF_HEADS, d_v, tq), lambda bi, h, i: (bi, h, 0, i)),
        out_shape=jax.ShapeDtypeStruct((b, n_heads, d_v, n_lat), BF16),
        scratch_shapes=_stream_scratch(tk, tq // 2, 4 * DIFF_SCRATCH_SETS, FLASH_BUFFERS),
        compiler_params=_params(("arbitrary", "arbitrary", "arbitrary")),
        name="diff_attention",
    )(q, q, k, vt, lam_vecs, subw)


def _post_kernel(*refs, n_attn):
    n_x = len(refs) - n_attn - 6
    out_ref = refs[-1]
    out_ref[0] = _post_body(_stream_tile(refs[:n_x]), refs[n_x], refs[n_x + 1:n_x + 1 + n_attn], *refs[-5:-1])


def _post_inproj_kernel(*refs, n_attn, n_x, inproj_cfg):
    n_post = n_x + 1 + n_attn + 4
    x2 = _post_body(_stream_tile(refs[:n_x]), refs[n_x], refs[n_x + 1:n_x + 1 + n_attn], *refs[n_post - 4:n_post])
    out_ref = refs[n_post + 8]
    out_ref[0] = x2
    _inproj_body(x2, *refs[n_post:n_post + 8], *refs[n_post + 9:], **inproj_cfg)


def _post_body(x, mod_ref, o_refs, wo_ref, nw_ref, wi_ref, wf_ref):
    n_attn = len(o_refs)
    mod = mod_ref[0, 0]
    kw = wo_ref.shape[0] // n_attn
    a = None
    for j, o_ref in enumerate(o_refs):
        ot = o_ref[0].reshape(kw, o_ref.shape[3])
        part = lax.dot_general(ot, wo_ref[kw * j:kw * (j + 1), :], (((0,), (0,)), ((), ())),
                               preferred_element_type=F32)
        a = part if a is None else a + part
    x1 = x + mod[2:3, :] * a
    h = _modulated_norm(x1, nw_ref[...], mod[3:4, :], mod[4:5, :])
    u = _dot(h.astype(BF16), wi_ref[...])
    f = wf_ref.shape[0]
    gate = u[:, :f]
    act = (gate / (1.0 + jnp.exp(-gate))) * u[:, f:]
    y = _dot(act.astype(BF16), wf_ref[...])
    return x1 + mod[5:6, :] * y


def _post(xs, mod_tab, attn_outs, wo, nw, wi, wf, *, latent_only, next_inproj=None):
    b, d = xs[0].shape[0], xs[0].shape[2]
    tb = sum(a.shape[1] for a in xs)
    tm = ROW_TILE
    off = CTX_LEN // tm if latent_only else 0
    n_rows = tb - CTX_LEN if latent_only else tb
    n_attn = len(attn_outs)

    in_specs = _stream_specs(xs, tm, off) + [
        pl.BlockSpec((1, 1, MOD_ROWS, d), lambda bi, i: (bi, jnp.minimum(i + off, 1), 0, 0)),
    ]
    for o in attn_outs:
        in_specs.append(pl.BlockSpec((1, o.shape[1], o.shape[2], tm), lambda bi, i: (bi, 0, 0, i)))
    in_specs += [_resident(wo.shape), _resident(nw.shape), _resident(wi.shape), _resident(wf.shape)]
    x_spec = pl.BlockSpec((1, tm, d), lambda bi, i: (bi, i, 0))
    x_shape = jax.ShapeDtypeStruct((b, n_rows, d), F32)
    if next_inproj is not None:
        assert not latent_only
        *operands, cfg = next_inproj
        k_blocks, k_width = (cfg["n_k"] // 2, 2 * HEAD_DIM) if cfg["k_pairs"] else (cfg["n_k"], HEAD_DIM)
        p_in, p_out, p_shape = _inproj_specs(b, tb, d, operands[2], operands[3], operands[4], k_blocks, k_width,
                                             cfg["n_v"], cfg["d_v"])
        return pl.pallas_call(
            functools.partial(_post_inproj_kernel, n_attn=n_attn, n_x=len(xs), inproj_cfg=dict(cfg, n_q=D_MODEL)),
            grid=(b, n_rows // tm),
            in_specs=in_specs + p_in,
            out_specs=[x_spec] + p_out,
            out_shape=[x_shape] + p_shape,
            compiler_params=_params(("arbitrary", "arbitrary")),
            name="post_inproj",
        )(*xs, mod_tab, *attn_outs, wo, nw, wi, wf, *operands)
    return pl.pallas_call(
        functools.partial(_post_kernel, n_attn=n_attn),
        grid=(b, n_rows // tm),
        in_specs=in_specs,
        out_specs=x_spec,
        out_shape=x_shape,
        compiler_params=_params(("arbitrary", "arbitrary")),
        name="post",
    )(*xs, mod_tab, *attn_outs, wo, nw, wi, wf)


def _deinterleave_perm(n_heads):
    one = jnp.concatenate([jnp.arange(0, HEAD_DIM, 2), jnp.arange(1, HEAD_DIM, 2)])
    return (jnp.arange(n_heads)[:, None] * HEAD_DIM + one[None, :]).reshape(-1)


def _rope_tables(n_lat):
    rows = n_lat // GRID_W
    row = jnp.repeat(jnp.arange(rows, dtype=F32), GRID_W)
    col = jnp.tile(jnp.arange(GRID_W, dtype=F32), rows)
    n_freq = HEAD_DIM // 4
    inv = ROPE_THETA ** (-jnp.arange(n_freq, dtype=F32) / n_freq)
    ang = jnp.concatenate([row[:, None] * inv, col[:, None] * inv], axis=-1)
    cos, sin = jnp.cos(ang), jnp.sin(ang)
    cos = jnp.concatenate([jnp.ones((CTX_LEN, HEAD_DIM // 2), F32), cos], axis=0)
    sin = jnp.concatenate([jnp.zeros((CTX_LEN, HEAD_DIM // 2), F32), sin], axis=0)
    cos_t = jnp.tile(jnp.concatenate([cos, cos], axis=-1), (1, 2))
    sin_t = jnp.tile(jnp.concatenate([-sin, sin], axis=-1), (1, 2))
    return cos_t, sin_t


def _gain_row(parts):
    one = _deinterleave_perm(1)
    return jnp.concatenate([jnp.tile(g[one], n) for g, n in parts])[None, :].astype(F32)


def kernel(x, c, ctx, c_ctx, mod_w, mod_b, norm_mix_w, norm_ffn_w, ev_w_in, ev_w_out, ev_qn_a, ev_kn_a,
           ev_qn_b, ev_kn_b, ev_sink_a, od_w_in, od_w_out, od_qn, od_kn, od_lq1, od_lk1, od_lq2, od_lk2,
           od_subln, ffn_w_in, ffn_w_out):
    b, n_lat, d = x.shape
    assert d == D_MODEL and ctx.shape[1] == CTX_LEN and b < MOD_ROWS
    assert n_lat % (DIFF_QUERY_TILES * QUERY_TILE) == 0 and n_lat % KEY_CHUNK == 0 and n_lat % GRID_W == 0
    hd = HEAD_DIM

    cc = jnp.zeros((MOD_ROWS, d), F32).at[:b].set(c).at[b].set(c_ctx)
    mod = _modulation(cc, mod_w, mod_b).reshape(DEPTH, MOD_ROWS, 6, d)
    mod_lat = mod[:, :b]
    mod_ctx = jnp.broadcast_to(mod[:, b:b + 1], mod_lat.shape)
    mod_tab = jnp.stack([mod_ctx, mod_lat], axis=2)
    mod_tab = jnp.pad(mod_tab, ((0, 0), (0, 0), (0, 0), (0, MOD_ROWS - 6), (0, 0)))

    cos_t, sin_t = _rope_tables(n_lat)
    bd = jnp.kron(jnp.eye(MXU_TILE // hd, dtype=F32), jnp.ones((hd, hd), F32)).astype(BF16)

    w = ev_w_in[0]
    widths = [8 * hd, 2 * hd, 2 * hd, 8 * hd, 2 * hd, 2 * hd]
    qa, ka, va, qb, kb, vb = jnp.split(w, [sum(widths[:n]) for n in range(1, 6)], axis=1)
    p8, p2 = _deinterleave_perm(8), _deinterleave_perm(2)
    w0 = jnp.concatenate([qa[:, p8], qb[:, p8], ka[:, p2], kb[:, p2]], axis=1).astype(BF16)
    wvt0 = jnp.concatenate([va, vb], axis=1).T.astype(BF16)
    g0 = _gain_row([(ev_qn_a[0], 8), (ev_qn_b[0], 8), (ev_kn_a[0], 2), (ev_kn_b[0], 2)])
    q0, k0, vt0 = _inproj((ctx, x), mod_tab[0], norm_mix_w[0][None, :], w0, wvt0, g0, cos_t, sin_t, bd,
                          n_k=4, n_v=4, d_v=hd)
    sink_rows = jnp.repeat(ev_sink_a[0].astype(F32) * LOG2E, QUERY_TILE).reshape(2, 1, 4 * QUERY_TILE)
    o_a, o_b = _layer0_attention(q0, k0, vt0, sink_rows, tq=QUERY_TILE, tk=KEY_CHUNK)
    w = od_w_in[0]
    p16 = _deinterleave_perm(16)
    n_qk = 16 * hd
    w1 = jnp.concatenate([w[:, :n_qk][:, p16], w[:, n_qk:2 * n_qk][:, p16]], axis=1).astype(BF16)
    wvt1 = w[:, 2 * n_qk:].T.astype(BF16)
    g1 = _gain_row([(od_qn[0], 16), (od_kn[0], 16)])
    xs, q1, k1, vt1 = _post((ctx, x), mod_tab[0], [o_a, o_b], ev_w_out[0].astype(BF16), norm_ffn_w[0][None, :],
                            ffn_w_in[0].astype(BF16), ffn_w_out[0].astype(BF16), latent_only=False,
                            next_inproj=(mod_tab[1], norm_mix_w[1][None, :], w1, wvt1, g1, cos_t, sin_t, bd,
                                         dict(n_k=16, n_v=8, d_v=2 * hd, k_pairs=True)))
    lam_init = 0.8 - 0.6 * math.exp(-0.3 * 1)
    lam_vecs = jnp.stack([od_lq1[0], od_lk1[0], od_lq2[0], od_lk2[0]]).astype(F32)
    o_c = _diff_attention(q1, k1, vt1, lam_vecs, od_subln[0].astype(F32)[:, None],
                          tq=DIFF_QUERY_TILES * QUERY_TILE, tk=KEY_CHUNK, lam_init=lam_init)
    return _post((xs,), mod_tab[1], [o_c], od_w_out[0].astype(BF16), norm_ffn_w[1][None, :],
                 ffn_w_in[1].astype(BF16), ffn_w_out[1].astype(BF16), latent_only=True)
```

```python
import functools
import math

import jax
import jax.numpy as jnp
from jax import lax
from jax.experimental import pallas as pl
from jax.experimental.pallas import tpu as pltpu

LANES = 128
MXU_TILE = 256
D_MODEL = 1024
HEAD_DIM = 64
CTX_LEN = 256
GRID_W = 64
WINDOW = 128
DEPTH = 2
ROPE_THETA = 10000.0
EPS = 1e-6
NEG_INF = -1e30
LOG2E = 1.4426950408889634
Q_SCALE = HEAD_DIM ** -0.5 * LOG2E
MOD_ROWS = 8
ROW_TILE = 256
MOD_COL_TILE = 1536
KEY_CHUNK = 256
QUERY_TILE = 256
DIFF_QUERY_TILES = 2
DIFF_HEADS = 2
FLASH_BUFFERS = 4
WINDOW_BUFFERS = 2
SOFTMAX_ROWS = 64
SUM_ROWS = 16
VMEM_LIMIT = 56 * 1024 * 1024

F32 = jnp.float32
BF16 = jnp.bfloat16


def _nt_dot(a, b):
    return lax.dot_general(a, b, (((1,), (1,)), ((), ())), preferred_element_type=F32)


def _dot(a, b):
    return jnp.dot(a, b, preferred_element_type=F32)


def _params(sem):
    sem = ("parallel",) * len(sem)
    return pltpu.CompilerParams(dimension_semantics=sem, vmem_limit_bytes=VMEM_LIMIT)


def _mod_kernel(cc_ref, w_ref, b_ref, o_ref):
    a = cc_ref[...]
    a = a / (1.0 + jnp.exp(-a))
    o_ref[0] = _dot(a.astype(BF16), w_ref[0].astype(BF16)) + b_ref[0]


def _modulation(cc, mod_w, mod_b):
    depth, d, n = mod_w.shape
    tn = MOD_COL_TILE
    return pl.pallas_call(
        _mod_kernel,
        grid=(depth, n // tn),
        in_specs=[
            pl.BlockSpec((MOD_ROWS, d), lambda l, j: (0, 0)),
            pl.BlockSpec((1, d, tn), lambda l, j: (l, 0, j)),
            pl.BlockSpec((1, 1, tn), lambda l, j: (l, 0, j)),
        ],
        out_specs=pl.BlockSpec((1, MOD_ROWS, tn), lambda l, j: (l, 0, j)),
        out_shape=jax.ShapeDtypeStruct((depth, MOD_ROWS, n), F32),
        compiler_params=_params(("arbitrary", "arbitrary")),
        name="modulation",
    )(cc, mod_w, mod_b.reshape(depth, 1, n))


def _modulated_norm(x, nw, shift, scale):
    ms = jnp.mean(x * x, axis=-1, keepdims=True)
    return (x * lax.rsqrt(ms + EPS) * nw) * (1.0 + scale) + shift


def _stream_specs(xs, tm, off=0):
    if len(xs) == 1:
        return [pl.BlockSpec((1, tm, xs[0].shape[2]), lambda bi, i: (bi, i + off, 0))]
    assert tm == CTX_LEN and off == 0
    d = xs[0].shape[2]
    return [pl.BlockSpec((1, tm, d), lambda bi, i: (bi, 0, 0)),
            pl.BlockSpec((1, tm, d), lambda bi, i: (bi, jnp.maximum(i - 1, 0), 0))]


def _stream_tile(x_refs):
    if len(x_refs) == 1:
        return x_refs[0][0]
    return jnp.where(pl.program_id(1) == 0, x_refs[0][0], x_refs[1][0])


def _inproj_kernel(*refs, n_q, n_k, n_v, d_v, k_pairs):
    _inproj_body(_stream_tile(refs[:-11]), *refs[-11:], n_q=n_q, n_k=n_k, n_v=n_v, d_v=d_v, k_pairs=k_pairs)


def _inproj_body(x, mod_ref, nw_ref, w_ref, wvt_ref, g_ref, cos_ref, sin_ref, bd_ref, q_ref, k_ref, vt_ref,
                 *, n_q, n_k, n_v, d_v, k_pairs):
    tm = q_ref.shape[1]
    h = _modulated_norm(x, nw_ref[...], mod_ref[0, 0, 0:1, :], mod_ref[0, 0, 1:2, :])
    hb = h.astype(BF16)
    y = _dot(hb, w_ref[...])
    cos = cos_ref[...]
    sin = sin_ref[...]
    lane = lax.broadcasted_iota(jnp.int32, (tm, LANES), 1)
    first_half = (lane % HEAD_DIM) < (HEAD_DIM // 2)
    n_norm = n_q + n_k * HEAD_DIM
    for c in range(n_norm // MXU_TILE):
        yc = y[:, MXU_TILE * c:MXU_TILE * (c + 1)]
        ss = _dot((yc * yc).astype(BF16), bd_ref[...])
        z = yc * lax.rsqrt(ss * (1.0 / HEAD_DIM) + EPS) * g_ref[:, MXU_TILE * c:MXU_TILE * (c + 1)]
        for half in range(MXU_TILE // LANES):
            zc = z[:, LANES * half:LANES * (half + 1)]
            partner = jnp.where(first_half, pltpu.roll(zc, LANES - HEAD_DIM // 2, 1),
                                pltpu.roll(zc, HEAD_DIM // 2, 1))
            o = zc * cos + partner * sin
            col = MXU_TILE * c + LANES * half
            if col < n_q:
                q_ref[0, :, col:col + LANES] = (o * Q_SCALE).astype(BF16)
            else:
                kh = (col - n_q) // HEAD_DIM
                if k_pairs:
                    k_ref[0, kh // 2] = o.astype(BF16)
                else:
                    k_ref[0, kh] = o[:, :HEAD_DIM].astype(BF16)
                    k_ref[0, kh + 1] = o[:, HEAD_DIM:].astype(BF16)
    vt = _nt_dot(wvt_ref[...], hb).astype(BF16)
    ones_rows = (lax.broadcasted_iota(jnp.int32, (SUM_ROWS, tm), 0) == 0).astype(BF16)
    for hv in range(n_v):
        vt_ref[0, hv, 0:d_v, :] = vt[d_v * hv:d_v * (hv + 1), :]
        vt_ref[0, hv, d_v:d_v + SUM_ROWS, :] = ones_rows


def _inproj(xs, mod_tab, nw, w, wvt, gains, cos_t, sin_t, bd, *, n_k, n_v, d_v, k_pairs=False):
    b, d = xs[0].shape[0], xs[0].shape[2]
    tb = sum(a.shape[1] for a in xs)
    n_q = D_MODEL
    n_in = w.shape[1]
    tm = ROW_TILE
    kern = functools.partial(_inproj_kernel, n_q=n_q, n_k=n_k, n_v=n_v, d_v=d_v, k_pairs=k_pairs)
    k_blocks, k_width = (n_k // 2, 2 * HEAD_DIM) if k_pairs else (n_k, HEAD_DIM)
    in_specs, out_specs, out_shape = _inproj_specs(b, tb, d, w, wvt, gains, k_blocks, k_width, n_v, d_v)
    return pl.pallas_call(
        kern,
        grid=(b, tb // tm),
        in_specs=_stream_specs(xs, tm) + in_specs,
        out_specs=out_specs,
        out_shape=out_shape,
        compiler_params=_params(("arbitrary", "arbitrary")),
        name="inproj",
    )(*xs, mod_tab, nw, w, wvt, gains, cos_t, sin_t, bd)


def _resident(shape):
    return pl.BlockSpec(shape, lambda bi, i: (0,) * len(shape), pipeline_mode=pl.Buffered(1))


def _inproj_specs(b, tb, d, w, wvt, gains, k_blocks, k_width, n_v, d_v):
    tm = ROW_TILE
    in_specs = [
        pl.BlockSpec((1, 1, MOD_ROWS, d), lambda bi, i: (bi, jnp.minimum(i, 1), 0, 0)),
        _resident((1, d)), _resident(w.shape), _resident(wvt.shape), _resident(gains.shape),
        pl.BlockSpec((tm, LANES), lambda bi, i: (i, 0)),
        pl.BlockSpec((tm, LANES), lambda bi, i: (i, 0)),
        _resident((MXU_TILE, MXU_TILE)),
    ]
    out_specs = [
        pl.BlockSpec((1, tm, D_MODEL), lambda bi, i: (bi, i, 0)),
        pl.BlockSpec((1, k_blocks, tm, k_width), lambda bi, i: (bi, 0, i, 0)),
        pl.BlockSpec((1, n_v, d_v + SUM_ROWS, tm), lambda bi, i: (bi, 0, 0, i)),
    ]
    out_shape = [
        jax.ShapeDtypeStruct((b, tb, D_MODEL), BF16),
        jax.ShapeDtypeStruct((b, k_blocks, tb, k_width), BF16),
        jax.ShapeDtypeStruct((b, n_v, d_v + SUM_ROWS, tb), BF16),
    ]
    return in_specs, out_specs, out_shape


def _softmax_step(s_ref, p_ref, m, chunk_max):
    m_new = chunk_max if m is None else jnp.maximum(m, chunk_max)
    for r in range(0, s_ref.shape[0], SOFTMAX_ROWS):
        p_ref[r:r + SOFTMAX_ROWS, :] = jnp.exp2(s_ref[r:r + SOFTMAX_ROWS, :] - m_new).astype(BF16)
    return m_new, (None if m is None else jnp.exp2(m - m_new))


class _Stream:
    def __init__(self, chunks, score, vt_slice, s_bufs, p_bufs):
        self.chunks, self.score, self.vt_slice = chunks, score, vt_slice
        self.s_bufs, self.p_bufs = s_bufs, p_bufs
        self.m = self.acc = None
        self.cmax = {}

    def buf(self, refs, c):
        return refs[c % len(refs)].at[0:self.chunks[c][1], :]

    def write_scores(self, c):
        if c < len(self.chunks):
            s = self.score(*self.chunks[c])
            self.buf(self.s_bufs, c)[...] = s
            self.cmax[c] = jnp.max(s, axis=0, keepdims=True)

    def step(self, c):
        if c >= len(self.chunks):
            return
        self.write_scores(c + 2)
        lo, size, _ = self.chunks[c]
        p_ref = self.buf(self.p_bufs, c)
        self.m, alpha = _softmax_step(self.buf(self.s_bufs, c), p_ref, self.m, self.cmax.pop(c))
        pv = _dot(self.vt_slice(lo, size), p_ref[...])
        self.acc = pv if alpha is None else alpha * self.acc + pv


def _run_streams(streams):
    for c in range(2):
        for st in streams:
            st.write_scores(c)
    for c in range(max(len(st.chunks) for st in streams)):
        for st in streams:
            st.step(c)


def _key_chunks(n_keys, tk):
    return [(0, CTX_LEN, "ctx")] + [(lo, tk, "lat") for lo in range(CTX_LEN, n_keys, tk)]


def _stream_scratch(n_keys, nq, n_streams, n_bufs):
    n = n_streams * n_bufs
    return [pltpu.VMEM((n_keys, nq), F32)] * n + [pltpu.VMEM((n_keys, nq), BF16)] * n


def _split_scratch(scratch, n_streams, n_bufs):
    n = n_streams * n_bufs
    return [(scratch[j * n_bufs:(j + 1) * n_bufs], scratch[n + j * n_bufs:n + (j + 1) * n_bufs])
            for j in range(n_streams)]


def _layer0_attention_kernel(qa_ref, qb_ref, ka_ref, vta_ref, kb_ref, vtb_ref, sink_ref, oa_ref, ob_ref,
                             *scratch, tk):
    tq = qa_ref.shape[1]
    tb = ka_ref.shape[2]
    span = tq + 2 * WINDOW
    t = pl.program_id(2)
    n_dense = 2 * 4 * FLASH_BUFFERS
    dense_bufs = _split_scratch(scratch[:n_dense], 4, FLASH_BUFFERS)
    window_bufs = _split_scratch(scratch[n_dense:], 4, WINDOW_BUFFERS)

    def run(chunks):
        qa = qa_ref[0]
        ws = pl.multiple_of(jnp.clip(tq * t - WINDOW, 0, tb - span), LANES)
        kpos = ws - CTX_LEN + lax.broadcasted_iota(jnp.int32, (span, tq), 0)
        qpos = tq * t - CTX_LEN + lax.broadcasted_iota(jnp.int32, (span, tq), 1)
        ok = (jnp.abs(qpos - kpos) <= WINDOW) & (kpos >= 0) & (qpos >= 0)

        def window_score(g):
            qh = qa[:, HEAD_DIM * g:HEAD_DIM * (g + 1)]

            def score(lo, size, tag):
                s = _nt_dot(ka_ref[0, 0, pl.ds(lo, size), :], qh)
                return jnp.where(ok, s, NEG_INF) if tag == "win" else s
            return score

        win = [_Stream([(0, CTX_LEN, "ctx"), (ws, span, "win")], window_score(g),
                       lambda lo, size: vta_ref[0, 0, :, pl.ds(lo, size)], *window_bufs[g])
               for g in range(4)]

        qb = qb_ref[0]
        qts = [qb[:, HEAD_DIM * g:HEAD_DIM * (g + 1)].astype(F32).T.astype(BF16) for g in range(4)]
        dense = [_Stream(chunks, lambda lo, size, tag, qt=qt: _dot(kb_ref[0, 0, lo:lo + size, :], qt),
                         lambda lo, size: vtb_ref[0, 0, :, lo:lo + size], *dense_bufs[g])
                 for g, qt in enumerate(qts)]

        _run_streams(win + dense)
        for g, st in enumerate(win):
            sink = sink_ref[0, :, tq * g:tq * (g + 1)]
            m_all = jnp.maximum(st.m, sink)
            scale = jnp.exp2(st.m - m_all)
            l = st.acc[HEAD_DIM:HEAD_DIM + 1, :] * scale + jnp.exp2(sink - m_all)
            oa_ref[0, g] = (st.acc[:HEAD_DIM, :] * scale / l).astype(BF16)
        for g, st in enumerate(dense):
            ob_ref[0, g] = (st.acc[:HEAD_DIM, :] / st.acc[HEAD_DIM:HEAD_DIM + 1, :]).astype(BF16)

    chunks = _key_chunks(tb, tk)
    is_latent = t * tq >= CTX_LEN
    pl.when(is_latent)(lambda: run(chunks))
    pl.when(jnp.logical_not(is_latent))(lambda: run(chunks[:1]))


def _layer0_attention(q, k, vt, sink_rows, *, tq, tk):
    b, tb, _ = q.shape
    assert (tb - CTX_LEN) % tk == 0

    def kv_specs(first_head):
        return [pl.BlockSpec((1, 1, tb, HEAD_DIM), lambda bi, h, i: (bi, first_head + h, 0, 0)),
                pl.BlockSpec((1, 1, HEAD_DIM + SUM_ROWS, tb), lambda bi, h, i: (bi, first_head + h, 0, 0))]

    out = jax.ShapeDtypeStruct((b, 8, HEAD_DIM, tb), BF16)
    return pl.pallas_call(
        functools.partial(_layer0_attention_kernel, tk=tk),
        grid=(b, 2, tb // tq),
        in_specs=[pl.BlockSpec((1, tq, 4 * HEAD_DIM), lambda bi, h, i: (bi, i, h)),
                  pl.BlockSpec((1, tq, 4 * HEAD_DIM), lambda bi, h, i: (bi, i, 2 + h))]
        + kv_specs(0) + kv_specs(2) + [pl.BlockSpec((1, 1, 4 * tq), lambda bi, h, i: (h, 0, 0))],
        out_specs=[pl.BlockSpec((1, 4, HEAD_DIM, tq), lambda bi, h, i: (bi, h, 0, i))] * 2,
        out_shape=[out, out],
        scratch_shapes=(_stream_scratch(tk, tq, 4, FLASH_BUFFERS)
                        + _stream_scratch(tq + 2 * WINDOW, tq, 4, WINDOW_BUFFERS)),
        compiler_params=_params(("arbitrary", "arbitrary", "arbitrary")),
        name="layer0_attention",
    )(q, q, k, vt, k, vt, sink_rows)


def _diff_kernel(qa_ref, qb_ref, k_ref, vt_ref, lam_ref, subw_ref, o_ref, *scratch, tk, lam_init):
    tb = k_ref.shape[2]
    d_v = 2 * HEAD_DIM
    chunks = _key_chunks(tb, tk)
    bufs = _split_scratch(scratch, 4 * DIFF_HEADS, FLASH_BUFFERS)
    lane = lax.broadcasted_iota(jnp.int32, (qa_ref.shape[1], 2 * HEAD_DIM), 1)
    lam = (jnp.exp(jnp.sum(lam_ref[0:1, :] * lam_ref[1:2, :], axis=-1, keepdims=True))
           - jnp.exp(jnp.sum(lam_ref[2:3, :] * lam_ref[3:4, :], axis=-1, keepdims=True)) + lam_init)
    for hh in range(DIFF_HEADS):
        streams = []
        for a in range(2):
            for q_ref in (qa_ref, qb_ref):
                q = q_ref[0, :, 2 * HEAD_DIM * hh:2 * HEAD_DIM * (hh + 1)]
                qh = jnp.where(lane // HEAD_DIM == a, q, jnp.zeros_like(q))
                streams.append(_Stream(
                    chunks, lambda lo, size, tag, qh=qh, hh=hh: _nt_dot(k_ref[0, hh, lo:lo + size, :], qh),
                    lambda lo, size, hh=hh: vt_ref[0, hh, :, lo:lo + size], *bufs[4 * hh + len(streams)]))
        _run_streams(streams)
        ots = [st.acc[:d_v, :] / st.acc[d_v:d_v + 1, :] for st in streams]
        ots = [jnp.concatenate(ots[0:2], axis=1), jnp.concatenate(ots[2:4], axis=1)]
        o = ots[0] - lam * ots[1]
        ms = jnp.mean(o * o, axis=0, keepdims=True)
        o = (o * lax.rsqrt(ms + EPS) * subw_ref[...]) * (1.0 - lam_init)
        o_ref[0, hh] = o.astype(BF16)


def _diff_attention(q, k, vt, lam_vecs, subw, *, tq, tk, lam_init):
    b, tb, _ = q.shape
    n_lat = tb - CTX_LEN
    n_heads = vt.shape[1]
    d_v = vt.shape[2] - SUM_ROWS
    hq = tq // 2
    assert CTX_LEN % hq == 0
    q_off = CTX_LEN // hq
    return pl.pallas_call(
        functools.partial(_diff_kernel, tk=tk, lam_init=lam_init),
        grid=(b, n_heads // DIFF_HEADS, n_lat // tq),
        in_specs=[
            pl.BlockSpec((1, hq, 2 * HEAD_DIM * DIFF_HEADS), lambda bi, h, i: (bi, 2 * i + q_off, h)),
            pl.BlockSpec((1, hq, 2 * HEAD_DIM * DIFF_HEADS), lambda bi, h, i: (bi, 2 * i + 1 + q_off, h)),
            pl.BlockSpec((1, DIFF_HEADS, tb, 2 * HEAD_DIM), lambda bi, h, i: (bi, h, 0, 0)),
            pl.BlockSpec((1, DIFF_HEADS, d_v + SUM_ROWS, tb), lambda bi, h, i: (bi, h, 0, 0)),
            pl.BlockSpec((4, HEAD_DIM), lambda bi, h, i: (0, 0)),
            pl.BlockSpec((d_v, 1), lambda bi, h, i: (0, 0)),
        ],
        out_specs=pl.BlockSpec((1, DIFF_HEADS, d_v, tq), lambda bi, h, i: (bi, h, 0, i)),
        out_shape=jax.ShapeDtypeStruct((b, n_heads, d_v, n_lat), BF16),
        scratch_shapes=_stream_scratch(tk, tq // 2, 4 * DIFF_HEADS, FLASH_BUFFERS),
        compiler_params=_params(("arbitrary", "arbitrary", "arbitrary")),
        name="diff_attention",
    )(q, q, k, vt, lam_vecs, subw)


def _post_kernel(*refs, n_attn):
    n_x = len(refs) - n_attn - 6
    out_ref = refs[-1]
    o_tiles = [r[0] for r in refs[n_x + 1:n_x + 1 + n_attn]]
    out_ref[0] = _post_body(_stream_tile(refs[:n_x]), refs[n_x], o_tiles, *refs[-5:-1])


def _post_pair_kernel(xa_ref, xb_ref, mod_ref, *rest, n_attn):
    o_refs = rest[:n_attn]
    wo_ref, nw_ref, wi_ref, wf_ref, out_ref = rest[n_attn:]
    tm = xa_ref.shape[1]
    for r, x_ref in enumerate((xa_ref, xb_ref)):
        o_tiles = [o[0, :, :, tm * r:tm * (r + 1)] for o in o_refs]
        out_ref[0, tm * r:tm * (r + 1), :] = _post_body(x_ref[0], mod_ref, o_tiles, wo_ref, nw_ref, wi_ref, wf_ref)


def _post_inproj_kernel(*refs, n_attn, n_x, inproj_cfg):
    n_post = n_x + 1 + n_attn + 4
    o_tiles = [r[0] for r in refs[n_x + 1:n_x + 1 + n_attn]]
    x2 = _post_body(_stream_tile(refs[:n_x]), refs[n_x], o_tiles, *refs[n_post - 4:n_post])
    out_ref = refs[n_post + 8]
    out_ref[0] = x2
    _inproj_body(x2, *refs[n_post:n_post + 8], *refs[n_post + 9:], **inproj_cfg)


def _post_body(x, mod_ref, o_tiles, wo_ref, nw_ref, wi_ref, wf_ref):
    n_attn = len(o_tiles)
    mod = mod_ref[0, 0]
    kw = wo_ref.shape[0] // n_attn
    a = None
    for j, o in enumerate(o_tiles):
        ot = o.reshape(kw, o.shape[2])
        part = lax.dot_general(ot, wo_ref[kw * j:kw * (j + 1), :], (((0,), (0,)), ((), ())),
                               preferred_element_type=F32)
        a = part if a is None else a + part
    x1 = x + mod[2:3, :] * a
    h = _modulated_norm(x1, nw_ref[...], mod[3:4, :], mod[4:5, :])
    u = _dot(h.astype(BF16), wi_ref[...])
    f = wf_ref.shape[0]
    gate = u[:, :f]
    act = (gate / (1.0 + jnp.exp(-gate))) * u[:, f:]
    y = _dot(act.astype(BF16), wf_ref[...])
    return x1 + mod[5:6, :] * y


def _post(xs, mod_tab, attn_outs, wo, nw, wi, wf, *, latent_only, next_inproj=None):
    b, d = xs[0].shape[0], xs[0].shape[2]
    tb = sum(a.shape[1] for a in xs)
    tm = ROW_TILE
    off = CTX_LEN // tm if latent_only else 0
    n_rows = tb - CTX_LEN if latent_only else tb
    n_attn = len(attn_outs)

    in_specs = _stream_specs(xs, tm, off) + [
        pl.BlockSpec((1, 1, MOD_ROWS, d), lambda bi, i: (bi, jnp.minimum(i + off, 1), 0, 0)),
    ]
    for o in attn_outs:
        in_specs.append(pl.BlockSpec((1, o.shape[1], o.shape[2], tm), lambda bi, i: (bi, 0, 0, i)))
    in_specs += [_resident(wo.shape), _resident(nw.shape), _resident(wi.shape), _resident(wf.shape)]
    x_spec = pl.BlockSpec((1, tm, d), lambda bi, i: (bi, i, 0))
    x_shape = jax.ShapeDtypeStruct((b, n_rows, d), F32)
    if next_inproj is not None:
        assert not latent_only
        *operands, cfg = next_inproj
        k_blocks, k_width = (cfg["n_k"] // 2, 2 * HEAD_DIM) if cfg["k_pairs"] else (cfg["n_k"], HEAD_DIM)
        p_in, p_out, p_shape = _inproj_specs(b, tb, d, operands[2], operands[3], operands[4], k_blocks, k_width,
                                             cfg["n_v"], cfg["d_v"])
        return pl.pallas_call(
            functools.partial(_post_inproj_kernel, n_attn=n_attn, n_x=len(xs), inproj_cfg=dict(cfg, n_q=D_MODEL)),
            grid=(b, n_rows // tm),
            in_specs=in_specs + p_in,
            out_specs=[x_spec] + p_out,
            out_shape=[x_shape] + p_shape,
            compiler_params=_params(("arbitrary", "arbitrary")),
            name="post_inproj",
        )(*xs, mod_tab, *attn_outs, wo, nw, wi, wf, *operands)
    if latent_only:
        assert len(xs) == 1 and n_rows % (2 * tm) == 0
        pair_specs = [pl.BlockSpec((1, tm, d), lambda bi, i: (bi, 2 * i + off, 0)),
                      pl.BlockSpec((1, tm, d), lambda bi, i: (bi, 2 * i + 1 + off, 0)),
                      pl.BlockSpec((1, 1, MOD_ROWS, d), lambda bi, i: (bi, 1, 0, 0))]
        pair_specs += [pl.BlockSpec((1, o.shape[1], o.shape[2], 2 * tm), lambda bi, i: (bi, 0, 0, i))
                       for o in attn_outs]
        return pl.pallas_call(
            functools.partial(_post_pair_kernel, n_attn=n_attn),
            grid=(b, n_rows // (2 * tm)),
            in_specs=pair_specs + in_specs[-4:],
            out_specs=pl.BlockSpec((1, 2 * tm, d), lambda bi, i: (bi, i, 0)),
            out_shape=x_shape,
            compiler_params=_params(("arbitrary", "arbitrary")),
            name="post_pair",
        )(xs[0], xs[0], mod_tab, *attn_outs, wo, nw, wi, wf)
    return pl.pallas_call(
        functools.partial(_post_kernel, n_attn=n_attn),
        grid=(b, n_rows // tm),
        in_specs=in_specs,
        out_specs=x_spec,
        out_shape=x_shape,
        compiler_params=_params(("arbitrary", "arbitrary")),
        name="post",
    )(*xs, mod_tab, *attn_outs, wo, nw, wi, wf)


def _deinterleave_perm(n_heads):
    one = jnp.concatenate([jnp.arange(0, HEAD_DIM, 2), jnp.arange(1, HEAD_DIM, 2)])
    return (jnp.arange(n_heads)[:, None] * HEAD_DIM + one[None, :]).reshape(-1)


def _rope_tables(n_lat):
    rows = n_lat // GRID_W
    row = jnp.repeat(jnp.arange(rows, dtype=F32), GRID_W)
    col = jnp.tile(jnp.arange(GRID_W, dtype=F32), rows)
    n_freq = HEAD_DIM // 4
    inv = ROPE_THETA ** (-jnp.arange(n_freq, dtype=F32) / n_freq)
    ang = jnp.concatenate([row[:, None] * inv, col[:, None] * inv], axis=-1)
    cos, sin = jnp.cos(ang), jnp.sin(ang)
    cos = jnp.concatenate([jnp.ones((CTX_LEN, HEAD_DIM // 2), F32), cos], axis=0)
    sin = jnp.concatenate([jnp.zeros((CTX_LEN, HEAD_DIM // 2), F32), sin], axis=0)
    cos_t = jnp.tile(jnp.concatenate([cos, cos], axis=-1), (1, 2))
    sin_t = jnp.tile(jnp.concatenate([-sin, sin], axis=-1), (1, 2))
    return cos_t, sin_t


def _gain_row(parts):
    one = _deinterleave_perm(1)
    return jnp.concatenate([jnp.tile(g[one], n) for g, n in parts])[None, :].astype(F32)


def kernel(x, c, ctx, c_ctx, mod_w, mod_b, norm_mix_w, norm_ffn_w, ev_w_in, ev_w_out, ev_qn_a, ev_kn_a,
           ev_qn_b, ev_kn_b, ev_sink_a, od_w_in, od_w_out, od_qn, od_kn, od_lq1, od_lk1, od_lq2, od_lk2,
           od_subln, ffn_w_in, ffn_w_out):
    b, n_lat, d = x.shape
    assert d == D_MODEL and ctx.shape[1] == CTX_LEN and b < MOD_ROWS
    assert n_lat % (DIFF_QUERY_TILES * QUERY_TILE) == 0 and n_lat % KEY_CHUNK == 0 and n_lat % GRID_W == 0
    hd = HEAD_DIM

    cc = jnp.zeros((MOD_ROWS, d), F32).at[:b].set(c).at[b].set(c_ctx)
    mod = _modulation(cc, mod_w, mod_b).reshape(DEPTH, MOD_ROWS, 6, d)
    mod_lat = mod[:, :b]
    mod_ctx = jnp.broadcast_to(mod[:, b:b + 1], mod_lat.shape)
    mod_tab = jnp.stack([mod_ctx, mod_lat], axis=2)
    mod_tab = jnp.pad(mod_tab, ((0, 0), (0, 0), (0, 0), (0, MOD_ROWS - 6), (0, 0)))

    cos_t, sin_t = _rope_tables(n_lat)
    bd = jnp.kron(jnp.eye(MXU_TILE // hd, dtype=F32), jnp.ones((hd, hd), F32)).astype(BF16)

    w = ev_w_in[0]
    widths = [8 * hd, 2 * hd, 2 * hd, 8 * hd, 2 * hd, 2 * hd]
    qa, ka, va, qb, kb, vb = jnp.split(w, [sum(widths[:n]) for n in range(1, 6)], axis=1)
    p8, p2 = _deinterleave_perm(8), _deinterleave_perm(2)
    w0 = jnp.concatenate([qa[:, p8], qb[:, p8], ka[:, p2], kb[:, p2]], axis=1).astype(BF16)
    wvt0 = jnp.concatenate([va, vb], axis=1).T.astype(BF16)
    g0 = _gain_row([(ev_qn_a[0], 8), (ev_qn_b[0], 8), (ev_kn_a[0], 2), (ev_kn_b[0], 2)])
    q0, k0, vt0 = _inproj((ctx, x), mod_tab[0], norm_mix_w[0][None, :], w0, wvt0, g0, cos_t, sin_t, bd,
                          n_k=4, n_v=4, d_v=hd)
    sink_rows = jnp.repeat(ev_sink_a[0].astype(F32) * LOG2E, QUERY_TILE).reshape(2, 1, 4 * QUERY_TILE)
    o_a, o_b = _layer0_attention(q0, k0, vt0, sink_rows, tq=QUERY_TILE, tk=KEY_CHUNK)
    w = od_w_in[0]
    p16 = _deinterleave_perm(16)
    n_qk = 16 * hd
    w1 = jnp.concatenate([w[:, :n_qk][:, p16], w[:, n_qk:2 * n_qk][:, p16]], axis=1).astype(BF16)
    wvt1 = w[:, 2 * n_qk:].T.astype(BF16)
    g1 = _gain_row([(od_qn[0], 16), (od_kn[0], 16)])
    xs, q1, k1, vt1 = _post((ctx, x), mod_tab[0], [o_a, o_b], ev_w_out[0].astype(BF16), norm_ffn_w[0][None, :],
                            ffn_w_in[0].astype(BF16), ffn_w_out[0].astype(BF16), latent_only=False,
                            next_inproj=(mod_tab[1], norm_mix_w[1][None, :], w1, wvt1, g1, cos_t, sin_t, bd,
                                         dict(n_k=16, n_v=8, d_v=2 * hd, k_pairs=True)))
    lam_init = 0.8 - 0.6 * math.exp(-0.3 * 1)
    lam_vecs = jnp.stack([od_lq1[0], od_lk1[0], od_lq2[0], od_lk2[0]]).astype(F32)
    o_c = _diff_attention(q1, k1, vt1, lam_vecs, od_subln[0].astype(F32)[:, None],
                          tq=DIFF_QUERY_TILES * QUERY_TILE, tk=KEY_CHUNK, lam_init=lam_init)
    return _post((xs,), mod_tab[1], [o_c], od_w_out[0].astype(BF16), norm_ffn_w[1][None, :],
                 ffn_w_in[1].astype(BF16), ffn_w_out[1].astype(BF16), latent_only=True)
```
